```python
import math
import jax
import jax.numpy as jnp
from jax import lax
import numpy as np

D_MODEL = 1024
BATCH = 2
SEQ = 8192
DEPTH = 2

CTX_LEN = 256
GRID_W = 64
MIX_W = D_MODEL
FOURIER_W = MIX_W // 4
FOURIER_GROUPS = 4
FOURIER_GD = FOURIER_W // FOURIER_GROUPS
POOL_W = MIX_W // 4
POOL_WINDOWS = (2, 4, 8, 16)
POOL_GD = POOL_W // len(POOL_WINDOWS)
ATTN_W = MIX_W - FOURIER_W - POOL_W
QK_DIM = 64
V_DIM = 2 * QK_DIM
N_HEADS = ATTN_W // V_DIM
QK_W = N_HEADS * 2 * QK_DIM
Q_BLOCK = 128
ROPE_THETA = 10000.0
EPS = 1e-6

A_OFF = 0
B_OFF = A_OFF + FOURIER_W
Q_OFF = B_OFF + POOL_W
K_OFF = Q_OFF + QK_W
V_OFF = K_OFF + QK_W
G_OFF = V_OFF + ATTN_W
IN_W = G_OFF + MIX_W

kernel_name = "hybrid_fourier_pool_diffattn_dit"


def rms_norm(t, g):
    tf = t.astype(jnp.float32)
    y = tf * lax.rsqrt(jnp.mean(tf * tf, axis=-1, keepdims=True) + EPS)
    return (y * g.astype(jnp.float32)).astype(t.dtype)


def axial_rope_tables(rows):
    row = jnp.repeat(jnp.arange(rows), GRID_W).astype(jnp.float32)
    col = jnp.tile(jnp.arange(GRID_W), rows).astype(jnp.float32)
    half = QK_DIM // 2
    inv_freq = ROPE_THETA ** (-jnp.arange(0, half, 2, dtype=jnp.float32) / half)
    ang_r = row[:, None] * inv_freq[None, :]
    ang_c = col[:, None] * inv_freq[None, :]
    ang = jnp.concatenate([ang_r, ang_r, ang_c, ang_c], axis=-1)
    return jnp.cos(ang), jnp.sin(ang)


def apply_rope(t, cos, sin):
    tr = t.reshape(t.shape[:-1] + (2, 2, QK_DIM // 4))
    rot = jnp.stack([-tr[..., 1, :], tr[..., 0, :]], axis=-2).reshape(t.shape)
    return (t * cos + rot * sin).astype(t.dtype)


def centred_window_mean(u, w):
    L = u.shape[1]
    lo = w // 2
    hi = w - lo - 1
    cs = jnp.pad(jnp.cumsum(u.astype(jnp.float32), axis=1), ((0, 0), (1, 0), (0, 0)))
    t = jnp.arange(L)
    a = jnp.clip(t - lo, 0, L - 1)
    b = jnp.clip(t + hi, 0, L - 1)
    s = cs[:, b + 1] - cs[:, a]
    cnt = (b - a + 1).astype(jnp.float32)
    return (s / cnt[None, :, None]).astype(u.dtype)


def fourier_mix(a, w_fourier):
    B_, L = a.shape[:2]
    ag = a.reshape(B_, L, FOURIER_GROUPS, FOURIER_GD).astype(jnp.float32)
    f = jnp.fft.fftn(ag, axes=(1, 3), norm='ortho').real.astype(a.dtype)
    y = jnp.einsum('blgc,gcd->blgd', f, w_fourier)
    return y.reshape(B_, L, FOURIER_W)


def pool_mix(b, w_pool, pool_scale):
    B_, L = b.shape[:2]
    bg = b.reshape(B_, L, len(POOL_WINDOWS), POOL_GD)
    pooled = jnp.stack([centred_window_mean(bg[:, :, gi], w) for gi, w in enumerate(POOL_WINDOWS)], axis=2) - bg
    y = jnp.einsum('blgc,gcd->blgd', pooled, w_pool).reshape(B_, L, POOL_W)
    return y * pool_scale


def split_heads_qk(t):
    B_, L = t.shape[:2]
    return t.reshape(B_, L, N_HEADS, 2, QK_DIM).transpose(0, 2, 3, 1, 4)


def split_heads_v(t):
    B_, L = t.shape[:2]
    return t.reshape(B_, L, N_HEADS, V_DIM).transpose(0, 2, 1, 3)


def diff_attend(q, k, v, lam):
    s = jnp.einsum('bhmqd,bhmkd->bhmqk', q, k).astype(jnp.float32) * (QK_DIM ** -0.5)
    p = jax.nn.softmax(s, axis=-1)
    p = p[:, :, 0] - lam * p[:, :, 1]
    return jnp.einsum('bhqk,bhkd->bhqd', p.astype(v.dtype), v)


def attn_post(o, subln_g, lam_init):
    o = rms_norm(o, subln_g) * (1.0 - lam_init)
    B_, H, L, dv = o.shape
    return o.transpose(0, 2, 1, 3).reshape(B_, L, H * dv)


def mixer_out(a_in, b_in, attn, gate, w_fourier, w_pool, pool_scale, w_out):
    y = jnp.concatenate([fourier_mix(a_in, w_fourier), pool_mix(b_in, w_pool, pool_scale), attn], axis=-1)
    return (y * jax.nn.silu(gate)) @ w_out


def mixer_layer(layer_idx, x, ctx, c, c_ctx, norm_g, w_mod, b_mod, w_in, w_fourier, w_pool,
                pool_scale, qk_norm_g, lam_vecs, subln_g, w_out, rope_cos, rope_sin, update_ctx):
    shift, scale, gate = jnp.split(jax.nn.silu(c) @ w_mod + b_mod, 3, axis=-1)
    shift_c, scale_c, gate_c = jnp.split(jax.nn.silu(c_ctx) @ w_mod + b_mod, 3, axis=-1)
    h = rms_norm(x, norm_g) * (1 + scale[:, None]) + shift[:, None]
    hc = rms_norm(ctx, norm_g) * (1 + scale_c) + shift_c

    lam_init = 0.8 - 0.6 * math.exp(-0.3 * layer_idx)
    lv = lam_vecs.astype(jnp.float32)
    lam = jnp.exp(jnp.sum(lv[0] * lv[1])) - jnp.exp(jnp.sum(lv[2] * lv[3])) + lam_init

    p = h @ w_in
    a_in, b_in = p[..., A_OFF:B_OFF], p[..., B_OFF:Q_OFF]
    q, k, v, g = p[..., Q_OFF:K_OFF], p[..., K_OFF:V_OFF], p[..., V_OFF:G_OFF], p[..., G_OFF:]

    if update_ctx:
        pc = hc @ w_in
        kc, vc = pc[..., K_OFF:V_OFF], pc[..., V_OFF:G_OFF]
    else:
        pkv = hc @ w_in[:, K_OFF:G_OFF]
        kc, vc = pkv[..., :QK_W], pkv[..., QK_W:]

    q_lat = apply_rope(rms_norm(split_heads_qk(q), qk_norm_g[0]), rope_cos, rope_sin)
    k_lat = apply_rope(rms_norm(split_heads_qk(k), qk_norm_g[1]), rope_cos, rope_sin)
    k_ctx = rms_norm(split_heads_qk(kc), qk_norm_g[1])
    v_ctx = split_heads_v(vc)
    k_all = jnp.concatenate([k_ctx, k_lat], axis=3)
    v_all = jnp.concatenate([v_ctx, split_heads_v(v)], axis=2)

    B_, H, _, L, d = q_lat.shape
    nb = L // Q_BLOCK
    qb = jnp.moveaxis(q_lat.reshape(B_, H, 2, nb, Q_BLOCK, d), 3, 0)
    ob = lax.map(lambda qblk: diff_attend(qblk, k_all, v_all, lam), qb)
    o = jnp.moveaxis(ob, 0, 2).reshape(B_, H, L, V_DIM)
    attn = attn_post(o, subln_g, lam_init)

    x_new = x + gate[:, None] * mixer_out(a_in, b_in, attn, g, w_fourier, w_pool, pool_scale, w_out)

    if update_ctx:
        qc = rms_norm(split_heads_qk(pc[..., Q_OFF:K_OFF]), qk_norm_g[0])
        attn_c = attn_post(diff_attend(qc, k_ctx, v_ctx, lam), subln_g, lam_init)
        ctx = ctx + gate_c * mixer_out(pc[..., A_OFF:B_OFF], pc[..., B_OFF:Q_OFF], attn_c,
                                       pc[..., G_OFF:], w_fourier, w_pool, pool_scale, w_out)
    return x_new, ctx


def setup_inputs(seed: int = 0) -> dict:
    key = jax.random.key(seed)
    ks = jax.random.split(key, 16)
    f32 = jnp.float32
    nrm = lambda k, shape: jax.random.normal(k, shape, dtype=f32)
    return {
        'x': nrm(ks[0], (BATCH, SEQ, D_MODEL)),
        'c': nrm(ks[1], (BATCH, D_MODEL)),
        'ctx': nrm(ks[2], (BATCH, CTX_LEN, D_MODEL)),
        'c_ctx': nrm(ks[3], (D_MODEL,)),
        'norm_g': 1.0 + 0.02 * nrm(ks[4], (DEPTH, D_MODEL)),
        'w_mod': nrm(ks[5], (DEPTH, D_MODEL, 3 * D_MODEL)) * D_MODEL ** -0.5,
        'b_mod': 0.02 * nrm(ks[6], (DEPTH, 3 * D_MODEL)),
        'w_in': nrm(ks[7], (DEPTH, D_MODEL, IN_W)) * D_MODEL ** -0.5,
        'w_fourier': nrm(ks[8], (DEPTH, FOURIER_GROUPS, FOURIER_GD, FOURIER_GD)) * FOURIER_GD ** -0.5,
        'w_pool': nrm(ks[9], (DEPTH, len(POOL_WINDOWS), POOL_GD, POOL_GD)) * POOL_GD ** -0.5,
        'pool_scale': 1.0 + 0.1 * nrm(ks[10], (DEPTH, POOL_W)),
        'qk_norm_g': 1.0 + 0.02 * nrm(ks[11], (DEPTH, 2, QK_DIM)),
        'lam_vecs': 0.1 * nrm(ks[12], (DEPTH, 4, QK_DIM)),
        'subln_g': 1.0 + 0.02 * nrm(ks[13], (DEPTH, V_DIM)),
        'w_out': nrm(ks[14], (DEPTH, MIX_W, D_MODEL)) * MIX_W ** -0.5,
    }


def reference(x, c, ctx, c_ctx, norm_g, w_mod, b_mod, w_in, w_fourier, w_pool, pool_scale,
              qk_norm_g, lam_vecs, subln_g, w_out):
    rows = x.shape[1] // GRID_W
    rope_cos, rope_sin = axial_rope_tables(rows)
    for l in range(DEPTH):
        x, ctx = mixer_layer(l, x, ctx, c, c_ctx, norm_g[l], w_mod[l], b_mod[l], w_in[l],
                             w_fourier[l], w_pool[l], pool_scale[l], qk_norm_g[l], lam_vecs[l],
                             subln_g[l], w_out[l], rope_cos, rope_sin, l < DEPTH - 1)
    return x
```

```python
import functools
import math

import numpy as np
import jax
import jax.numpy as jnp
from jax import lax
from jax.experimental import pallas as pl
from jax.experimental.pallas import tpu as pltpu

GRID_W = 64
FOURIER_GROUPS = 4
GROUP_W = 64
MIX_Q = 256
POOL_WINDOWS = (2, 4, 8, 16)
QK_DIM = 64
HEAD_W = 2 * QK_DIM
N_HEADS = 4
ATTN_W = N_HEADS * HEAD_W
ROPE_THETA = 10000.0
EPS = 1e-6
U_OFF, V_OFF, P_OFF, Q_OFF, K_OFF, VV_OFF, G_OFF, W_EFF = 0, 256, 512, 768, 1280, 1792, 2304, 3328

LANES = 128
VMEM_LIMIT = 56 * 1024 * 1024
HI = lax.Precision.HIGHEST
F32 = jnp.float32
BF16 = jnp.bfloat16


def _cparams(*sem):
    return pltpu.CompilerParams(dimension_semantics=sem, vmem_limit_bytes=VMEM_LIMIT)


def _channel_dft():
    c = np.arange(GROUP_W)
    ang = 2.0 * np.pi * np.outer(c, c) / GROUP_W
    s = 1.0 / math.sqrt(GROUP_W)
    return np.cos(ang) * s, np.sin(ang) * s


def _stage_tables(n1, n2):
    n = n1 * n2
    k1 = np.arange(n1)
    i1 = np.arange(n1)
    j = np.arange(n2)
    theta = 2.0 * np.pi * (j[:, None, None] * k1[None, :, None] / n + k1[None, :, None] * i1[None, None, :] / n1)
    m1 = np.concatenate([np.cos(theta), -np.sin(theta)], axis=1) / math.sqrt(n)
    ang2 = 2.0 * np.pi * np.outer(j, j) / n2
    return m1.astype(np.float32), np.cos(ang2).astype(np.float32), np.sin(ang2).astype(np.float32)


def _rope_tables(rows):
    row = jnp.repeat(jnp.arange(rows), GRID_W).astype(F32)
    col = jnp.tile(jnp.arange(GRID_W), rows).astype(F32)
    half = QK_DIM // 2
    inv_freq = ROPE_THETA ** (-jnp.arange(0, half, 2, dtype=F32) / half)
    ang_r = row[:, None] * inv_freq[None, :]
    ang_c = col[:, None] * inv_freq[None, :]
    ang = jnp.concatenate([ang_r, ang_r, ang_c, ang_c], axis=-1)
    cos, sin = jnp.cos(ang), jnp.sin(ang)
    quarter = QK_DIM // 4
    first = (jnp.arange(QK_DIM) % half) < quarter
    sin_up = jnp.where(first, -sin, 0.0)
    sin_dn = jnp.where(first, 0.0, sin)
    rep = LANES // QK_DIM
    return tuple(jnp.tile(t, (1, rep)) for t in (cos, sin_up, sin_dn))


def _fold_kernel(w_ref, wf_ref, wp_ref, ps_ref, cd_ref, sd_ref, o_ref):
    cd, sd = cd_ref[...], sd_ref[...]
    for g in range(FOURIER_GROUPS):
        sl = slice(g * GROUP_W, (g + 1) * GROUP_W)
        wa = w_ref[:, sl]
        wf = wf_ref[g]
        cw = jnp.dot(cd, wf, precision=HI, preferred_element_type=F32)
        sw = jnp.dot(sd, wf, precision=HI, preferred_element_type=F32)
        o_ref[:, U_OFF + g * GROUP_W:U_OFF + (g + 1) * GROUP_W] = jnp.dot(
            wa, cw, precision=HI, preferred_element_type=F32).astype(o_ref.dtype)
        o_ref[:, V_OFF + g * GROUP_W:V_OFF + (g + 1) * GROUP_W] = jnp.dot(
            wa, sw, precision=HI, preferred_element_type=F32).astype(o_ref.dtype)
        wb = w_ref[:, MIX_Q + g * GROUP_W:MIX_Q + (g + 1) * GROUP_W]
        pw = jnp.dot(wb, wp_ref[g], precision=HI, preferred_element_type=F32) * ps_ref[:, sl]
        o_ref[:, P_OFF + g * GROUP_W:P_OFF + (g + 1) * GROUP_W] = pw.astype(o_ref.dtype)
    o_ref[:, Q_OFF:] = w_ref[:, 2 * MIX_Q:].astype(o_ref.dtype)


def _fold_weights(w_in, w_fourier, w_pool, pool_scale):
    depth, d, in_w = w_in.shape
    tr = 256
    cd, sd = _channel_dft()
    return pl.pallas_call(
        _fold_kernel,
        grid=(depth, d // tr),
        in_specs=[
            pl.BlockSpec((None, tr, in_w), lambda l, i: (l, i, 0)),
            pl.BlockSpec((None, FOURIER_GROUPS, GROUP_W, GROUP_W), lambda l, i: (l, 0, 0, 0)),
            pl.BlockSpec((None, FOURIER_GROUPS, GROUP_W, GROUP_W), lambda l, i: (l, 0, 0, 0)),
            pl.BlockSpec((None, 1, MIX_Q), lambda l, i: (l, 0, 0)),
            pl.BlockSpec((GROUP_W, GROUP_W), lambda l, i: (0, 0)),
            pl.BlockSpec((GROUP_W, GROUP_W), lambda l, i: (0, 0)),
        ],
        out_specs=pl.BlockSpec((None, tr, W_EFF), lambda l, i: (l, i, 0)),
        out_shape=jax.ShapeDtypeStruct((depth, d, W_EFF), BF16),
        compiler_params=_cparams("parallel", "parallel"),
        name="fold_weights",
    )(w_in, w_fourier, w_pool, pool_scale[:, None, :], jnp.asarray(cd, F32), jnp.asarray(sd, F32))


def _cast_kernel(w_ref, o_ref):
    o_ref[...] = w_ref[...].astype(o_ref.dtype)


def _cast_bf16(w):
    depth, r, c = w.shape
    return pl.pallas_call(
        _cast_kernel,
        grid=(depth,),
        in_specs=[pl.BlockSpec((None, r, c), lambda l: (l, 0, 0))],
        out_specs=pl.BlockSpec((None, r, c), lambda l: (l, 0, 0)),
        out_shape=jax.ShapeDtypeStruct(w.shape, BF16),
        compiler_params=_cparams("parallel"),
        name="cast_w_out",
    )(w)


def _mod_kernel(c_ref, w_ref, b_ref, o_ref):
    cc = c_ref[...]
    sc = cc * jax.nn.sigmoid(cc)
    o_ref[...] = jnp.dot(sc, w_ref[...], precision=HI, preferred_element_type=F32) + b_ref[...]


def _modulation(cc, w_mod, b_mod):
    depth, d, n = w_mod.shape
    tn = 1024
    rows = cc.shape[0]
    return pl.pallas_call(
        _mod_kernel,
        grid=(depth, n // tn),
        in_specs=[
            pl.BlockSpec((rows, d), lambda l, j: (0, 0)),
            pl.BlockSpec((None, d, tn), lambda l, j: (l, 0, j)),
            pl.BlockSpec((None, 1, tn), lambda l, j: (l, 0, j)),
        ],
        out_specs=pl.BlockSpec((None, rows, tn), lambda l, j: (l, 0, j)),
        out_shape=jax.ShapeDtypeStruct((depth, rows, n), F32),
        compiler_params=_cparams("parallel", "parallel"),
        name="modulation",
    )(cc, w_mod, b_mod[:, None, :])


def _qk_normalise(t, ones_blk, gain):
    sq = t * t
    hi = sq.astype(BF16)
    lo = (sq - hi.astype(F32)).astype(BF16)
    ss = jnp.dot(hi, ones_blk, preferred_element_type=F32) + jnp.dot(lo, ones_blk, preferred_element_type=F32)
    return t * lax.rsqrt(ss * (1.0 / QK_DIM) + EPS) * gain


def _rotate(t, cos, sin_up, sin_dn):
    quarter = QK_DIM // 4
    up = pltpu.roll(t, LANES - quarter, 1)
    dn = pltpu.roll(t, quarter, 1)
    return t * cos + up * sin_up + dn * sin_dn


def _inproj_kernel(x_ref, mod_ref, ng_ref, w_ref, qkg_ref, ones_ref, *rest, d, ctx_row, rope):
    if rope:
        cos_ref, sup_ref, sdn_ref = rest[:3]
        rest = rest[3:]
    uv_ref, bp_ref, q_ref, kt_ref, v_ref, sg_ref = rest
    row = ctx_row if ctx_row is not None else pl.program_id(0)
    mod = mod_ref[pl.ds(row, 1), :]
    shift, scale = mod[:, :d], mod[:, d:2 * d]
    xf = x_ref[...]
    ms = jnp.mean(xf * xf, axis=-1, keepdims=True)
    y = xf * lax.rsqrt(ms + EPS) * ng_ref[...]
    h = (y * (1.0 + scale) + shift).astype(BF16)

    def proj(lo, hi):
        return jnp.dot(h, w_ref[:, lo:hi], preferred_element_type=F32)

    uv_ref[...] = proj(U_OFF, P_OFF)
    bp_ref[...] = proj(P_OFF, Q_OFF)
    v_ref[...] = proj(VV_OFF, G_OFF).astype(v_ref.dtype)
    g = proj(G_OFF, W_EFF)
    sg_ref[...] = (g * jax.nn.sigmoid(g)).astype(sg_ref.dtype)

    ones_blk = ones_ref[...]
    if rope:
        cos, sup, sdn = cos_ref[...], sup_ref[...], sdn_ref[...]
    for t in range(N_HEADS):
        sl = slice(t * LANES, (t + 1) * LANES)
        qt = _qk_normalise(proj(Q_OFF + t * LANES, Q_OFF + (t + 1) * LANES), ones_blk, qkg_ref[0:1, :])
        kt = _qk_normalise(proj(K_OFF + t * LANES, K_OFF + (t + 1) * LANES), ones_blk, qkg_ref[1:2, :])
        if rope:
            qt = _rotate(qt, cos, sup, sdn)
            kt = _rotate(kt, cos, sup, sdn)
        q_ref[:, sl] = (qt * (QK_DIM ** -0.5)).astype(q_ref.dtype)
        kt_ref[sl, :] = kt.T.astype(kt_ref.dtype)


def _in_projection(x, mod, norm_g, w_eff, qk_gain, ones_blk, rope_tabs, *, ctx_row, tl):
    b, l, d = x.shape
    rope = rope_tabs is not None
    row_blk = lambda width: pl.BlockSpec((None, tl, width), lambda bi, i: (bi, i, 0))
    full = lambda arr: pl.BlockSpec(arr.shape, lambda bi, i: (0,) * arr.ndim)
    in_specs = [row_blk(d), full(mod), full(norm_g), full(w_eff), full(qk_gain), full(ones_blk)]
    args = [x, mod, norm_g, w_eff, qk_gain, ones_blk]
    if rope:
        in_specs += [pl.BlockSpec((tl, LANES), lambda bi, i: (i, 0))] * 3
        args += list(rope_tabs)
    out_shape = (
        jax.ShapeDtypeStruct((b, l, 2 * MIX_Q), F32),
        jax.ShapeDtypeStruct((b, l, MIX_Q), F32),
        jax.ShapeDtypeStruct((b, l, ATTN_W), BF16),
        jax.ShapeDtypeStruct((b, ATTN_W, l), BF16),
        jax.ShapeDtypeStruct((b, l, ATTN_W), BF16),
        jax.ShapeDtypeStruct((b, l, d), BF16),
    )
    out_specs = (
        row_blk(2 * MIX_Q), row_blk(MIX_Q), row_blk(ATTN_W),
        pl.BlockSpec((None, ATTN_W, tl), lambda bi, i: (bi, 0, i)),
        row_blk(ATTN_W), row_blk(d),
    )
    return pl.pallas_call(
        functools.partial(_inproj_kernel, d=d, ctx_row=ctx_row, rope=rope),
        grid=(b, l // tl),
        in_specs=in_specs,
        out_specs=out_specs,
        out_shape=out_shape,
        compiler_params=_cparams("parallel", "parallel"),
        name="in_projection_ctx" if ctx_row is not None else "in_projection",
    )(*args)


def _attn_kernel(*refs, n_seg, seg_len, tk, lam_init):
    q_ref = refs[0]
    kv_refs = refs[1:1 + 2 * n_seg]
    lam_ref, sub_ref, o_ref, m_ref, l_ref, acc_ref = refs[1 + 2 * n_seg:]
    tq = q_ref.shape[0]
    q = q_ref[...]
    qs = (q[:, :QK_DIM], q[:, QK_DIM:])

    m_ref[...] = jnp.full(m_ref.shape, -jnp.inf, F32)
    l_ref[...] = jnp.zeros(l_ref.shape, F32)
    acc_ref[...] = jnp.zeros(acc_ref.shape, F32)

    def chunk(kt_ref, v_ref, start, size):
        kt = kt_ref[:, pl.ds(start, size)]
        v = v_ref[pl.ds(start, size), :]
        for mp in range(2):
            s = jnp.dot(qs[mp], kt[mp * QK_DIM:(mp + 1) * QK_DIM, :], preferred_element_type=F32)
            m_old = m_ref[mp]
            m_new = jnp.maximum(m_old, jnp.max(s, axis=-1, keepdims=True))
            alpha = jnp.exp(m_old - m_new)
            e = jnp.exp(s - m_new)
            l_ref[mp] = alpha * l_ref[mp] + jnp.sum(e, axis=-1, keepdims=True)
            acc_ref[mp] = alpha * acc_ref[mp] + jnp.dot(e.astype(BF16), v, preferred_element_type=F32)
            m_ref[mp] = m_new

    for si in range(n_seg):
        kt_ref, v_ref = kv_refs[2 * si], kv_refs[2 * si + 1]
        length = seg_len[si]
        size = min(tk, length)
        n_chunks = length // size
        if n_chunks == 1:
            chunk(kt_ref, v_ref, 0, size)
        else:
            def body(j, carry, kt_ref=kt_ref, v_ref=v_ref, size=size):
                chunk(kt_ref, v_ref, pl.multiple_of(j * size, size), size)
                return carry
            lax.fori_loop(0, n_chunks, body, 0)

    lv = lam_ref[...]
    lam = (jnp.exp(jnp.sum(lv[0:1] * lv[1:2], axis=-1, keepdims=True))
           - jnp.exp(jnp.sum(lv[2:3] * lv[3:4], axis=-1, keepdims=True)) + lam_init)
    o = acc_ref[0] / l_ref[0] - lam * (acc_ref[1] / l_ref[1])
    ms = jnp.mean(o * o, axis=-1, keepdims=True)
    o = o * lax.rsqrt(ms + EPS) * sub_ref[...] * (1.0 - lam_init)
    o_ref[...] = o.astype(o_ref.dtype)


def _attention(q, segs, lam_vecs, subln_g, lam_init, *, tq, tk):
    b, lq, _ = q.shape
    in_specs = [pl.BlockSpec((None, tq, HEAD_W), lambda bi, h, i: (bi, i, h))]
    args = [q]
    seg_len = []
    for kt, v in segs:
        lk = kt.shape[2]
        seg_len.append(lk)
        in_specs.append(pl.BlockSpec((None, HEAD_W, lk), lambda bi, h, i: (bi, h, 0)))
        in_specs.append(pl.BlockSpec((None, lk, HEAD_W), lambda bi, h, i: (bi, 0, h)))
        args += [kt, v]
    in_specs += [pl.BlockSpec(lam_vecs.shape, lambda bi, h, i: (0, 0)),
                 pl.BlockSpec(subln_g.shape, lambda bi, h, i: (0, 0))]
    args += [lam_vecs, subln_g]
    return pl.pallas_call(
        functools.partial(_attn_kernel, n_seg=len(segs), seg_len=tuple(seg_len), tk=tk, lam_init=lam_init),
        grid=(b, N_HEADS, lq // tq),
        in_specs=in_specs,
        out_specs=pl.BlockSpec((None, tq, HEAD_W), lambda bi, h, i: (bi, i, h)),
        out_shape=jax.ShapeDtypeStruct((b, lq, ATTN_W), BF16),
        scratch_shapes=[pltpu.VMEM((2, tq, 1), F32), pltpu.VMEM((2, tq, 1), F32),
                        pltpu.VMEM((2, tq, HEAD_W), F32)],
        compiler_params=_cparams("parallel", "parallel", "arbitrary"),
        name="diff_attention",
    )(*args)


def _fourier_kernel(u_ref, v_ref, m1_ref, c2_ref, s2_ref, o_ref, yr_ref, yi_ref, *, n1, n2):
    def stage1(j, carry):
        xu = u_ref[pl.ds(j, n1, stride=n2), :]
        xv = v_ref[pl.ds(j, n1, stride=n2), :]
        x = jnp.concatenate([xu, xv], axis=1).astype(BF16)
        p = jnp.dot(m1_ref[j], x, preferred_element_type=F32)
        yr_ref[pl.ds(j, n1, stride=n2), :] = p[:n1, :LANES] + p[n1:, LANES:]
        yi_ref[pl.ds(j, n1, stride=n2), :] = p[n1:, :LANES] - p[:n1, LANES:]
        return carry

    lax.fori_loop(0, n2, stage1, 0)
    c2, s2 = c2_ref[...], s2_ref[...]

    def stage2(k1, carry):
        rows = pl.ds(pl.multiple_of(k1 * n2, n2), n2)
        xr = (jnp.dot(c2, yr_ref[rows, :].astype(BF16), preferred_element_type=F32)
              + jnp.dot(s2, yi_ref[rows, :].astype(BF16), preferred_element_type=F32))
        o_ref[pl.ds(k1, n2, stride=n1), :] = xr
        return carry

    lax.fori_loop(0, n1, stage2, 0)


def _fourier_latent(uv, n1, n2):
    b, l, _ = uv.shape
    m1, c2, s2 = _stage_tables(n1, n2)
    m1, c2, s2 = (jnp.asarray(t).astype(BF16) for t in (m1, c2, s2))
    halves = MIX_Q // LANES
    return pl.pallas_call(
        functools.partial(_fourier_kernel, n1=n1, n2=n2),
        grid=(b, halves),
        in_specs=[
            pl.BlockSpec((None, l, LANES), lambda bi, c: (bi, 0, c)),
            pl.BlockSpec((None, l, LANES), lambda bi, c: (bi, 0, c + halves)),
            pl.BlockSpec(m1.shape, lambda bi, c: (0, 0, 0)),
            pl.BlockSpec(c2.shape, lambda bi, c: (0, 0)),
            pl.BlockSpec(s2.shape, lambda bi, c: (0, 0)),
        ],
        out_specs=pl.BlockSpec((None, l, LANES), lambda bi, c: (bi, 0, c)),
        out_shape=jax.ShapeDtypeStruct((b, l, MIX_Q), F32),
        scratch_shapes=[pltpu.VMEM((l, LANES), F32), pltpu.VMEM((l, LANES), F32)],
        compiler_params=_cparams("parallel", "parallel"),
        name="fourier_positions",
    )(uv, uv, m1, c2, s2)


def _fourier_dense_kernel(uv_ref, c_ref, s_ref, o_ref):
    uv = uv_ref[...].astype(BF16)
    o_ref[...] = (jnp.dot(c_ref[...], uv[:, :MIX_Q], preferred_element_type=F32)
                  - jnp.dot(s_ref[...], uv[:, MIX_Q:], preferred_element_type=F32))


def _fourier_dense(uv):
    b, l, _ = uv.shape
    n = np.arange(l)
    ang = 2.0 * np.pi * np.outer(n, n) / l
    c = jnp.asarray((np.cos(ang) / math.sqrt(l)).astype(np.float32)).astype(BF16)
    s = jnp.asarray((np.sin(ang) / math.sqrt(l)).astype(np.float32)).astype(BF16)
    return pl.pallas_call(
        _fourier_dense_kernel,
        grid=(b,),
        in_specs=[pl.BlockSpec((None, l, 2 * MIX_Q), lambda bi: (bi, 0, 0)),
                  pl.BlockSpec((l, l), lambda bi: (0, 0)),
                  pl.BlockSpec((l, l), lambda bi: (0, 0))],
        out_specs=pl.BlockSpec((None, l, MIX_Q), lambda bi: (bi, 0, 0)),
        out_shape=jax.ShapeDtypeStruct((b, l, MIX_Q), F32),
        compiler_params=_cparams("parallel"),
        name="fourier_positions_ctx",
    )(uv, c, s)


POOL_HALO = 8


def _pool_kernel(b_ref, o_ref, pad_ref, *, l, rows):
    zeros = jnp.zeros((POOL_HALO, LANES), F32)
    lane = lax.broadcasted_iota(jnp.int32, (rows, LANES), 1)
    low_group = lane < GROUP_W
    n_chunks = l // rows
    for tile in range(MIX_Q // LANES):
        w_lo, w_hi = POOL_WINDOWS[2 * tile], POOL_WINDOWS[2 * tile + 1]
        pad_ref[0:POOL_HALO, :] = zeros
        pad_ref[POOL_HALO + l:, :] = zeros
        pad_ref[POOL_HALO:POOL_HALO + l, :] = b_ref[:, tile * LANES:(tile + 1) * LANES]

        def body(ci, carry, tile=tile, w_lo=w_lo, w_hi=w_hi):
            t0 = pl.multiple_of(ci * rows, rows)
            pos = t0 + lax.broadcasted_iota(jnp.int32, (rows, LANES), 0)

            def window_mean(w):
                half = w // 2
                s = pad_ref[pl.ds(t0 + POOL_HALO - half, rows), :]
                for dlt in range(-half + 1, half):
                    s = s + pad_ref[pl.ds(t0 + POOL_HALO + dlt, rows), :]
                cnt = jnp.minimum(pos + (half - 1), l - 1) - jnp.maximum(pos - half, 0) + 1
                return s / cnt.astype(F32)

            centre = pad_ref[pl.ds(t0 + POOL_HALO, rows), :]
            pooled = jnp.where(low_group, window_mean(w_lo), window_mean(w_hi)) - centre
            o_ref[pl.ds(t0, rows), tile * LANES:(tile + 1) * LANES] = pooled
            return carry

        lax.fori_loop(0, n_chunks, body, 0)


def _pool(bp):
    b, l, _ = bp.shape
    rows = 256
    return pl.pallas_call(
        functools.partial(_pool_kernel, l=l, rows=rows),
        grid=(b,),
        in_specs=[pl.BlockSpec((None, l, MIX_Q), lambda bi: (bi, 0, 0))],
        out_specs=pl.BlockSpec((None, l, MIX_Q), lambda bi: (bi, 0, 0)),
        out_shape=jax.ShapeDtypeStruct((b, l, MIX_Q), F32),
        scratch_shapes=[pltpu.VMEM((l + 2 * POOL_HALO, LANES), F32)],
        compiler_params=_cparams("parallel"),
        name="pool_windows",
    )(bp)


def _outproj_kernel(x_ref, f_ref, p_ref, a_ref, sg_ref, mod_ref, w_ref, o_ref, *, d, ctx_row):
    row = ctx_row if ctx_row is not None else pl.program_id(0)
    gate = mod_ref[pl.ds(row, 1), 2 * d:]
    sg = sg_ref[...].astype(F32)
    yf = (f_ref[...] * sg[:, :MIX_Q]).astype(BF16)
    yp = (p_ref[...] * sg[:, MIX_Q:2 * MIX_Q]).astype(BF16)
    ya = (a_ref[...].astype(F32) * sg[:, 2 * MIX_Q:]).astype(BF16)
    acc = jnp.dot(yf, w_ref[0:MIX_Q, :], preferred_element_type=F32)
    acc += jnp.dot(yp, w_ref[MIX_Q:2 * MIX_Q, :], preferred_element_type=F32)
    acc += jnp.dot(ya, w_ref[2 * MIX_Q:, :], preferred_element_type=F32)
    o_ref[...] = x_ref[...] + gate * acc


def _out_projection(x, four, pool, attn, sg, mod, w_out, *, ctx_row, tl):
    b, l, d = x.shape
    row_blk = lambda width: pl.BlockSpec((None, tl, width), lambda bi, i: (bi, i, 0))
    full = lambda arr: pl.BlockSpec(arr.shape, lambda bi, i: (0,) * arr.ndim)
    return pl.pallas_call(
        functools.partial(_outproj_kernel, d=d, ctx_row=ctx_row),
        grid=(b, l // tl),
        in_specs=[row_blk(d), row_blk(MIX_Q), row_blk(MIX_Q), row_blk(ATTN_W), row_blk(d), full(mod), full(w_out)],
        out_specs=row_blk(d),
        out_shape=jax.ShapeDtypeStruct((b, l, d), F32),
        compiler_params=_cparams("parallel", "parallel"),
        name="out_projection_ctx" if ctx_row is not None else "out_projection",
    )(x, four, pool, attn, sg, mod, w_out)


def kernel(x, c, ctx, c_ctx, norm_g, w_mod, b_mod, w_in, w_fourier, w_pool, pool_scale, qk_norm_g, lam_vecs,
           subln_g, w_out):
    b, l, d = x.shape
    depth = w_in.shape[0]
    ctx_len = ctx.shape[1]
    ctx_row = b

    cc = jnp.concatenate([c, c_ctx[None, :], jnp.zeros((8 - b - 1, d), F32)], axis=0)
    mod = _modulation(cc, w_mod, b_mod)
    w_eff = _fold_weights(w_in, w_fourier, w_pool, pool_scale)
    w_out_bf = _cast_bf16(w_out)
    rope_tabs = _rope_tables(l // GRID_W)
    lane_chunk = lax.broadcasted_iota(jnp.int32, (LANES, LANES), 0) // QK_DIM
    ones_blk = (lane_chunk == lane_chunk.T).astype(BF16)
    n1 = 64
    n2 = l // n1

    for layer in range(depth):
        lam_init = 0.8 - 0.6 * math.exp(-0.3 * layer)
        update_ctx = layer < depth - 1
        ng = norm_g[layer][None, :]
        qk_gain = jnp.tile(qk_norm_g[layer], (1, LANES // QK_DIM))
        sub_g = subln_g[layer][None, :]
        lv = lam_vecs[layer]

        uv, bp, q, kt, v, sg = _in_projection(x, mod[layer], ng, w_eff[layer], qk_gain, ones_blk, rope_tabs,
                                               ctx_row=None, tl=512)
        uv_c, bp_c, q_c, kt_c, v_c, sg_c = _in_projection(ctx, mod[layer], ng, w_eff[layer], qk_gain, ones_blk,
                                                          None, ctx_row=ctx_row, tl=ctx_len)
        attn = _attention(q, [(kt_c, v_c), (kt, v)], lv, sub_g, lam_init, tq=256, tk=512)
        four = _fourier_latent(uv, n1, n2)
        pool = _pool(bp)
        x_new = _out_projection(x, four, pool, attn, sg, mod[layer], w_out_bf[layer], ctx_row=None, tl=512)
        if update_ctx:
            attn_c = _attention(q_c, [(kt_c, v_c)], lv, sub_g, lam_init, tq=ctx_len, tk=512)
            ctx = _out_projection(ctx, _fourier_dense(uv_c), _pool(bp_c), attn_c, sg_c, mod[layer],
                                  w_out_bf[layer], ctx_row=ctx_row, tl=ctx_len)
        x = x_new
    return x
```

```python
import functools
import math

import numpy as np
import jax
import jax.numpy as jnp
from jax import lax
from jax.experimental import pallas as pl
from jax.experimental.pallas import tpu as pltpu

GRID_W = 64
FOURIER_GROUPS = 4
GROUP_W = 64
MIX_Q = 256
POOL_WINDOWS = (2, 4, 8, 16)
QK_DIM = 64
HEAD_W = 2 * QK_DIM
N_HEADS = 4
ATTN_W = N_HEADS * HEAD_W
ROPE_THETA = 10000.0
EPS = 1e-6
Q_SCALE = QK_DIM ** -0.5 * math.log2(math.e)
UNROLL_T = 2
ATTN_TK = 512
MAX_UNSHIFTED_SCORE = 40.0
BF16_ROUNDING_SLACK = 1.02
U_OFF, V_OFF, P_OFF, Q_OFF, K_OFF, VV_OFF, G_OFF, W_EFF = 0, 256, 512, 768, 1280, 1792, 2304, 3328

LANES = 128
VMEM_LIMIT = 56 * 1024 * 1024
HI = lax.Precision.HIGHEST
F32 = jnp.float32
BF16 = jnp.bfloat16


def _cparams(*sem):
    return pltpu.CompilerParams(dimension_semantics=sem, vmem_limit_bytes=VMEM_LIMIT)


def _channel_dft():
    c = np.arange(GROUP_W)
    ang = 2.0 * np.pi * np.outer(c, c) / GROUP_W
    s = 1.0 / math.sqrt(GROUP_W)
    return np.cos(ang) * s, np.sin(ang) * s


def _stage_tables(n1, n2):
    n = n1 * n2
    k1 = np.arange(n1)
    i1 = np.arange(n1)
    j = np.arange(n2)
    theta = 2.0 * np.pi * (j[:, None, None] * k1[None, :, None] / n + k1[None, :, None] * i1[None, None, :] / n1)
    m1 = np.concatenate([np.cos(theta), -np.sin(theta)], axis=1) / math.sqrt(n)
    ang2 = 2.0 * np.pi * np.outer(j, j) / n2
    return m1.astype(np.float32), np.cos(ang2).astype(np.float32), np.sin(ang2).astype(np.float32)


def _rope_tables(rows):
    row = jnp.repeat(jnp.arange(rows), GRID_W).astype(F32)
    col = jnp.tile(jnp.arange(GRID_W), rows).astype(F32)
    half = QK_DIM // 2
    inv_freq = ROPE_THETA ** (-jnp.arange(0, half, 2, dtype=F32) / half)
    ang_r = row[:, None] * inv_freq[None, :]
    ang_c = col[:, None] * inv_freq[None, :]
    ang = jnp.concatenate([ang_r, ang_r, ang_c, ang_c], axis=-1)
    cos, sin = jnp.cos(ang), jnp.sin(ang)
    quarter = QK_DIM // 4
    first = (jnp.arange(QK_DIM) % half) < quarter
    sin_up = jnp.where(first, -sin, 0.0)
    sin_dn = jnp.where(first, 0.0, sin)
    rep = LANES // QK_DIM
    return tuple(jnp.tile(t, (1, rep)) for t in (cos, sin_up, sin_dn))


def _fold_kernel(w_ref, wf_ref, wp_ref, ps_ref, cd_ref, sd_ref, o_ref):
    cd, sd = cd_ref[...], sd_ref[...]
    for g in range(FOURIER_GROUPS):
        sl = slice(g * GROUP_W, (g + 1) * GROUP_W)
        wa = w_ref[:, sl]
        wf = wf_ref[g]
        cw = jnp.dot(cd, wf, precision=HI, preferred_element_type=F32)
        sw = jnp.dot(sd, wf, precision=HI, preferred_element_type=F32)
        o_ref[:, U_OFF + g * GROUP_W:U_OFF + (g + 1) * GROUP_W] = jnp.dot(
            wa, cw, precision=HI, preferred_element_type=F32).astype(o_ref.dtype)
        o_ref[:, V_OFF + g * GROUP_W:V_OFF + (g + 1) * GROUP_W] = jnp.dot(
            wa, sw, precision=HI, preferred_element_type=F32).astype(o_ref.dtype)
        wb = w_ref[:, MIX_Q + g * GROUP_W:MIX_Q + (g + 1) * GROUP_W]
        pw = jnp.dot(wb, wp_ref[g], precision=HI, preferred_element_type=F32) * ps_ref[:, sl]
        o_ref[:, P_OFF + g * GROUP_W:P_OFF + (g + 1) * GROUP_W] = pw.astype(o_ref.dtype)
    o_ref[:, Q_OFF:] = w_ref[:, 2 * MIX_Q:].astype(o_ref.dtype)


def _fold_weights(w_in, w_fourier, w_pool, pool_scale):
    depth, d, in_w = w_in.shape
    tr = 256
    cd, sd = _channel_dft()
    return pl.pallas_call(
        _fold_kernel,
        grid=(depth, d // tr),
        in_specs=[
            pl.BlockSpec((None, tr, in_w), lambda l, i: (l, i, 0)),
            pl.BlockSpec((None, FOURIER_GROUPS, GROUP_W, GROUP_W), lambda l, i: (l, 0, 0, 0)),
            pl.BlockSpec((None, FOURIER_GROUPS, GROUP_W, GROUP_W), lambda l, i: (l, 0, 0, 0)),
            pl.BlockSpec((None, 1, MIX_Q), lambda l, i: (l, 0, 0)),
            pl.BlockSpec((GROUP_W, GROUP_W), lambda l, i: (0, 0)),
            pl.BlockSpec((GROUP_W, GROUP_W), lambda l, i: (0, 0)),
        ],
        out_specs=pl.BlockSpec((None, tr, W_EFF), lambda l, i: (l, i, 0)),
        out_shape=jax.ShapeDtypeStruct((depth, d, W_EFF), BF16),
        compiler_params=_cparams("parallel", "parallel"),
        name="fold_weights",
    )(w_in, w_fourier, w_pool, pool_scale[:, None, :], jnp.asarray(cd, F32), jnp.asarray(sd, F32))


def _cast_kernel(w_ref, o_ref):
    o_ref[...] = w_ref[...].astype(o_ref.dtype)


def _cast_bf16(w):
    depth, r, c = w.shape
    return pl.pallas_call(
        _cast_kernel,
        grid=(depth,),
        in_specs=[pl.BlockSpec((None, r, c), lambda l: (l, 0, 0))],
        out_specs=pl.BlockSpec((None, r, c), lambda l: (l, 0, 0)),
        out_shape=jax.ShapeDtypeStruct(w.shape, BF16),
        compiler_params=_cparams("parallel"),
        name="cast_w_out",
    )(w)


def _mod_kernel(c_ref, w_ref, b_ref, o_ref):
    cc = c_ref[...]
    sc = cc * jax.nn.sigmoid(cc)
    o_ref[...] = jnp.dot(sc, w_ref[...], precision=HI, preferred_element_type=F32) + b_ref[...]


def _modulation(cc, w_mod, b_mod):
    depth, d, n = w_mod.shape
    tn = 1024
    rows = cc.shape[0]
    return pl.pallas_call(
        _mod_kernel,
        grid=(depth, n // tn),
        in_specs=[
            pl.BlockSpec((rows, d), lambda l, j: (0, 0)),
            pl.BlockSpec((None, d, tn), lambda l, j: (l, 0, j)),
            pl.BlockSpec((None, 1, tn), lambda l, j: (l, 0, j)),
        ],
        out_specs=pl.BlockSpec((None, rows, tn), lambda l, j: (l, 0, j)),
        out_shape=jax.ShapeDtypeStruct((depth, rows, n), F32),
        compiler_params=_cparams("parallel", "parallel"),
        name="modulation",
    )(cc, w_mod, b_mod[:, None, :])


def _qk_normalise(t, ones_blk, gain):
    sq = t * t
    hi = sq.astype(BF16)
    lo = (sq - hi.astype(F32)).astype(BF16)
    ss = jnp.dot(hi, ones_blk, preferred_element_type=F32) + jnp.dot(lo, ones_blk, preferred_element_type=F32)
    return t * lax.rsqrt(ss * (1.0 / QK_DIM) + EPS) * gain


def _rotate(t, cos, sin_up, sin_dn):
    quarter = QK_DIM // 4
    up = pltpu.roll(t, LANES - quarter, 1)
    dn = pltpu.roll(t, quarter, 1)
    return t * cos + up * sin_up + dn * sin_dn


def _inproj_kernel(x_ref, mod_ref, ng_ref, w_ref, qkg_ref, ones_ref, *rest, d, ctx_row, rope):
    if rope:
        cos_ref, sup_ref, sdn_ref = rest[:3]
        rest = rest[3:]
    uv_ref, bp_ref, qt_ref, k_ref, vt_ref, sg_ref = rest
    row = ctx_row if ctx_row is not None else pl.program_id(0)
    mod = mod_ref[pl.ds(row, 1), :]
    shift, scale = mod[:, :d], mod[:, d:2 * d]
    xf = x_ref[...]
    ms = jnp.mean(xf * xf, axis=-1, keepdims=True)
    y = xf * lax.rsqrt(ms + EPS) * ng_ref[...]
    h = (y * (1.0 + scale) + shift).astype(BF16)

    def proj(lo, hi):
        return jnp.dot(h, w_ref[:, lo:hi], preferred_element_type=F32)

    uv_ref[...] = proj(U_OFF, P_OFF)
    bp_ref[...] = proj(P_OFF, Q_OFF)
    g = proj(G_OFF, W_EFF)
    sg_ref[...] = (g * jax.nn.sigmoid(g)).astype(sg_ref.dtype)

    ones_blk = ones_ref[...]
    if rope:
        cos, sup, sdn = cos_ref[...], sup_ref[...], sdn_ref[...]
    for t in range(N_HEADS):
        sl = slice(t * LANES, (t + 1) * LANES)
        qt = _qk_normalise(proj(Q_OFF + t * LANES, Q_OFF + (t + 1) * LANES), ones_blk, qkg_ref[0:1, :])
        kt = _qk_normalise(proj(K_OFF + t * LANES, K_OFF + (t + 1) * LANES), ones_blk, qkg_ref[1:2, :])
        if rope:
            qt = _rotate(qt, cos, sup, sdn)
            kt = _rotate(kt, cos, sup, sdn)
        qt_ref[sl, :] = (qt * Q_SCALE).T.astype(qt_ref.dtype)
        k_ref[:, sl] = kt.astype(k_ref.dtype)
        vt_ref[sl, :] = proj(VV_OFF + t * LANES, VV_OFF + (t + 1) * LANES).T.astype(vt_ref.dtype)


def _in_projection(x, mod, norm_g, w_eff, qk_gain, ones_blk, rope_tabs, *, ctx_row, tl):
    b, l, d = x.shape
    rope = rope_tabs is not None
    row_blk = lambda width: pl.BlockSpec((None, tl, width), lambda bi, i: (bi, i, 0))
    full = lambda arr: pl.BlockSpec(arr.shape, lambda bi, i: (0,) * arr.ndim)
    in_specs = [row_blk(d), full(mod), full(norm_g), full(w_eff), full(qk_gain), full(ones_blk)]
    args = [x, mod, norm_g, w_eff, qk_gain, ones_blk]
    if rope:
        in_specs += [pl.BlockSpec((tl, LANES), lambda bi, i: (i, 0))] * 3
        args += list(rope_tabs)
    out_shape = (
        jax.ShapeDtypeStruct((b, l, 2 * MIX_Q), F32),
        jax.ShapeDtypeStruct((b, l, MIX_Q), F32),
        jax.ShapeDtypeStruct((b, ATTN_W, l), BF16),
        jax.ShapeDtypeStruct((b, l, ATTN_W), BF16),
        jax.ShapeDtypeStruct((b, ATTN_W, l), BF16),
        jax.ShapeDtypeStruct((b, l, d), BF16),
    )
    col_blk = pl.BlockSpec((None, ATTN_W, tl), lambda bi, i: (bi, 0, i))
    out_specs = (row_blk(2 * MIX_Q), row_blk(MIX_Q), col_blk, row_blk(ATTN_W), col_blk, row_blk(d))
    return pl.pallas_call(
        functools.partial(_inproj_kernel, d=d, ctx_row=ctx_row, rope=rope),
        grid=(b, l // tl),
        in_specs=in_specs,
        out_specs=out_specs,
        out_shape=out_shape,
        compiler_params=_cparams("parallel", "parallel"),
        name="in_projection_ctx" if ctx_row is not None else "in_projection",
    )(*args)


def _attn_kernel(*refs, n_seg, seg_len, tk, lam_init):
    qt_ref = refs[0]
    kv_refs = refs[1:1 + 2 * n_seg]
    lam_ref, sub_ref, o_ref, m_ref, l_ref, acc_ref = refs[1 + 2 * n_seg:]
    qt = qt_ref[...]
    first_map = lax.broadcasted_iota(jnp.int32, qt.shape, 0) < QK_DIM
    zero = jnp.zeros_like(qt)
    ws = (jnp.where(first_map, qt, zero), jnp.where(first_map, zero, qt))

    m_ref[...] = jnp.full(m_ref.shape, -jnp.inf, F32)
    l_ref[...] = jnp.zeros(l_ref.shape, F32)
    acc_ref[...] = jnp.zeros(acc_ref.shape, F32)

    def chunk(k_ref, vt_ref, start, size):
        k = k_ref[pl.ds(start, size), :]
        vt = vt_ref[:, pl.ds(start, size)]
        for mp in range(2):
            s = jnp.dot(k, ws[mp], preferred_element_type=F32)
            m_old = m_ref[mp]
            m_new = jnp.maximum(m_old, jnp.max(s, axis=0, keepdims=True))
            alpha = jnp.exp2(m_old - m_new)
            e = jnp.exp2(s - m_new)
            l_ref[mp] = alpha * l_ref[mp] + jnp.sum(e, axis=0, keepdims=True)
            acc_ref[mp] = alpha * acc_ref[mp] + jnp.dot(vt, e.astype(BF16), preferred_element_type=F32)
            m_ref[mp] = m_new

    for si in range(n_seg):
        k_ref, vt_ref = kv_refs[2 * si], kv_refs[2 * si + 1]
        length = seg_len[si]
        size = min(tk, length)
        n_chunks = length // size
        if n_chunks == 1:
            chunk(k_ref, vt_ref, 0, size)
        else:
            def body(j, carry, k_ref=k_ref, vt_ref=vt_ref, size=size):
                chunk(k_ref, vt_ref, pl.multiple_of(j * size, size), size)
                return carry
            lax.fori_loop(0, n_chunks, body, 0)

    lv = lam_ref[...]
    lam = (jnp.exp(jnp.sum(lv[0:1] * lv[1:2], axis=-1, keepdims=True))
           - jnp.exp(jnp.sum(lv[2:3] * lv[3:4], axis=-1, keepdims=True)) + lam_init)
    o = acc_ref[0] / l_ref[0] - lam * (acc_ref[1] / l_ref[1])
    ms = jnp.mean(o * o, axis=0, keepdims=True)
    o = o * lax.rsqrt(ms + EPS) * (sub_ref[...] * (1.0 - lam_init))
    o_ref[...] = o.T.astype(o_ref.dtype)


def _attn_unshifted_kernel(*refs, n_seg, seg_len, tk, lam_init):
    qt_ref = refs[0]
    kv_refs = refs[1:1 + 2 * n_seg]
    lam_ref, sub_ref, o_ref, l_ref, acc_ref, sa_ref, sb_ref = refs[1 + 2 * n_seg:]
    qt = qt_ref[...]
    tq = qt.shape[1]
    first_map = lax.broadcasted_iota(jnp.int32, qt.shape, 0) < QK_DIM
    zero = jnp.zeros_like(qt)
    w = jnp.concatenate([jnp.where(first_map, qt, zero), jnp.where(first_map, zero, qt)], axis=1)

    l_ref[...] = jnp.zeros(l_ref.shape, F32)
    acc_ref[...] = jnp.zeros(acc_ref.shape, F32)

    def scores(s_ref, k_ref, start, size):
        s_ref[0:size, :] = jnp.dot(k_ref[pl.ds(start, size), :], w, preferred_element_type=F32)

    def consume(s_ref, vt_ref, start, size):
        e = jnp.exp2(s_ref[0:size, :])
        l_ref[...] += jnp.sum(e.reshape(size // 8, 8, 2 * tq), axis=0)
        acc_ref[...] += jnp.dot(vt_ref[:, pl.ds(start, size)], e.astype(BF16), preferred_element_type=F32)

    short = [(kv_refs[2 * i], kv_refs[2 * i + 1], seg_len[i]) for i in range(n_seg) if seg_len[i] <= tk]
    long = [(kv_refs[2 * i], kv_refs[2 * i + 1], seg_len[i]) for i in range(n_seg) if seg_len[i] > tk]
    assert len(short) <= 1 and len(long) <= 1
    if short:
        kc_ref, vtc_ref, len_c = short[0]
        scores(sa_ref, kc_ref, 0, len_c)
    if long:
        k_ref, vt_ref, length = long[0]
        n_chunks = length // tk
        assert length % tk == 0 and n_chunks % 2 == 0
        at = lambda c: pl.multiple_of(c * tk, tk)
        scores(sb_ref, k_ref, 0, tk)
        if short:
            consume(sa_ref, vtc_ref, 0, len_c)

        def body(p, carry):
            c0 = 2 * p
            scores(sa_ref, k_ref, at(c0 + 1), tk)
            consume(sb_ref, vt_ref, at(c0), tk)
            scores(sb_ref, k_ref, at(c0 + 2), tk)
            consume(sa_ref, vt_ref, at(c0 + 1), tk)
            return carry

        lax.fori_loop(0, n_chunks // 2 - 1, body, 0)
        scores(sa_ref, k_ref, (n_chunks - 1) * tk, tk)
        consume(sb_ref, vt_ref, (n_chunks - 2) * tk, tk)
        consume(sa_ref, vt_ref, (n_chunks - 1) * tk, tk)
    elif short:
        consume(sa_ref, vtc_ref, 0, len_c)

    lv = lam_ref[...]
    lam = (jnp.exp(jnp.sum(lv[0:1] * lv[1:2], axis=-1, keepdims=True))
           - jnp.exp(jnp.sum(lv[2:3] * lv[3:4], axis=-1, keepdims=True)) + lam_init)
    lsum = jnp.sum(l_ref[...], axis=0, keepdims=True)
    acc = acc_ref[...]
    o = acc[:, :tq] / lsum[:, :tq] - lam * (acc[:, tq:] / lsum[:, tq:])
    ms = jnp.mean(o * o, axis=0, keepdims=True)
    o = o * lax.rsqrt(ms + EPS) * (sub_ref[...] * (1.0 - lam_init))
    o_ref[...] = o.T.astype(o_ref.dtype)


def _attention(qt, segs, lam_vecs, subln_col, lam_init, *, tq, tk, shifted):
    b, _, lq = qt.shape
    in_specs = [pl.BlockSpec((None, HEAD_W, tq), lambda bi, h, i: (bi, h, i))]
    args = [qt]
    seg_len = []
    for k, vt in segs:
        lk = k.shape[1]
        seg_len.append(lk)
        in_specs.append(pl.BlockSpec((None, lk, HEAD_W), lambda bi, h, i: (bi, 0, h)))
        in_specs.append(pl.BlockSpec((None, HEAD_W, lk), lambda bi, h, i: (bi, h, 0)))
        args += [k, vt]
    in_specs += [pl.BlockSpec(lam_vecs.shape, lambda bi, h, i: (0, 0)),
                 pl.BlockSpec(subln_col.shape, lambda bi, h, i: (0, 0))]
    args += [lam_vecs, subln_col]
    if shifted:
        body = _attn_kernel
        scratch = [pltpu.VMEM((2, 1, tq), F32), pltpu.VMEM((2, 1, tq), F32), pltpu.VMEM((2, HEAD_W, tq), F32)]
    else:
        body = _attn_unshifted_kernel
        s_rows = min(tk, max(seg_len))
        scratch = [pltpu.VMEM((8, 2 * tq), F32), pltpu.VMEM((HEAD_W, 2 * tq), F32),
                   pltpu.VMEM((s_rows, 2 * tq), F32), pltpu.VMEM((s_rows, 2 * tq), F32)]
    return pl.pallas_call(
        functools.partial(body, n_seg=len(segs), seg_len=tuple(seg_len), tk=tk, lam_init=lam_init),
        grid=(b, N_HEADS, lq // tq),
        in_specs=in_specs,
        out_specs=pl.BlockSpec((None, tq, HEAD_W), lambda bi, h, i: (bi, i, h)),
        out_shape=jax.ShapeDtypeStruct((b, lq, ATTN_W), BF16),
        scratch_shapes=scratch,
        compiler_params=_cparams("parallel", "parallel", "arbitrary"),
        name="diff_attention_shifted" if shifted else "diff_attention",
    )(*args)


def _attention_dispatch(score_bound, *args, **kw):
    return lax.cond(score_bound <= MAX_UNSHIFTED_SCORE,
                    lambda: _attention(*args, shifted=False, **kw),
                    lambda: _attention(*args, shifted=True, **kw))


def _fourier_kernel(u_ref, v_ref, m1_ref, c2_ref, s2_ref, o_ref, yr_ref, yi_ref, *, n1, n2):
    def stage1(j, carry):
        xu = u_ref[pl.ds(j, n1, stride=n2), :]
        xv = v_ref[pl.ds(j, n1, stride=n2), :]
        x = jnp.concatenate([xu, xv], axis=1).astype(BF16)
        p = jnp.dot(m1_ref[j], x, preferred_element_type=F32)
        yr_ref[pl.ds(j, n1, stride=n2), :] = p[:n1, :LANES] + p[n1:, LANES:]
        yi_ref[pl.ds(j, n1, stride=n2), :] = p[n1:, :LANES] - p[:n1, LANES:]
        return carry

    lax.fori_loop(0, n2, stage1, 0)
    c2, s2 = c2_ref[...], s2_ref[...]

    def stage2(k1, carry):
        rows = pl.ds(pl.multiple_of(k1 * n2, n2), n2)
        xr = (jnp.dot(c2, yr_ref[rows, :].astype(BF16), preferred_element_type=F32)
              + jnp.dot(s2, yi_ref[rows, :].astype(BF16), preferred_element_type=F32))
        o_ref[pl.ds(k1, n2, stride=n1), :] = xr
        return carry

    lax.fori_loop(0, n1, stage2, 0)


def _fourier_latent(uv, n1, n2):
    b, l, _ = uv.shape
    m1, c2, s2 = _stage_tables(n1, n2)
    m1, c2, s2 = (jnp.asarray(t).astype(BF16) for t in (m1, c2, s2))
    halves = MIX_Q // LANES
    return pl.pallas_call(
        functools.partial(_fourier_kernel, n1=n1, n2=n2),
        grid=(b, halves),
        in_specs=[
            pl.BlockSpec((None, l, LANES), lambda bi, c: (bi, 0, c)),
            pl.BlockSpec((None, l, LANES), lambda bi, c: (bi, 0, c + halves)),
            pl.BlockSpec(m1.shape, lambda bi, c: (0, 0, 0)),
            pl.BlockSpec(c2.shape, lambda bi, c: (0, 0)),
            pl.BlockSpec(s2.shape, lambda bi, c: (0, 0)),
        ],
        out_specs=pl.BlockSpec((None, l, LANES), lambda bi, c: (bi, 0, c)),
        out_shape=jax.ShapeDtypeStruct((b, l, MIX_Q), F32),
        scratch_shapes=[pltpu.VMEM((l, LANES), F32), pltpu.VMEM((l, LANES), F32)],
        compiler_params=_cparams("parallel", "parallel"),
        name="fourier_positions",
    )(uv, uv, m1, c2, s2)


def _fourier_dense_kernel(uv_ref, c_ref, s_ref, o_ref):
    uv = uv_ref[...].astype(BF16)
    o_ref[...] = (jnp.dot(c_ref[...], uv[:, :MIX_Q], preferred_element_type=F32)
                  - jnp.dot(s_ref[...], uv[:, MIX_Q:], preferred_element_type=F32))


def _fourier_dense(uv):
    b, l, _ = uv.shape
    n = np.arange(l)
    ang = 2.0 * np.pi * np.outer(n, n) / l
    c = jnp.asarray((np.cos(ang) / math.sqrt(l)).astype(np.float32)).astype(BF16)
    s = jnp.asarray((np.sin(ang) / math.sqrt(l)).astype(np.float32)).astype(BF16)
    return pl.pallas_call(
        _fourier_dense_kernel,
        grid=(b,),
        in_specs=[pl.BlockSpec((None, l, 2 * MIX_Q), lambda bi: (bi, 0, 0)),
                  pl.BlockSpec((l, l), lambda bi: (0, 0)),
                  pl.BlockSpec((l, l), lambda bi: (0, 0))],
        out_specs=pl.BlockSpec((None, l, MIX_Q), lambda bi: (bi, 0, 0)),
        out_shape=jax.ShapeDtypeStruct((b, l, MIX_Q), F32),
        compiler_params=_cparams("parallel"),
        name="fourier_positions_ctx",
    )(uv, c, s)


POOL_HALO = 8


def _pool_kernel(b_ref, o_ref, pad_ref, *, l, rows):
    zeros = jnp.zeros((POOL_HALO, LANES), F32)
    lane = lax.broadcasted_iota(jnp.int32, (rows, LANES), 1)
    low_group = lane < GROUP_W
    n_chunks = l // rows
    for tile in range(MIX_Q // LANES):
        w_lo, w_hi = POOL_WINDOWS[2 * tile], POOL_WINDOWS[2 * tile + 1]
        pad_ref[0:POOL_HALO, :] = zeros
        pad_ref[POOL_HALO + l:, :] = zeros
        pad_ref[POOL_HALO:POOL_HALO + l, :] = b_ref[:, tile * LANES:(tile + 1) * LANES]

        def body(ci, carry, tile=tile, w_lo=w_lo, w_hi=w_hi):
            t0 = pl.multiple_of(ci * rows, rows)
            pos = t0 + lax.broadcasted_iota(jnp.int32, (rows, LANES), 0)

            def window_mean(w):
                half = w // 2
                s = pad_ref[pl.ds(t0 + POOL_HALO - half, rows), :]
                for dlt in range(-half + 1, half):
                    s = s + pad_ref[pl.ds(t0 + POOL_HALO + dlt, rows), :]
                cnt = jnp.minimum(pos + (half - 1), l - 1) - jnp.maximum(pos - half, 0) + 1
                return s / cnt.astype(F32)

            centre = pad_ref[pl.ds(t0 + POOL_HALO, rows), :]
            pooled = jnp.where(low_group, window_mean(w_lo), window_mean(w_hi)) - centre
            o_ref[pl.ds(t0, rows), tile * LANES:(tile + 1) * LANES] = pooled
            return carry

        lax.fori_loop(0, n_chunks, body, 0)


def _pool(bp):
    b, l, _ = bp.shape
    rows = 256
    return pl.pallas_call(
        functools.partial(_pool_kernel, l=l, rows=rows),
        grid=(b,),
        in_specs=[pl.BlockSpec((None, l, MIX_Q), lambda bi: (bi, 0, 0))],
        out_specs=pl.BlockSpec((None, l, MIX_Q), lambda bi: (bi, 0, 0)),
        out_shape=jax.ShapeDtypeStruct((b, l, MIX_Q), F32),
        scratch_shapes=[pltpu.VMEM((l + 2 * POOL_HALO, LANES), F32)],
        compiler_params=_cparams("parallel"),
        name="pool_windows",
    )(bp)


def _outproj_kernel(x_ref, f_ref, p_ref, a_ref, sg_ref, mod_ref, w_ref, o_ref, *, d, ctx_row):
    row = ctx_row if ctx_row is not None else pl.program_id(0)
    gate = mod_ref[pl.ds(row, 1), 2 * d:]
    sg = sg_ref[...].astype(F32)
    yf = (f_ref[...] * sg[:, :MIX_Q]).astype(BF16)
    yp = (p_ref[...] * sg[:, MIX_Q:2 * MIX_Q]).astype(BF16)
    ya = (a_ref[...].astype(F32) * sg[:, 2 * MIX_Q:]).astype(BF16)
    acc = jnp.dot(yf, w_ref[0:MIX_Q, :], preferred_element_type=F32)
    acc += jnp.dot(yp, w_ref[MIX_Q:2 * MIX_Q, :], preferred_element_type=F32)
    acc += jnp.dot(ya, w_ref[2 * MIX_Q:, :], preferred_element_type=F32)
    o_ref[...] = x_ref[...] + gate * acc


def _out_projection(x, four, pool, attn, sg, mod, w_out, *, ctx_row, tl):
    b, l, d = x.shape
    row_blk = lambda width: pl.BlockSpec((None, tl, width), lambda bi, i: (bi, i, 0))
    full = lambda arr: pl.BlockSpec(arr.shape, lambda bi, i: (0,) * arr.ndim)
    return pl.pallas_call(
        functools.partial(_outproj_kernel, d=d, ctx_row=ctx_row),
        grid=(b, l // tl),
        in_specs=[row_blk(d), row_blk(MIX_Q), row_blk(MIX_Q), row_blk(ATTN_W), row_blk(d), full(mod), full(w_out)],
        out_specs=row_blk(d),
        out_shape=jax.ShapeDtypeStruct((b, l, d), F32),
        compiler_params=_cparams("parallel", "parallel"),
        name="out_projection_ctx" if ctx_row is not None else "out_projection",
    )(x, four, pool, attn, sg, mod, w_out)


def kernel(x, c, ctx, c_ctx, norm_g, w_mod, b_mod, w_in, w_fourier, w_pool, pool_scale, qk_norm_g, lam_vecs,
           subln_g, w_out):
    b, l, d = x.shape
    depth = w_in.shape[0]
    ctx_len = ctx.shape[1]
    ctx_row = b

    cc = jnp.concatenate([c, c_ctx[None, :], jnp.zeros((8 - b - 1, d), F32)], axis=0)
    mod = _modulation(cc, w_mod, b_mod)
    w_eff = _fold_weights(w_in, w_fourier, w_pool, pool_scale)
    w_out_bf = _cast_bf16(w_out)
    rope_tabs = _rope_tables(l // GRID_W)
    lane_chunk = lax.broadcasted_iota(jnp.int32, (LANES, LANES), 0) // QK_DIM
    ones_blk = (lane_chunk == lane_chunk.T).astype(BF16)
    n1 = 64
    n2 = l // n1

    for layer in range(depth):
        lam_init = 0.8 - 0.6 * math.exp(-0.3 * layer)
        update_ctx = layer < depth - 1
        ng = norm_g[layer][None, :]
        qk_gain = jnp.tile(qk_norm_g[layer], (1, LANES // QK_DIM))
        sub_g = subln_g[layer][:, None]
        lv = lam_vecs[layer]

        uv, bp, qt, k, vt, sg = _in_projection(x, mod[layer], ng, w_eff[layer], qk_gain, ones_blk, rope_tabs,
                                               ctx_row=None, tl=512)
        uv_c, bp_c, qt_c, k_c, vt_c, sg_c = _in_projection(ctx, mod[layer], ng, w_eff[layer], qk_gain, ones_blk,
                                                           None, ctx_row=ctx_row, tl=ctx_len)
        gains = jnp.max(jnp.abs(qk_norm_g[layer]), axis=-1)
        score_bound = QK_DIM * gains[0] * gains[1] * (Q_SCALE * BF16_ROUNDING_SLACK)
        attn = _attention_dispatch(score_bound, qt, [(k_c, vt_c), (k, vt)], lv, sub_g, lam_init, tq=256,
                                  tk=ATTN_TK)
        four = _fourier_latent(uv, n1, n2)
        pool = _pool(bp)
        x_new = _out_projection(x, four, pool, attn, sg, mod[layer], w_out_bf[layer], ctx_row=None, tl=512)
        if update_ctx:
            attn_c = _attention_dispatch(score_bound, qt_c, [(k_c, vt_c)], lv, sub_g, lam_init, tq=ctx_len, tk=512)
            ctx = _out_projection(ctx, _fourier_dense(uv_c), _pool(bp_c), attn_c, sg_c, mod[layer],
                                  w_out_bf[layer], ctx_row=ctx_row, tl=ctx_len)
        x = x_new
    return x
```

```python
import functools
import math

import numpy as np
import jax
import jax.numpy as jnp
from jax import lax
from jax.experimental import pallas as pl
from jax.experimental.pallas import tpu as pltpu

GRID_W = 64
FOURIER_GROUPS = 4
GROUP_W = 64
MIX_Q = 256
POOL_WINDOWS = (2, 4, 8, 16)
QK_DIM = 64
HEAD_W = 2 * QK_DIM
N_HEADS = 4
ATTN_W = N_HEADS * HEAD_W
ROPE_THETA = 10000.0
EPS = 1e-6
Q_SCALE = QK_DIM ** -0.5 * math.log2(math.e)
FOURIER_UNROLL = 8
ATTN_TK = 512
MAX_UNSHIFTED_SCORE = 40.0
BF16_ROUNDING_SLACK = 1.02
U_OFF, V_OFF, P_OFF, Q_OFF, K_OFF, VV_OFF, G_OFF, W_EFF = 0, 256, 512, 768, 1280, 1792, 2304, 3328

LANES = 128
VMEM_LIMIT = 56 * 1024 * 1024
HI = lax.Precision.HIGHEST
F32 = jnp.float32
BF16 = jnp.bfloat16


def _cparams(*sem):
    return pltpu.CompilerParams(dimension_semantics=sem, vmem_limit_bytes=VMEM_LIMIT)


def _channel_dft():
    c = np.arange(GROUP_W)
    ang = 2.0 * np.pi * np.outer(c, c) / GROUP_W
    s = 1.0 / math.sqrt(GROUP_W)
    return np.cos(ang) * s, np.sin(ang) * s


def _stage_tables(n1, n2):
    n = n1 * n2
    k1 = np.arange(n1)
    i1 = np.arange(n1)
    j = np.arange(n2)
    theta = 2.0 * np.pi * (j[:, None, None] * k1[None, :, None] / n + k1[None, :, None] * i1[None, None, :] / n1)
    m1 = np.concatenate([np.cos(theta), -np.sin(theta)], axis=1) / math.sqrt(n)
    ang2 = 2.0 * np.pi * np.outer(j, j) / n2
    return m1.astype(np.float32), np.cos(ang2).astype(np.float32), np.sin(ang2).astype(np.float32)


def _rope_tables(rows):
    row = jnp.repeat(jnp.arange(rows), GRID_W).astype(F32)
    col = jnp.tile(jnp.arange(GRID_W), rows).astype(F32)
    half = QK_DIM // 2
    inv_freq = ROPE_THETA ** (-jnp.arange(0, half, 2, dtype=F32) / half)
    ang_r = row[:, None] * inv_freq[None, :]
    ang_c = col[:, None] * inv_freq[None, :]
    ang = jnp.concatenate([ang_r, ang_r, ang_c, ang_c], axis=-1)
    cos, sin = jnp.cos(ang), jnp.sin(ang)
    quarter = QK_DIM // 4
    first = (jnp.arange(QK_DIM) % half) < quarter
    sin_up = jnp.where(first, -sin, 0.0)
    sin_dn = jnp.where(first, 0.0, sin)
    rep = LANES // QK_DIM
    return tuple(jnp.tile(t, (1, rep)) for t in (cos, sin_up, sin_dn))


def _fold_kernel(w_ref, wf_ref, wp_ref, ps_ref, cd_ref, sd_ref, o_ref):
    cd, sd = cd_ref[...], sd_ref[...]
    for g in range(FOURIER_GROUPS):
        sl = slice(g * GROUP_W, (g + 1) * GROUP_W)
        wa = w_ref[:, sl]
        wf = wf_ref[g]
        cw = jnp.dot(cd, wf, precision=HI, preferred_element_type=F32)
        sw = jnp.dot(sd, wf, precision=HI, preferred_element_type=F32)
        o_ref[:, U_OFF + g * GROUP_W:U_OFF + (g + 1) * GROUP_W] = jnp.dot(
            wa, cw, precision=HI, preferred_element_type=F32).astype(o_ref.dtype)
        o_ref[:, V_OFF + g * GROUP_W:V_OFF + (g + 1) * GROUP_W] = jnp.dot(
            wa, sw, precision=HI, preferred_element_type=F32).astype(o_ref.dtype)
        wb = w_ref[:, MIX_Q + g * GROUP_W:MIX_Q + (g + 1) * GROUP_W]
        pw = jnp.dot(wb, wp_ref[g], precision=HI, preferred_element_type=F32) * ps_ref[:, sl]
        o_ref[:, P_OFF + g * GROUP_W:P_OFF + (g + 1) * GROUP_W] = pw.astype(o_ref.dtype)
    o_ref[:, Q_OFF:] = w_ref[:, 2 * MIX_Q:].astype(o_ref.dtype)


def _fold_weights(w_in, w_fourier, w_pool, pool_scale):
    depth, d, in_w = w_in.shape
    tr = 256
    cd, sd = _channel_dft()
    return pl.pallas_call(
        _fold_kernel,
        grid=(depth, d // tr),
        in_specs=[
            pl.BlockSpec((None, tr, in_w), lambda l, i: (l, i, 0)),
            pl.BlockSpec((None, FOURIER_GROUPS, GROUP_W, GROUP_W), lambda l, i: (l, 0, 0, 0)),
            pl.BlockSpec((None, FOURIER_GROUPS, GROUP_W, GROUP_W), lambda l, i: (l, 0, 0, 0)),
            pl.BlockSpec((None, 1, MIX_Q), lambda l, i: (l, 0, 0)),
            pl.BlockSpec((GROUP_W, GROUP_W), lambda l, i: (0, 0)),
            pl.BlockSpec((GROUP_W, GROUP_W), lambda l, i: (0, 0)),
        ],
        out_specs=pl.BlockSpec((None, tr, W_EFF), lambda l, i: (l, i, 0)),
        out_shape=jax.ShapeDtypeStruct((depth, d, W_EFF), BF16),
        compiler_params=_cparams("parallel", "parallel"),
        name="fold_weights",
    )(w_in, w_fourier, w_pool, pool_scale[:, None, :], jnp.asarray(cd, F32), jnp.asarray(sd, F32))


def _cast_kernel(w_ref, o_ref):
    o_ref[...] = w_ref[...].astype(o_ref.dtype)


def _cast_bf16(w):
    depth, r, c = w.shape
    return pl.pallas_call(
        _cast_kernel,
        grid=(depth,),
        in_specs=[pl.BlockSpec((None, r, c), lambda l: (l, 0, 0))],
        out_specs=pl.BlockSpec((None, r, c), lambda l: (l, 0, 0)),
        out_shape=jax.ShapeDtypeStruct(w.shape, BF16),
        compiler_params=_cparams("parallel"),
        name="cast_w_out",
    )(w)


def _mod_kernel(c_ref, w_ref, b_ref, o_ref):
    cc = c_ref[...]
    sc = cc * jax.nn.sigmoid(cc)
    o_ref[...] = jnp.dot(sc, w_ref[...], precision=HI, preferred_element_type=F32) + b_ref[...]


def _modulation(cc, w_mod, b_mod):
    depth, d, n = w_mod.shape
    tn = 1024
    rows = cc.shape[0]
    return pl.pallas_call(
        _mod_kernel,
        grid=(depth, n // tn),
        in_specs=[
            pl.BlockSpec((rows, d), lambda l, j: (0, 0)),
            pl.BlockSpec((None, d, tn), lambda l, j: (l, 0, j)),
            pl.BlockSpec((None, 1, tn), lambda l, j: (l, 0, j)),
        ],
        out_specs=pl.BlockSpec((None, rows, tn), lambda l, j: (l, 0, j)),
        out_shape=jax.ShapeDtypeStruct((depth, rows, n), F32),
        compiler_params=_cparams("parallel", "parallel"),
        name="modulation",
    )(cc, w_mod, b_mod[:, None, :])


def _qk_normalise(t, ones_blk, gain):
    ss = jnp.dot((t * t).astype(BF16), ones_blk, preferred_element_type=F32)
    return t * lax.rsqrt(ss * (1.0 / QK_DIM) + EPS) * gain


def _rotate(t, cos, sin_up, sin_dn):
    quarter = QK_DIM // 4
    up = pltpu.roll(t, LANES - quarter, 1)
    dn = pltpu.roll(t, quarter, 1)
    return t * cos + up * sin_up + dn * sin_dn


def _inproj_kernel(x_ref, mod_ref, ng_ref, w_ref, qkg_ref, ones_ref, *rest, d, ctx_row, rope):
    if rope:
        cos_ref, sup_ref, sdn_ref = rest[:3]
        rest = rest[3:]
    uv_ref, bp_ref, qt_ref, k_ref, vt_ref, sg_ref = rest
    row = ctx_row if ctx_row is not None else pl.program_id(0)
    mod = mod_ref[pl.ds(row, 1), :]
    shift, scale = mod[:, :d], mod[:, d:2 * d]
    xf = x_ref[...]
    ms = jnp.mean(xf * xf, axis=-1, keepdims=True)
    y = xf * lax.rsqrt(ms + EPS) * ng_ref[...]
    h = (y * (1.0 + scale) + shift).astype(BF16)

    def proj(lo, hi):
        return jnp.dot(h, w_ref[:, lo:hi], preferred_element_type=F32)

    uv_ref[...] = proj(U_OFF, P_OFF)
    bp_ref[...] = proj(P_OFF, Q_OFF)
    g = proj(G_OFF, W_EFF)
    sg_ref[...] = (g * jax.nn.sigmoid(g)).astype(sg_ref.dtype)

    ones_blk = ones_ref[...]
    if rope:
        cos, sup, sdn = cos_ref[...], sup_ref[...], sdn_ref[...]
    pair = 2 * LANES
    for p in range(ATTN_W // pair):
        qp = _qk_normalise(proj(Q_OFF + p * pair, Q_OFF + (p + 1) * pair), ones_blk, qkg_ref[0:1, :])
        kp = _qk_normalise(proj(K_OFF + p * pair, K_OFF + (p + 1) * pair), ones_blk, qkg_ref[1:2, :])
        vp = proj(VV_OFF + p * pair, VV_OFF + (p + 1) * pair)
        for half in range(2):
            sl = slice(p * pair + half * LANES, p * pair + (half + 1) * LANES)
            hs = slice(half * LANES, (half + 1) * LANES)
            qt, kt = qp[:, hs], kp[:, hs]
            if rope:
                qt = _rotate(qt, cos, sup, sdn)
                kt = _rotate(kt, cos, sup, sdn)
            qt_ref[sl, :] = (qt * Q_SCALE).T.astype(qt_ref.dtype)
            k_ref[:, sl] = kt.astype(k_ref.dtype)
            vt_ref[sl, :] = vp[:, hs].T.astype(vt_ref.dtype)


def _in_projection(x, mod, norm_g, w_eff, qk_gain, ones_blk, rope_tabs, *, ctx_row, tl):
    b, l, d = x.shape
    rope = rope_tabs is not None
    row_blk = lambda width: pl.BlockSpec((None, tl, width), lambda bi, i: (bi, i, 0))
    full = lambda arr: pl.BlockSpec(arr.shape, lambda bi, i: (0,) * arr.ndim)
    in_specs = [row_blk(d), full(mod), full(norm_g), full(w_eff), full(qk_gain), full(ones_blk)]
    args = [x, mod, norm_g, w_eff, qk_gain, ones_blk]
    if rope:
        in_specs += [pl.BlockSpec((tl, LANES), lambda bi, i: (i, 0))] * 3
        args += list(rope_tabs)
    out_shape = (
        jax.ShapeDtypeStruct((b, l, 2 * MIX_Q), F32),
        jax.ShapeDtypeStruct((b, l, MIX_Q), F32),
        jax.ShapeDtypeStruct((b, ATTN_W, l), BF16),
        jax.ShapeDtypeStruct((b, l, ATTN_W), BF16),
        jax.ShapeDtypeStruct((b, ATTN_W, l), BF16),
        jax.ShapeDtypeStruct((b, l, d), BF16),
    )
    col_blk = pl.BlockSpec((None, ATTN_W, tl), lambda bi, i: (bi, 0, i))
    out_specs = (row_blk(2 * MIX_Q), row_blk(MIX_Q), col_blk, row_blk(ATTN_W), col_blk, row_blk(d))
    return pl.pallas_call(
        functools.partial(_inproj_kernel, d=d, ctx_row=ctx_row, rope=rope),
        grid=(b, l // tl),
        in_specs=in_specs,
        out_specs=out_specs,
        out_shape=out_shape,
        compiler_params=_cparams("parallel", "parallel"),
        name="in_projection_ctx" if ctx_row is not None else "in_projection",
    )(*args)


def _attn_kernel(*refs, n_seg, seg_len, tk, lam_init):
    qt_ref = refs[0]
    kv_refs = refs[1:1 + 2 * n_seg]
    lam_ref, sub_ref, o_ref, m_ref, l_ref, acc_ref = refs[1 + 2 * n_seg:]
    qt = qt_ref[...]
    first_map = lax.broadcasted_iota(jnp.int32, qt.shape, 0) < QK_DIM
    zero = jnp.zeros_like(qt)
    ws = (jnp.where(first_map, qt, zero), jnp.where(first_map, zero, qt))

    m_ref[...] = jnp.full(m_ref.shape, -jnp.inf, F32)
    l_ref[...] = jnp.zeros(l_ref.shape, F32)
    acc_ref[...] = jnp.zeros(acc_ref.shape, F32)

    def chunk(k_ref, vt_ref, start, size):
        k = k_ref[pl.ds(start, size), :]
        vt = vt_ref[:, pl.ds(start, size)]
        for mp in range(2):
            s = jnp.dot(k, ws[mp], preferred_element_type=F32)
            m_old = m_ref[mp]
            m_new = jnp.maximum(m_old, jnp.max(s, axis=0, keepdims=True))
            alpha = jnp.exp2(m_old - m_new)
            e = jnp.exp2(s - m_new)
            l_ref[mp] = alpha * l_ref[mp] + jnp.sum(e, axis=0, keepdims=True)
            acc_ref[mp] = alpha * acc_ref[mp] + jnp.dot(vt, e.astype(BF16), preferred_element_type=F32)
            m_ref[mp] = m_new

    for si in range(n_seg):
        k_ref, vt_ref = kv_refs[2 * si], kv_refs[2 * si + 1]
        length = seg_len[si]
        size = min(tk, length)
        n_chunks = length // size
        if n_chunks == 1:
            chunk(k_ref, vt_ref, 0, size)
        else:
            def body(j, carry, k_ref=k_ref, vt_ref=vt_ref, size=size):
                chunk(k_ref, vt_ref, pl.multiple_of(j * size, size), size)
                return carry
            lax.fori_loop(0, n_chunks, body, 0)

    lv = lam_ref[...]
    lam = (jnp.exp(jnp.sum(lv[0:1] * lv[1:2], axis=-1, keepdims=True))
           - jnp.exp(jnp.sum(lv[2:3] * lv[3:4], axis=-1, keepdims=True)) + lam_init)
    o = acc_ref[0] / l_ref[0] - lam * (acc_ref[1] / l_ref[1])
    ms = jnp.mean(o * o, axis=0, keepdims=True)
    o = o * lax.rsqrt(ms + EPS) * (sub_ref[...] * (1.0 - lam_init))
    o_ref[...] = o.T.astype(o_ref.dtype)


def _attn_unshifted_kernel(*refs, n_seg, seg_len, tk, lam_init):
    qt_ref = refs[0]
    kv_refs = refs[1:1 + 2 * n_seg]
    lam_ref, sub_ref, o_ref, l_ref, acc_ref, sa_ref, sb_ref = refs[1 + 2 * n_seg:]
    qt = qt_ref[...]
    tq = qt.shape[1]
    first_map = lax.broadcasted_iota(jnp.int32, qt.shape, 0) < QK_DIM
    zero = jnp.zeros_like(qt)
    w = jnp.concatenate([jnp.where(first_map, qt, zero), jnp.where(first_map, zero, qt)], axis=1)

    l_ref[...] = jnp.zeros(l_ref.shape, F32)
    acc_ref[...] = jnp.zeros(acc_ref.shape, F32)

    def scores(s_ref, k_ref, start, size):
        s_ref[0:size, :] = jnp.dot(k_ref[pl.ds(start, size), :], w, preferred_element_type=F32)

    def consume(s_ref, vt_ref, start, size):
        e = jnp.exp2(s_ref[0:size, :])
        l_ref[...] += jnp.sum(e.reshape(size // 8, 8, 2 * tq), axis=0)
        acc_ref[...] += jnp.dot(vt_ref[:, pl.ds(start, size)], e.astype(BF16), preferred_element_type=F32)

    chunks = []
    for i in range(n_seg):
        size = min(tk, seg_len[i])
        assert seg_len[i] % size == 0
        chunks += [(kv_refs[2 * i], kv_refs[2 * i + 1], c * size, size) for c in range(seg_len[i] // size)]
    bufs = (sa_ref, sb_ref)
    scores(bufs[0], chunks[0][0], chunks[0][2], chunks[0][3])
    for i, (_, vt_ref, start, size) in enumerate(chunks):
        if i + 1 < len(chunks):
            nxt = chunks[i + 1]
            scores(bufs[(i + 1) % 2], nxt[0], nxt[2], nxt[3])
        consume(bufs[i % 2], vt_ref, start, size)

    lv = lam_ref[...]
    lam = (jnp.exp(jnp.sum(lv[0:1] * lv[1:2], axis=-1, keepdims=True))
           - jnp.exp(jnp.sum(lv[2:3] * lv[3:4], axis=-1, keepdims=True)) + lam_init)
    lsum = jnp.sum(l_ref[...], axis=0, keepdims=True)
    acc = acc_ref[...]
    o = acc[:, :tq] / lsum[:, :tq] - lam * (acc[:, tq:] / lsum[:, tq:])
    ms = jnp.mean(o * o, axis=0, keepdims=True)
    o = o * lax.rsqrt(ms + EPS) * (sub_ref[...] * (1.0 - lam_init))
    o_ref[...] = o.T.astype(o_ref.dtype)


def _attention(qt, segs, lam_vecs, subln_col, lam_init, *, tq, tk, shifted):
    b, _, lq = qt.shape
    in_specs = [pl.BlockSpec((None, HEAD_W, tq), lambda bi, h, i: (bi, h, i))]
    args = [qt]
    seg_len = []
    for k, vt in segs:
        lk = k.shape[1]
        seg_len.append(lk)
        in_specs.append(pl.BlockSpec((None, lk, HEAD_W), lambda bi, h, i: (bi, 0, h)))
        in_specs.append(pl.BlockSpec((None, HEAD_W, lk), lambda bi, h, i: (bi, h, 0)))
        args += [k, vt]
    in_specs += [pl.BlockSpec(lam_vecs.shape, lambda bi, h, i: (0, 0)),
                 pl.BlockSpec(subln_col.shape, lambda bi, h, i: (0, 0))]
    args += [lam_vecs, subln_col]
    if shifted:
        body = _attn_kernel
        scratch = [pltpu.VMEM((2, 1, tq), F32), pltpu.VMEM((2, 1, tq), F32), pltpu.VMEM((2, HEAD_W, tq), F32)]
    else:
        body = _attn_unshifted_kernel
        s_rows = min(tk, max(seg_len))
        scratch = [pltpu.VMEM((8, 2 * tq), F32), pltpu.VMEM((HEAD_W, 2 * tq), F32),
                   pltpu.VMEM((s_rows, 2 * tq), F32), pltpu.VMEM((s_rows, 2 * tq), F32)]
    return pl.pallas_call(
        functools.partial(body, n_seg=len(segs), seg_len=tuple(seg_len), tk=tk, lam_init=lam_init),
        grid=(b, N_HEADS, lq // tq),
        in_specs=in_specs,
        out_specs=pl.BlockSpec((None, tq, HEAD_W), lambda bi, h, i: (bi, i, h)),
        out_shape=jax.ShapeDtypeStruct((b, lq, ATTN_W), BF16),
        scratch_shapes=scratch,
        compiler_params=_cparams("parallel", "parallel", "arbitrary"),
        name="diff_attention_shifted" if shifted else "diff_attention",
    )(*args)


def _attention_dispatch(score_bound, *args, **kw):
    return lax.cond(score_bound <= MAX_UNSHIFTED_SCORE,
                    lambda: _attention(*args, shifted=False, **kw),
                    lambda: _attention(*args, shifted=True, **kw))


def _fourier_kernel(u_ref, v_ref, m1_ref, c2_ref, s2_ref, o_ref, yr_ref, yi_ref, *, n1, n2):
    def stage1(j, carry):
        xu = u_ref[pl.ds(j, n1, stride=n2), :]
        xv = v_ref[pl.ds(j, n1, stride=n2), :]
        x = jnp.concatenate([xu, xv], axis=1).astype(BF16)
        p = jnp.dot(m1_ref[j], x, preferred_element_type=F32)
        rows = pl.ds(pl.multiple_of(j * n1, n1), n1)
        yr_ref[rows, :] = p[:n1, :LANES] + p[n1:, LANES:]
        yi_ref[rows, :] = p[n1:, :LANES] - p[:n1, LANES:]
        return carry

    lax.fori_loop(0, n2, stage1, 0, unroll=FOURIER_UNROLL)
    c2, s2 = c2_ref[...], s2_ref[...]

    def stage2(k1, carry):
        rows = pl.ds(k1, n2, stride=n1)
        xr = (jnp.dot(c2, yr_ref[rows, :].astype(BF16), preferred_element_type=F32)
              + jnp.dot(s2, yi_ref[rows, :].astype(BF16), preferred_element_type=F32))
        o_ref[rows, :] = xr
        return carry

    lax.fori_loop(0, n1, stage2, 0, unroll=FOURIER_UNROLL)


def _fourier_latent(uv, n1, n2):
    b, l, _ = uv.shape
    m1, c2, s2 = _stage_tables(n1, n2)
    m1, c2, s2 = (jnp.asarray(t).astype(BF16) for t in (m1, c2, s2))
    halves = MIX_Q // LANES
    return pl.pallas_call(
        functools.partial(_fourier_kernel, n1=n1, n2=n2),
        grid=(b, halves),
        in_specs=[
            pl.BlockSpec((None, l, LANES), lambda bi, c: (bi, 0, c)),
            pl.BlockSpec((None, l, LANES), lambda bi, c: (bi, 0, c + halves)),
            pl.BlockSpec(m1.shape, lambda bi, c: (0, 0, 0)),
            pl.BlockSpec(c2.shape, lambda bi, c: (0, 0)),
            pl.BlockSpec(s2.shape, lambda bi, c: (0, 0)),
        ],
        out_specs=pl.BlockSpec((None, l, LANES), lambda bi, c: (bi, 0, c)),
        out_shape=jax.ShapeDtypeStruct((b, l, MIX_Q), F32),
        scratch_shapes=[pltpu.VMEM((l, LANES), F32), pltpu.VMEM((l, LANES), F32)],
        compiler_params=_cparams("parallel", "parallel"),
        name="fourier_positions",
    )(uv, uv, m1, c2, s2)


def _fourier_dense_kernel(uv_ref, c_ref, s_ref, o_ref):
    uv = uv_ref[...].astype(BF16)
    o_ref[...] = (jnp.dot(c_ref[...], uv[:, :MIX_Q], preferred_element_type=F32)
                  - jnp.dot(s_ref[...], uv[:, MIX_Q:], preferred_element_type=F32))


def _fourier_dense(uv):
    b, l, _ = uv.shape
    n = np.arange(l)
    ang = 2.0 * np.pi * np.outer(n, n) / l
    c = jnp.asarray((np.cos(ang) / math.sqrt(l)).astype(np.float32)).astype(BF16)
    s = jnp.asarray((np.sin(ang) / math.sqrt(l)).astype(np.float32)).astype(BF16)
    return pl.pallas_call(
        _fourier_dense_kernel,
        grid=(b,),
        in_specs=[pl.BlockSpec((None, l, 2 * MIX_Q), lambda bi: (bi, 0, 0)),
                  pl.BlockSpec((l, l), lambda bi: (0, 0)),
                  pl.BlockSpec((l, l), lambda bi: (0, 0))],
        out_specs=pl.BlockSpec((None, l, MIX_Q), lambda bi: (bi, 0, 0)),
        out_shape=jax.ShapeDtypeStruct((b, l, MIX_Q), F32),
        compiler_params=_cparams("parallel"),
        name="fourier_positions_ctx",
    )(uv, c, s)


POOL_HALO = 8


def _pool_kernel(b_ref, o_ref, pad_ref, *, l, rows):
    zeros = jnp.zeros((POOL_HALO, LANES), F32)
    lane = lax.broadcasted_iota(jnp.int32, (rows, LANES), 1)
    low_group = lane < GROUP_W
    n_chunks = l // rows
    for tile in range(MIX_Q // LANES):
        w_lo, w_hi = POOL_WINDOWS[2 * tile], POOL_WINDOWS[2 * tile + 1]
        pad_ref[0:POOL_HALO, :] = zeros
        pad_ref[POOL_HALO + l:, :] = zeros
        pad_ref[POOL_HALO:POOL_HALO + l, :] = b_ref[:, tile * LANES:(tile + 1) * LANES]

        def body(ci, carry, tile=tile, w_lo=w_lo, w_hi=w_hi):
            t0 = pl.multiple_of(ci * rows, rows)
            pos = t0 + lax.broadcasted_iota(jnp.int32, (rows, LANES), 0)

            def window_mean(w):
                half = w // 2
                s = pad_ref[pl.ds(t0 + POOL_HALO - half, rows), :]
                for dlt in range(-half + 1, half):
                    s = s + pad_ref[pl.ds(t0 + POOL_HALO + dlt, rows), :]
                cnt = jnp.minimum(pos + (half - 1), l - 1) - jnp.maximum(pos - half, 0) + 1
                return s / cnt.astype(F32)

            centre = pad_ref[pl.ds(t0 + POOL_HALO, rows), :]
            pooled = jnp.where(low_group, window_mean(w_lo), window_mean(w_hi)) - centre
            o_ref[pl.ds(t0, rows), tile * LANES:(tile + 1) * LANES] = pooled
            return carry

        lax.fori_loop(0, n_chunks, body, 0)


def _pool(bp):
    b, l, _ = bp.shape
    rows = 256
    return pl.pallas_call(
        functools.partial(_pool_kernel, l=l, rows=rows),
        grid=(b,),
        in_specs=[pl.BlockSpec((None, l, MIX_Q), lambda bi: (bi, 0, 0))],
        out_specs=pl.BlockSpec((None, l, MIX_Q), lambda bi: (bi, 0, 0)),
        out_shape=jax.ShapeDtypeStruct((b, l, MIX_Q), F32),
        scratch_shapes=[pltpu.VMEM((l + 2 * POOL_HALO, LANES), F32)],
        compiler_params=_cparams("parallel"),
        name="pool_windows",
    )(bp)


def _outproj_kernel(x_ref, f_ref, p_ref, a_ref, sg_ref, mod_ref, w_ref, o_ref, *, d, ctx_row):
    row = ctx_row if ctx_row is not None else pl.program_id(0)
    gate = mod_ref[pl.ds(row, 1), 2 * d:]
    sg = sg_ref[...].astype(F32)
    yf = (f_ref[...] * sg[:, :MIX_Q]).astype(BF16)
    yp = (p_ref[...] * sg[:, MIX_Q:2 * MIX_Q]).astype(BF16)
    ya = (a_ref[...].astype(F32) * sg[:, 2 * MIX_Q:]).astype(BF16)
    acc = jnp.dot(yf, w_ref[0:MIX_Q, :], preferred_element_type=F32)
    acc += jnp.dot(yp, w_ref[MIX_Q:2 * MIX_Q, :], preferred_element_type=F32)
    acc += jnp.dot(ya, w_ref[2 * MIX_Q:, :], preferred_element_type=F32)
    o_ref[...] = x_ref[...] + gate * acc


def _out_projection(x, four, pool, attn, sg, mod, w_out, *, ctx_row, tl):
    b, l, d = x.shape
    row_blk = lambda width: pl.BlockSpec((None, tl, width), lambda bi, i: (bi, i, 0))
    full = lambda arr: pl.BlockSpec(arr.shape, lambda bi, i: (0,) * arr.ndim)
    return pl.pallas_call(
        functools.partial(_outproj_kernel, d=d, ctx_row=ctx_row),
        grid=(b, l // tl),
        in_specs=[row_blk(d), row_blk(MIX_Q), row_blk(MIX_Q), row_blk(ATTN_W), row_blk(d), full(mod), full(w_out)],
        out_specs=row_blk(d),
        out_shape=jax.ShapeDtypeStruct((b, l, d), F32),
        compiler_params=_cparams("parallel", "parallel"),
        name="out_projection_ctx" if ctx_row is not None else "out_projection",
    )(x, four, pool, attn, sg, mod, w_out)


def kernel(x, c, ctx, c_ctx, norm_g, w_mod, b_mod, w_in, w_fourier, w_pool, pool_scale, qk_norm_g, lam_vecs,
           subln_g, w_out):
    b, l, d = x.shape
    depth = w_in.shape[0]
    ctx_len = ctx.shape[1]
    ctx_row = b

    cc = jnp.concatenate([c, c_ctx[None, :], jnp.zeros((8 - b - 1, d), F32)], axis=0)
    mod = _modulation(cc, w_mod, b_mod)
    w_eff = _fold_weights(w_in, w_fourier, w_pool, pool_scale)
    w_out_bf = _cast_bf16(w_out)
    rope_tabs = _rope_tables(l // GRID_W)
    lane_chunk = lax.broadcasted_iota(jnp.int32, (2 * LANES, 2 * LANES), 0) // QK_DIM
    ones_blk = (lane_chunk == lane_chunk.T).astype(BF16)
    n1 = 64
    n2 = l // n1

    for layer in range(depth):
        lam_init = 0.8 - 0.6 * math.exp(-0.3 * layer)
        update_ctx = layer < depth - 1
        ng = norm_g[layer][None, :]
        qk_gain = jnp.tile(qk_norm_g[layer], (1, 2 * LANES // QK_DIM))
        sub_g = subln_g[layer][:, None]
        lv = lam_vecs[layer]

        uv, bp, qt, k, vt, sg = _in_projection(x, mod[layer], ng, w_eff[layer], qk_gain, ones_blk, rope_tabs,
                                               ctx_row=None, tl=512)
        uv_c, bp_c, qt_c, k_c, vt_c, sg_c = _in_projection(ctx, mod[layer], ng, w_eff[layer], qk_gain, ones_blk,
                                                           None, ctx_row=ctx_row, tl=ctx_len)
        gains = jnp.max(jnp.abs(qk_norm_g[layer]), axis=-1)
        score_bound = QK_DIM * gains[0] * gains[1] * (Q_SCALE * BF16_ROUNDING_SLACK)
        attn = _attention_dispatch(score_bound, qt, [(k_c, vt_c), (k, vt)], lv, sub_g, lam_init, tq=256,
                                  tk=ATTN_TK)
        four = _fourier_latent(uv, n1, n2)
        pool = _pool(bp)
        x_new = _out_projection(x, four, pool, attn, sg, mod[layer], w_out_bf[layer], ctx_row=None, tl=512)
        if update_ctx:
            attn_c = _attention_dispatch(score_bound, qt_c, [(k_c, vt_c)], lv, sub_g, lam_init, tq=ctx_len, tk=512)
            ctx = _out_projection(ctx, _fourier_dense(uv_c), _pool(bp_c), attn_c, sg_c, mod[layer],
                                  w_out_bf[layer], ctx_row=ctx_row, tl=ctx_len)
        x = x_new
    return x
```

```python
import functools
import math

import numpy as np
import jax
import jax.numpy as jnp
from jax import lax
from jax.experimental import pallas as pl
from jax.experimental.pallas import tpu as pltpu

GRID_W = 64
FOURIER_GROUPS = 4
GROUP_W = 64
MIX_Q = 256
POOL_WINDOWS = (2, 4, 8, 16)
QK_DIM = 64
HEAD_W = 2 * QK_DIM
N_HEADS = 4
ATTN_W = N_HEADS * HEAD_W
ROPE_THETA = 10000.0
EPS = 1e-6
Q_SCALE = QK_DIM ** -0.5 * math.log2(math.e)
FOURIER_UNROLL = 8
ATTN_TK = 512
ATTN_TQ = 512
MAX_UNSHIFTED_SCORE = 40.0
BF16_ROUNDING_SLACK = 1.02
U_OFF, V_OFF, P_OFF, Q_OFF, K_OFF, VV_OFF, G_OFF, W_EFF = 0, 256, 512, 768, 1280, 1792, 2304, 3328

LANES = 128
VMEM_LIMIT = 56 * 1024 * 1024
HI = lax.Precision.HIGHEST
F32 = jnp.float32
BF16 = jnp.bfloat16


def _cparams(*sem):
    return pltpu.CompilerParams(dimension_semantics=sem, vmem_limit_bytes=VMEM_LIMIT)


def _channel_dft():
    c = np.arange(GROUP_W)
    ang = 2.0 * np.pi * np.outer(c, c) / GROUP_W
    s = 1.0 / math.sqrt(GROUP_W)
    return np.cos(ang) * s, np.sin(ang) * s


def _stage_tables(n1, n2):
    n = n1 * n2
    k1 = np.arange(n1)
    i1 = np.arange(n1)
    j = np.arange(n2)
    theta = 2.0 * np.pi * (j[:, None, None] * k1[None, :, None] / n + k1[None, :, None] * i1[None, None, :] / n1)
    m1 = np.concatenate([np.cos(theta), -np.sin(theta)], axis=1) / math.sqrt(n)
    ang2 = 2.0 * np.pi * np.outer(j, j) / n2
    return m1.astype(np.float32), np.cos(ang2).astype(np.float32), np.sin(ang2).astype(np.float32)


def _rope_tables(rows):
    row = jnp.repeat(jnp.arange(rows), GRID_W).astype(F32)
    col = jnp.tile(jnp.arange(GRID_W), rows).astype(F32)
    half = QK_DIM // 2
    inv_freq = ROPE_THETA ** (-jnp.arange(0, half, 2, dtype=F32) / half)
    ang_r = row[:, None] * inv_freq[None, :]
    ang_c = col[:, None] * inv_freq[None, :]
    ang = jnp.concatenate([ang_r, ang_r, ang_c, ang_c], axis=-1)
    cos, sin = jnp.cos(ang), jnp.sin(ang)
    quarter = QK_DIM // 4
    first = (jnp.arange(QK_DIM) % half) < quarter
    sin_up = jnp.where(first, -sin, 0.0)
    sin_dn = jnp.where(first, 0.0, sin)
    rep = LANES // QK_DIM
    return tuple(jnp.tile(t, (1, rep)) for t in (cos, sin_up, sin_dn))


def _fold_kernel(w_ref, wf_ref, wp_ref, ps_ref, cd_ref, sd_ref, o_ref):
    cd, sd = cd_ref[...], sd_ref[...]
    for g in range(FOURIER_GROUPS):
        sl = slice(g * GROUP_W, (g + 1) * GROUP_W)
        wa = w_ref[:, sl]
        wf = wf_ref[g]
        cw = jnp.dot(cd, wf, precision=HI, preferred_element_type=F32)
        sw = jnp.dot(sd, wf, precision=HI, preferred_element_type=F32)
        o_ref[:, U_OFF + g * GROUP_W:U_OFF + (g + 1) * GROUP_W] = jnp.dot(
            wa, cw, precision=HI, preferred_element_type=F32).astype(o_ref.dtype)
        o_ref[:, V_OFF + g * GROUP_W:V_OFF + (g + 1) * GROUP_W] = jnp.dot(
            wa, sw, precision=HI, preferred_element_type=F32).astype(o_ref.dtype)
        wb = w_ref[:, MIX_Q + g * GROUP_W:MIX_Q + (g + 1) * GROUP_W]
        pw = jnp.dot(wb, wp_ref[g], precision=HI, preferred_element_type=F32) * ps_ref[:, sl]
        o_ref[:, P_OFF + g * GROUP_W:P_OFF + (g + 1) * GROUP_W] = pw.astype(o_ref.dtype)
    o_ref[:, Q_OFF:] = w_ref[:, 2 * MIX_Q:].astype(o_ref.dtype)


def _fold_weights(w_in, w_fourier, w_pool, pool_scale):
    depth, d, in_w = w_in.shape
    tr = 256
    cd, sd = _channel_dft()
    return pl.pallas_call(
        _fold_kernel,
        grid=(depth, d // tr),
        in_specs=[
            pl.BlockSpec((None, tr, in_w), lambda l, i: (l, i, 0)),
            pl.BlockSpec((None, FOURIER_GROUPS, GROUP_W, GROUP_W), lambda l, i: (l, 0, 0, 0)),
            pl.BlockSpec((None, FOURIER_GROUPS, GROUP_W, GROUP_W), lambda l, i: (l, 0, 0, 0)),
            pl.BlockSpec((None, 1, MIX_Q), lambda l, i: (l, 0, 0)),
            pl.BlockSpec((GROUP_W, GROUP_W), lambda l, i: (0, 0)),
            pl.BlockSpec((GROUP_W, GROUP_W), lambda l, i: (0, 0)),
        ],
        out_specs=pl.BlockSpec((None, tr, W_EFF), lambda l, i: (l, i, 0)),
        out_shape=jax.ShapeDtypeStruct((depth, d, W_EFF), BF16),
        compiler_params=_cparams("parallel", "parallel"),
        name="fold_weights",
    )(w_in, w_fourier, w_pool, pool_scale[:, None, :], jnp.asarray(cd, F32), jnp.asarray(sd, F32))


def _cast_kernel(w_ref, o_ref):
    o_ref[...] = w_ref[...].astype(o_ref.dtype)


def _cast_bf16(w):
    depth, r, c = w.shape
    return pl.pallas_call(
        _cast_kernel,
        grid=(depth,),
        in_specs=[pl.BlockSpec((None, r, c), lambda l: (l, 0, 0))],
        out_specs=pl.BlockSpec((None, r, c), lambda l: (l, 0, 0)),
        out_shape=jax.ShapeDtypeStruct(w.shape, BF16),
        compiler_params=_cparams("parallel"),
        name="cast_w_out",
    )(w)


def _mod_kernel(c_ref, w_ref, b_ref, o_ref):
    cc = c_ref[...]
    sc = cc * jax.nn.sigmoid(cc)
    o_ref[...] = jnp.dot(sc, w_ref[...], precision=HI, preferred_element_type=F32) + b_ref[...]


def _modulation(cc, w_mod, b_mod):
    depth, d, n = w_mod.shape
    tn = 1024
    rows = cc.shape[0]
    return pl.pallas_call(
        _mod_kernel,
        grid=(depth, n // tn),
        in_specs=[
            pl.BlockSpec((rows, d), lambda l, j: (0, 0)),
            pl.BlockSpec((None, d, tn), lambda l, j: (l, 0, j)),
            pl.BlockSpec((None, 1, tn), lambda l, j: (l, 0, j)),
        ],
        out_specs=pl.BlockSpec((None, rows, tn), lambda l, j: (l, 0, j)),
        out_shape=jax.ShapeDtypeStruct((depth, rows, n), F32),
        compiler_params=_cparams("parallel", "parallel"),
        name="modulation",
    )(cc, w_mod, b_mod[:, None, :])


def _qk_normalise(t, ones_blk, gain):
    ss = jnp.dot((t * t).astype(BF16), ones_blk, preferred_element_type=F32)
    return t * lax.rsqrt(ss * (1.0 / QK_DIM) + EPS) * gain


def _rotate(t, cos, sin_up, sin_dn):
    quarter = QK_DIM // 4
    up = pltpu.roll(t, LANES - quarter, 1)
    dn = pltpu.roll(t, quarter, 1)
    return t * cos + up * sin_up + dn * sin_dn


def _inproj_kernel(x_ref, mod_ref, ng_ref, w_ref, qkg_ref, ones_ref, *rest, d, ctx_row, rope):
    if rope:
        cos_ref, sup_ref, sdn_ref = rest[:3]
        rest = rest[3:]
    uv_ref, bp_ref, qt_ref, k_ref, vt_ref, sg_ref = rest
    row = ctx_row if ctx_row is not None else pl.program_id(0)
    mod = mod_ref[pl.ds(row, 1), :]
    shift, scale = mod[:, :d], mod[:, d:2 * d]
    xf = x_ref[...]
    ms = jnp.mean(xf * xf, axis=-1, keepdims=True)
    y = xf * lax.rsqrt(ms + EPS) * ng_ref[...]
    h = (y * (1.0 + scale) + shift).astype(BF16)

    def proj(lo, hi):
        return jnp.dot(h, w_ref[:, lo:hi], preferred_element_type=F32)

    uv_ref[...] = proj(U_OFF, P_OFF)
    bp_ref[...] = proj(P_OFF, Q_OFF)
    g = proj(G_OFF, W_EFF)
    sg_ref[...] = (g * jax.nn.sigmoid(g)).astype(sg_ref.dtype)

    ones_blk = ones_ref[...]
    if rope:
        cos, sup, sdn = cos_ref[...], sup_ref[...], sdn_ref[...]
    pair = 2 * LANES
    for p in range(ATTN_W // pair):
        qp = _qk_normalise(proj(Q_OFF + p * pair, Q_OFF + (p + 1) * pair), ones_blk, qkg_ref[0:1, :])
        kp = _qk_normalise(proj(K_OFF + p * pair, K_OFF + (p + 1) * pair), ones_blk, qkg_ref[1:2, :])
        vp = proj(VV_OFF + p * pair, VV_OFF + (p + 1) * pair)
        for half in range(2):
            sl = slice(p * pair + half * LANES, p * pair + (half + 1) * LANES)
            hs = slice(half * LANES, (half + 1) * LANES)
            qt, kt = qp[:, hs], kp[:, hs]
            if rope:
                qt = _rotate(qt, cos, sup, sdn)
                kt = _rotate(kt, cos, sup, sdn)
            qt_ref[sl, :] = (qt * Q_SCALE).T.astype(qt_ref.dtype)
            k_ref[:, sl] = kt.astype(k_ref.dtype)
            vt_ref[sl, :] = vp[:, hs].T.astype(vt_ref.dtype)


def _in_projection(x, mod, norm_g, w_eff, qk_gain, ones_blk, rope_tabs, *, layer, ctx_row, tl):
    b, l, d = x.shape
    rope = rope_tabs is not None
    row_blk = lambda width: pl.BlockSpec((None, tl, width), lambda bi, i: (bi, i, 0))
    full = lambda arr: pl.BlockSpec(arr.shape, lambda bi, i: (0,) * arr.ndim)
    of_layer = lambda arr: pl.BlockSpec((None,) + arr.shape[1:], lambda bi, i: (layer,) + (0,) * (arr.ndim - 1))
    in_specs = [row_blk(d), of_layer(mod), full(norm_g), of_layer(w_eff), full(qk_gain), full(ones_blk)]
    args = [x, mod, norm_g, w_eff, qk_gain, ones_blk]
    if rope:
        in_specs += [pl.BlockSpec((tl, LANES), lambda bi, i: (i, 0))] * 3
        args += list(rope_tabs)
    out_shape = (
        jax.ShapeDtypeStruct((b, l, 2 * MIX_Q), F32),
        jax.ShapeDtypeStruct((b, l, MIX_Q), F32),
        jax.ShapeDtypeStruct((b, ATTN_W, l), BF16),
        jax.ShapeDtypeStruct((b, l, ATTN_W), BF16),
        jax.ShapeDtypeStruct((b, ATTN_W, l), BF16),
        jax.ShapeDtypeStruct((b, l, d), BF16),
    )
    col_blk = pl.BlockSpec((None, ATTN_W, tl), lambda bi, i: (bi, 0, i))
    out_specs = (row_blk(2 * MIX_Q), row_blk(MIX_Q), col_blk, row_blk(ATTN_W), col_blk, row_blk(d))
    return pl.pallas_call(
        functools.partial(_inproj_kernel, d=d, ctx_row=ctx_row, rope=rope),
        grid=(b, l // tl),
        in_specs=in_specs,
        out_specs=out_specs,
        out_shape=out_shape,
        compiler_params=_cparams("parallel", "parallel"),
        name="in_projection_ctx" if ctx_row is not None else "in_projection",
    )(*args)


def _attn_kernel(*refs, n_seg, seg_len, tk, lam_init):
    qt_ref = refs[0]
    kv_refs = refs[1:1 + 2 * n_seg]
    lam_ref, sub_ref, o_ref, m_ref, l_ref, acc_ref = refs[1 + 2 * n_seg:]
    qt = qt_ref[...]
    first_map = lax.broadcasted_iota(jnp.int32, qt.shape, 0) < QK_DIM
    zero = jnp.zeros_like(qt)
    ws = (jnp.where(first_map, qt, zero), jnp.where(first_map, zero, qt))

    m_ref[...] = jnp.full(m_ref.shape, -jnp.inf, F32)
    l_ref[...] = jnp.zeros(l_ref.shape, F32)
    acc_ref[...] = jnp.zeros(acc_ref.shape, F32)

    def chunk(k_ref, vt_ref, start, size):
        k = k_ref[pl.ds(start, size), :]
        vt = vt_ref[:, pl.ds(start, size)]
        for mp in range(2):
            s = jnp.dot(k, ws[mp], preferred_element_type=F32)
            m_old = m_ref[mp]
            m_new = jnp.maximum(m_old, jnp.max(s, axis=0, keepdims=True))
            alpha = jnp.exp2(m_old - m_new)
            e = jnp.exp2(s - m_new)
            l_ref[mp] = alpha * l_ref[mp] + jnp.sum(e, axis=0, keepdims=True)
            acc_ref[mp] = alpha * acc_ref[mp] + jnp.dot(vt, e.astype(BF16), preferred_element_type=F32)
            m_ref[mp] = m_new

    for si in range(n_seg):
        k_ref, vt_ref = kv_refs[2 * si], kv_refs[2 * si + 1]
        length = seg_len[si]
        size = min(tk, length)
        n_chunks = length // size
        if n_chunks == 1:
            chunk(k_ref, vt_ref, 0, size)
        else:
            def body(j, carry, k_ref=k_ref, vt_ref=vt_ref, size=size):
                chunk(k_ref, vt_ref, pl.multiple_of(j * size, size), size)
                return carry
            lax.fori_loop(0, n_chunks, body, 0)

    lv = lam_ref[...]
    lam = (jnp.exp(jnp.sum(lv[0:1] * lv[1:2], axis=-1, keepdims=True))
           - jnp.exp(jnp.sum(lv[2:3] * lv[3:4], axis=-1, keepdims=True)) + lam_init)
    o = acc_ref[0] / l_ref[0] - lam * (acc_ref[1] / l_ref[1])
    ms = jnp.mean(o * o, axis=0, keepdims=True)
    o = o * lax.rsqrt(ms + EPS) * (sub_ref[...] * (1.0 - lam_init))
    o_ref[...] = o.T.astype(o_ref.dtype)


def _attn_unshifted_kernel(*refs, n_seg, seg_len, tk, lam_init):
    qt_ref = refs[0]
    kv_refs = refs[1:1 + 2 * n_seg]
    lam_ref, sub_ref, o_ref, l_ref, acc_ref, sa_ref, sb_ref = refs[1 + 2 * n_seg:]
    qt = qt_ref[...]
    tq = qt.shape[1]
    first_map = lax.broadcasted_iota(jnp.int32, qt.shape, 0) < QK_DIM
    zero = jnp.zeros_like(qt)
    w = jnp.concatenate([jnp.where(first_map, qt, zero), jnp.where(first_map, zero, qt)], axis=1)

    l_ref[...] = jnp.zeros(l_ref.shape, F32)
    acc_ref[...] = jnp.zeros(acc_ref.shape, F32)

    def scores(s_ref, k_ref, start, size):
        s_ref[0:size, :] = jnp.dot(k_ref[pl.ds(start, size), :], w, preferred_element_type=F32)

    def consume(s_ref, vt_ref, start, size):
        e = jnp.exp2(s_ref[0:size, :])
        l_ref[...] += jnp.sum(e.reshape(size // 8, 8, 2 * tq), axis=0)
        acc_ref[...] += jnp.dot(vt_ref[:, pl.ds(start, size)], e.astype(BF16), preferred_element_type=F32)

    chunks = []
    for i in range(n_seg):
        size = min(tk, seg_len[i])
        assert seg_len[i] % size == 0
        chunks += [(kv_refs[2 * i], kv_refs[2 * i + 1], c * size, size) for c in range(seg_len[i] // size)]
    bufs = (sa_ref, sb_ref)
    scores(bufs[0], chunks[0][0], chunks[0][2], chunks[0][3])
    for i, (_, vt_ref, start, size) in enumerate(chunks):
        if i + 1 < len(chunks):
            nxt = chunks[i + 1]
            scores(bufs[(i + 1) % 2], nxt[0], nxt[2], nxt[3])
        consume(bufs[i % 2], vt_ref, start, size)

    lv = lam_ref[...]
    lam = (jnp.exp(jnp.sum(lv[0:1] * lv[1:2], axis=-1, keepdims=True))
           - jnp.exp(jnp.sum(lv[2:3] * lv[3:4], axis=-1, keepdims=True)) + lam_init)
    lsum = jnp.sum(l_ref[...], axis=0, keepdims=True)
    acc = acc_ref[...]
    o = acc[:, :tq] / lsum[:, :tq] - lam * (acc[:, tq:] / lsum[:, tq:])
    ms = jnp.mean(o * o, axis=0, keepdims=True)
    o = o * lax.rsqrt(ms + EPS) * (sub_ref[...] * (1.0 - lam_init))
    o_ref[...] = o.T.astype(o_ref.dtype)


def _attention(qt, segs, lam_vecs, subln_col, lam_init, *, tq, tk, shifted):
    b, _, lq = qt.shape
    in_specs = [pl.BlockSpec((None, HEAD_W, tq), lambda bi, h, i: (bi, h, i))]
    args = [qt]
    seg_len = []
    for k, vt in segs:
        lk = k.shape[1]
        seg_len.append(lk)
        in_specs.append(pl.BlockSpec((None, lk, HEAD_W), lambda bi, h, i: (bi, 0, h)))
        in_specs.append(pl.BlockSpec((None, HEAD_W, lk), lambda bi, h, i: (bi, h, 0)))
        args += [k, vt]
    in_specs += [pl.BlockSpec(lam_vecs.shape, lambda bi, h, i: (0, 0)),
                 pl.BlockSpec(subln_col.shape, lambda bi, h, i: (0, 0))]
    args += [lam_vecs, subln_col]
    if shifted:
        body = _attn_kernel
        scratch = [pltpu.VMEM((2, 1, tq), F32), pltpu.VMEM((2, 1, tq), F32), pltpu.VMEM((2, HEAD_W, tq), F32)]
    else:
        body = _attn_unshifted_kernel
        s_rows = min(tk, max(seg_len))
        scratch = [pltpu.VMEM((8, 2 * tq), F32), pltpu.VMEM((HEAD_W, 2 * tq), F32),
                   pltpu.VMEM((s_rows, 2 * tq), F32), pltpu.VMEM((s_rows, 2 * tq), F32)]
    return pl.pallas_call(
        functools.partial(body, n_seg=len(segs), seg_len=tuple(seg_len), tk=tk, lam_init=lam_init),
        grid=(b, N_HEADS, lq // tq),
        in_specs=in_specs,
        out_specs=pl.BlockSpec((None, tq, HEAD_W), lambda bi, h, i: (bi, i, h)),
        out_shape=jax.ShapeDtypeStruct((b, lq, ATTN_W), BF16),
        scratch_shapes=scratch,
        compiler_params=_cparams("parallel", "parallel", "arbitrary"),
        name="diff_attention_shifted" if shifted else "diff_attention",
    )(*args)


def _attention_dispatch(score_bound, *args, **kw):
    return lax.cond(score_bound <= MAX_UNSHIFTED_SCORE,
                    lambda: _attention(*args, shifted=False, **kw),
                    lambda: _attention(*args, shifted=True, **kw))


def _fourier_kernel(u_ref, v_ref, m1_ref, c2_ref, s2_ref, o_ref, yr_ref, yi_ref, *, n1, n2):
    def stage1(j, carry):
        xu = u_ref[pl.ds(j, n1, stride=n2), :]
        xv = v_ref[pl.ds(j, n1, stride=n2), :]
        x = jnp.concatenate([xu, xv], axis=1).astype(BF16)
        p = jnp.dot(m1_ref[j], x, preferred_element_type=F32)
        rows = pl.ds(pl.multiple_of(j * n1, n1), n1)
        yr_ref[rows, :] = p[:n1, :LANES] + p[n1:, LANES:]
        yi_ref[rows, :] = p[n1:, :LANES] - p[:n1, LANES:]
        return carry

    lax.fori_loop(0, n2, stage1, 0, unroll=FOURIER_UNROLL)
    c2, s2 = c2_ref[...], s2_ref[...]

    def stage2(k1, carry):
        rows = pl.ds(k1, n2, stride=n1)
        xr = (jnp.dot(c2, yr_ref[rows, :].astype(BF16), preferred_element_type=F32)
              + jnp.dot(s2, yi_ref[rows, :].astype(BF16), preferred_element_type=F32))
        o_ref[rows, :] = xr
        return carry

    lax.fori_loop(0, n1, stage2, 0, unroll=FOURIER_UNROLL)


def _fourier_latent(uv, n1, n2):
    b, l, _ = uv.shape
    m1, c2, s2 = _stage_tables(n1, n2)
    m1, c2, s2 = (jnp.asarray(t).astype(BF16) for t in (m1, c2, s2))
    halves = MIX_Q // LANES
    return pl.pallas_call(
        functools.partial(_fourier_kernel, n1=n1, n2=n2),
        grid=(b, halves),
        in_specs=[
            pl.BlockSpec((None, l, LANES), lambda bi, c: (bi, 0, c)),
            pl.BlockSpec((None, l, LANES), lambda bi, c: (bi, 0, c + halves)),
            pl.BlockSpec(m1.shape, lambda bi, c: (0, 0, 0)),
            pl.BlockSpec(c2.shape, lambda bi, c: (0, 0)),
            pl.BlockSpec(s2.shape, lambda bi, c: (0, 0)),
        ],
        out_specs=pl.BlockSpec((None, l, LANES), lambda bi, c: (bi, 0, c)),
        out_shape=jax.ShapeDtypeStruct((b, l, MIX_Q), F32),
        scratch_shapes=[pltpu.VMEM((l, LANES), F32), pltpu.VMEM((l, LANES), F32)],
        compiler_params=_cparams("parallel", "parallel"),
        name="fourier_positions",
    )(uv, uv, m1, c2, s2)


def _fourier_dense_kernel(uv_ref, c_ref, s_ref, o_ref):
    uv = uv_ref[...].astype(BF16)
    o_ref[...] = (jnp.dot(c_ref[...], uv[:, :MIX_Q], preferred_element_type=F32)
                  - jnp.dot(s_ref[...], uv[:, MIX_Q:], preferred_element_type=F32))


def _fourier_dense(uv):
    b, l, _ = uv.shape
    n = np.arange(l)
    ang = 2.0 * np.pi * np.outer(n, n) / l
    c = jnp.asarray((np.cos(ang) / math.sqrt(l)).astype(np.float32)).astype(BF16)
    s = jnp.asarray((np.sin(ang) / math.sqrt(l)).astype(np.float32)).astype(BF16)
    return pl.pallas_call(
        _fourier_dense_kernel,
        grid=(b,),
        in_specs=[pl.BlockSpec((None, l, 2 * MIX_Q), lambda bi: (bi, 0, 0)),
                  pl.BlockSpec((l, l), lambda bi: (0, 0)),
                  pl.BlockSpec((l, l), lambda bi: (0, 0))],
        out_specs=pl.BlockSpec((None, l, MIX_Q), lambda bi: (bi, 0, 0)),
        out_shape=jax.ShapeDtypeStruct((b, l, MIX_Q), F32),
        compiler_params=_cparams("parallel"),
        name="fourier_positions_ctx",
    )(uv, c, s)


POOL_HALO = 8


def _pool_kernel(b_ref, o_ref, pad_ref, *, l, rows):
    zeros = jnp.zeros((POOL_HALO, LANES), F32)
    lane = lax.broadcasted_iota(jnp.int32, (rows, LANES), 1)
    low_group = lane < GROUP_W
    n_chunks = l // rows
    for tile in range(MIX_Q // LANES):
        w_lo, w_hi = POOL_WINDOWS[2 * tile], POOL_WINDOWS[2 * tile + 1]
        pad_ref[0:POOL_HALO, :] = zeros
        pad_ref[POOL_HALO + l:, :] = zeros
        pad_ref[POOL_HALO:POOL_HALO + l, :] = b_ref[:, tile * LANES:(tile + 1) * LANES]

        def chunk(t0, edge, tile=tile, w_lo=w_lo, w_hi=w_hi):
            shifted = lambda dlt: pad_ref[pl.ds(t0 + POOL_HALO + dlt, rows), :]
            lo_range = range(-(w_lo // 2), w_lo // 2)
            inner = functools.reduce(lambda a, b: a + b, [shifted(d) for d in lo_range])
            outer = functools.reduce(lambda a, b: a + b,
                                     [shifted(d) for d in range(-(w_hi // 2), w_hi // 2) if d not in lo_range], inner)
            sums = jnp.where(low_group, inner, outer)
            if edge:
                pos = t0 + lax.broadcasted_iota(jnp.int32, (rows, LANES), 0)
                half = jnp.where(low_group, w_lo // 2, w_hi // 2)
                cnt = jnp.minimum(pos + (half - 1), l - 1) - jnp.maximum(pos - half, 0) + 1
                mean = sums / cnt.astype(F32)
            else:
                mean = sums * jnp.where(low_group, 1.0 / w_lo, 1.0 / w_hi)
            o_ref[pl.ds(t0, rows), tile * LANES:(tile + 1) * LANES] = mean - shifted(0)

        assert rows >= POOL_HALO
        chunk(0, True)
        if n_chunks > 1:
            def body(ci, carry, chunk=chunk):
                chunk(pl.multiple_of(ci * rows, rows), False)
                return carry

            lax.fori_loop(1, n_chunks - 1, body, 0)
            chunk((n_chunks - 1) * rows, True)


def _pool(bp):
    b, l, _ = bp.shape
    rows = 256
    return pl.pallas_call(
        functools.partial(_pool_kernel, l=l, rows=rows),
        grid=(b,),
        in_specs=[pl.BlockSpec((None, l, MIX_Q), lambda bi: (bi, 0, 0))],
        out_specs=pl.BlockSpec((None, l, MIX_Q), lambda bi: (bi, 0, 0)),
        out_shape=jax.ShapeDtypeStruct((b, l, MIX_Q), F32),
        scratch_shapes=[pltpu.VMEM((l + 2 * POOL_HALO, LANES), F32)],
        compiler_params=_cparams("parallel"),
        name="pool_windows",
    )(bp)


def _outproj_kernel(x_ref, f_ref, p_ref, a_ref, sg_ref, mod_ref, w_ref, o_ref, *, d, ctx_row):
    row = ctx_row if ctx_row is not None else pl.program_id(0)
    gate = mod_ref[pl.ds(row, 1), 2 * d:]
    sg = sg_ref[...].astype(F32)
    yf = (f_ref[...] * sg[:, :MIX_Q]).astype(BF16)
    yp = (p_ref[...] * sg[:, MIX_Q:2 * MIX_Q]).astype(BF16)
    ya = (a_ref[...].astype(F32) * sg[:, 2 * MIX_Q:]).astype(BF16)
    acc = jnp.dot(yf, w_ref[0:MIX_Q, :], preferred_element_type=F32)
    acc += jnp.dot(yp, w_ref[MIX_Q:2 * MIX_Q, :], preferred_element_type=F32)
    acc += jnp.dot(ya, w_ref[2 * MIX_Q:, :], preferred_element_type=F32)
    o_ref[...] = x_ref[...] + gate * acc


def _out_projection(x, four, pool, attn, sg, mod, w_out, *, layer, ctx_row, tl):
    b, l, d = x.shape
    row_blk = lambda width: pl.BlockSpec((None, tl, width), lambda bi, i: (bi, i, 0))
    of_layer = lambda arr: pl.BlockSpec((None,) + arr.shape[1:], lambda bi, i: (layer,) + (0,) * (arr.ndim - 1))
    return pl.pallas_call(
        functools.partial(_outproj_kernel, d=d, ctx_row=ctx_row),
        grid=(b, l // tl),
        in_specs=[row_blk(d), row_blk(MIX_Q), row_blk(MIX_Q), row_blk(ATTN_W), row_blk(d), of_layer(mod),
                  of_layer(w_out)],
        out_specs=row_blk(d),
        out_shape=jax.ShapeDtypeStruct((b, l, d), F32),
        compiler_params=_cparams("parallel", "parallel"),
        name="out_projection_ctx" if ctx_row is not None else "out_projection",
    )(x, four, pool, attn, sg, mod, w_out)


def kernel(x, c, ctx, c_ctx, norm_g, w_mod, b_mod, w_in, w_fourier, w_pool, pool_scale, qk_norm_g, lam_vecs,
           subln_g, w_out):
    b, l, d = x.shape
    depth = w_in.shape[0]
    ctx_len = ctx.shape[1]
    ctx_row = b

    cc = jnp.concatenate([c, c_ctx[None, :], jnp.zeros((8 - b - 1, d), F32)], axis=0)
    mod = _modulation(cc, w_mod, b_mod)
    w_eff = _fold_weights(w_in, w_fourier, w_pool, pool_scale)
    w_out_bf = _cast_bf16(w_out)
    rope_tabs = _rope_tables(l // GRID_W)
    lane_chunk = lax.broadcasted_iota(jnp.int32, (2 * LANES, 2 * LANES), 0) // QK_DIM
    ones_blk = (lane_chunk == lane_chunk.T).astype(BF16)
    n1 = 64
    n2 = l // n1

    for layer in range(depth):
        lam_init = 0.8 - 0.6 * math.exp(-0.3 * layer)
        update_ctx = layer < depth - 1
        ng = norm_g[layer][None, :]
        qk_gain = jnp.tile(qk_norm_g[layer], (1, 2 * LANES // QK_DIM))
        sub_g = subln_g[layer][:, None]
        lv = lam_vecs[layer]

        uv, bp, qt, k, vt, sg = _in_projection(x, mod, ng, w_eff, qk_gain, ones_blk, rope_tabs,
                                               layer=layer, ctx_row=None, tl=512)
        uv_c, bp_c, qt_c, k_c, vt_c, sg_c = _in_projection(ctx, mod, ng, w_eff, qk_gain, ones_blk, None,
                                                           layer=layer, ctx_row=ctx_row, tl=ctx_len)
        gains = jnp.max(jnp.abs(qk_norm_g[layer]), axis=-1)
        score_bound = QK_DIM * gains[0] * gains[1] * (Q_SCALE * BF16_ROUNDING_SLACK)
        attn = _attention_dispatch(score_bound, qt, [(k_c, vt_c), (k, vt)], lv, sub_g, lam_init, tq=ATTN_TQ,
                                  tk=ATTN_TK)
        four = _fourier_latent(uv, n1, n2)
        pool = _pool(bp)
        x_new = _out_projection(x, four, pool, attn, sg, mod, w_out_bf, layer=layer, ctx_row=None, tl=512)
        if update_ctx:
            attn_c = _attention_dispatch(score_bound, qt_c, [(k_c, vt_c)], lv, sub_g, lam_init, tq=ctx_len, tk=512)
            ctx = _out_projection(ctx, _fourier_dense(uv_c), _pool(bp_c), attn_c, sg_c, mod, w_out_bf,
                                  layer=layer, ctx_row=ctx_row, tl=ctx_len)
        x = x_new
    return x
```

```python
import functools
import math

import numpy as np
import jax
import jax.numpy as jnp
from jax import lax
from jax.experimental import pallas as pl
from jax.experimental.pallas import tpu as pltpu

GRID_W = 64
FOURIER_GROUPS = 4
GROUP_W = 64
MIX_Q = 256
POOL_WINDOWS = (2, 4, 8, 16)
QK_DIM = 64
HEAD_W = 2 * QK_DIM
N_HEADS = 4
ATTN_W = N_HEADS * HEAD_W
ROPE_THETA = 10000.0
EPS = 1e-6
Q_SCALE = QK_DIM ** -0.5 * math.log2(math.e)
FOURIER_UNROLL = 8
ATTN_TK = 512
ATTN_TQ = 512
MAX_UNSHIFTED_SCORE = 40.0
BF16_ROUNDING_SLACK = 1.02
U_OFF, V_OFF, P_OFF, Q_OFF, K_OFF, VV_OFF, G_OFF, W_EFF = 0, 256, 512, 768, 1280, 1792, 2304, 3328

LANES = 128
VMEM_LIMIT = 56 * 1024 * 1024
HI = lax.Precision.HIGHEST
F32 = jnp.float32
BF16 = jnp.bfloat16


def _cparams(*sem):
    return pltpu.CompilerParams(dimension_semantics=sem, vmem_limit_bytes=VMEM_LIMIT)


def _channel_dft():
    c = np.arange(GROUP_W)
    ang = 2.0 * np.pi * np.outer(c, c) / GROUP_W
    s = 1.0 / math.sqrt(GROUP_W)
    return np.cos(ang) * s, np.sin(ang) * s


def _stage_tables(n1, n2):
    n = n1 * n2
    k1 = np.arange(n1)
    i1 = np.arange(n1)
    j = np.arange(n2)
    theta = 2.0 * np.pi * (j[:, None, None] * k1[None, :, None] / n + k1[None, :, None] * i1[None, None, :] / n1)
    m1 = np.concatenate([np.cos(theta), -np.sin(theta)], axis=1) / math.sqrt(n)
    ang2 = 2.0 * np.pi * np.outer(j, j) / n2
    return m1.astype(np.float32), np.cos(ang2).astype(np.float32), np.sin(ang2).astype(np.float32)


def _rope_tables(rows):
    row = jnp.repeat(jnp.arange(rows), GRID_W).astype(F32)
    col = jnp.tile(jnp.arange(GRID_W), rows).astype(F32)
    half = QK_DIM // 2
    inv_freq = ROPE_THETA ** (-jnp.arange(0, half, 2, dtype=F32) / half)
    ang_r = row[:, None] * inv_freq[None, :]
    ang_c = col[:, None] * inv_freq[None, :]
    ang = jnp.concatenate([ang_r, ang_r, ang_c, ang_c], axis=-1)
    cos, sin = jnp.cos(ang), jnp.sin(ang)
    quarter = QK_DIM // 4
    first = (jnp.arange(QK_DIM) % half) < quarter
    sin_up = jnp.where(first, -sin, 0.0)
    sin_dn = jnp.where(first, 0.0, sin)
    rep = LANES // QK_DIM
    return tuple(jnp.tile(t, (1, rep)) for t in (cos, sin_up, sin_dn))


def _fold_kernel(w_ref, wf_ref, wp_ref, ps_ref, cd_ref, sd_ref, o_ref):
    cd, sd = cd_ref[...], sd_ref[...]
    for g in range(FOURIER_GROUPS):
        sl = slice(g * GROUP_W, (g + 1) * GROUP_W)
        wa = w_ref[:, sl]
        wf = wf_ref[g]
        cw = jnp.dot(cd, wf, precision=HI, preferred_element_type=F32)
        sw = jnp.dot(sd, wf, precision=HI, preferred_element_type=F32)
        o_ref[:, U_OFF + g * GROUP_W:U_OFF + (g + 1) * GROUP_W] = jnp.dot(
            wa, cw, precision=HI, preferred_element_type=F32).astype(o_ref.dtype)
        o_ref[:, V_OFF + g * GROUP_W:V_OFF + (g + 1) * GROUP_W] = jnp.dot(
            wa, sw, precision=HI, preferred_element_type=F32).astype(o_ref.dtype)
        wb = w_ref[:, MIX_Q + g * GROUP_W:MIX_Q + (g + 1) * GROUP_W]
        pw = jnp.dot(wb, wp_ref[g], precision=HI, preferred_element_type=F32) * ps_ref[:, sl]
        o_ref[:, P_OFF + g * GROUP_W:P_OFF + (g + 1) * GROUP_W] = pw.astype(o_ref.dtype)
    o_ref[:, Q_OFF:] = w_ref[:, 2 * MIX_Q:].astype(o_ref.dtype)


def _fold_weights(w_in, w_fourier, w_pool, pool_scale):
    depth, d, in_w = w_in.shape
    tr = 256
    cd, sd = _channel_dft()
    return pl.pallas_call(
        _fold_kernel,
        grid=(depth, d // tr),
        in_specs=[
            pl.BlockSpec((None, tr, in_w), lambda l, i: (l, i, 0)),
            pl.BlockSpec((None, FOURIER_GROUPS, GROUP_W, GROUP_W), lambda l, i: (l, 0, 0, 0)),
            pl.BlockSpec((None, FOURIER_GROUPS, GROUP_W, GROUP_W), lambda l, i: (l, 0, 0, 0)),
            pl.BlockSpec((None, 1, MIX_Q), lambda l, i: (l, 0, 0)),
            pl.BlockSpec((GROUP_W, GROUP_W), lambda l, i: (0, 0)),
            pl.BlockSpec((GROUP_W, GROUP_W), lambda l, i: (0, 0)),
        ],
        out_specs=pl.BlockSpec((None, tr, W_EFF), lambda l, i: (l, i, 0)),
        out_shape=jax.ShapeDtypeStruct((depth, d, W_EFF), BF16),
        compiler_params=_cparams("parallel", "parallel"),
        name="fold_weights",
    )(w_in, w_fourier, w_pool, pool_scale[:, None, :], jnp.asarray(cd, F32), jnp.asarray(sd, F32))


def _cast_kernel(w_ref, o_ref):
    o_ref[...] = w_ref[...].astype(o_ref.dtype)


def _cast_bf16(w):
    depth, r, c = w.shape
    return pl.pallas_call(
        _cast_kernel,
        grid=(depth,),
        in_specs=[pl.BlockSpec((None, r, c), lambda l: (l, 0, 0))],
        out_specs=pl.BlockSpec((None, r, c), lambda l: (l, 0, 0)),
        out_shape=jax.ShapeDtypeStruct(w.shape, BF16),
        compiler_params=_cparams("parallel"),
        name="cast_w_out",
    )(w)


def _mod_kernel(c_ref, w_ref, b_ref, o_ref):
    cc = c_ref[...]
    sc = cc * jax.nn.sigmoid(cc)
    o_ref[...] = jnp.dot(sc, w_ref[...], precision=HI, preferred_element_type=F32) + b_ref[...]


def _modulation(cc, w_mod, b_mod):
    depth, d, n = w_mod.shape
    tn = 1024
    rows = cc.shape[0]
    return pl.pallas_call(
        _mod_kernel,
        grid=(depth, n // tn),
        in_specs=[
            pl.BlockSpec((rows, d), lambda l, j: (0, 0)),
            pl.BlockSpec((None, d, tn), lambda l, j: (l, 0, j)),
            pl.BlockSpec((None, 1, tn), lambda l, j: (l, 0, j)),
        ],
        out_specs=pl.BlockSpec((None, rows, tn), lambda l, j: (l, 0, j)),
        out_shape=jax.ShapeDtypeStruct((depth, rows, n), F32),
        compiler_params=_cparams("parallel", "parallel"),
        name="modulation",
    )(cc, w_mod, b_mod[:, None, :])


def _qk_normalise(t, ones_blk, gain):
    ss = jnp.dot((t * t).astype(BF16), ones_blk, preferred_element_type=F32)
    return t * lax.rsqrt(ss * (1.0 / QK_DIM) + EPS) * gain


def _rotate(t, cos, sin_up, sin_dn):
    quarter = QK_DIM // 4
    up = pltpu.roll(t, LANES - quarter, 1)
    dn = pltpu.roll(t, quarter, 1)
    return t * cos + up * sin_up + dn * sin_dn


def _residual_update(x_ref, f_ref, p_ref, a_ref, sg_ref, mod_ref, w_ref, row, d):
    gate = mod_ref[pl.ds(row, 1), 2 * d:]
    sg = sg_ref[...].astype(F32)
    yf = (f_ref[...] * sg[:, :MIX_Q]).astype(BF16)
    yp = (p_ref[...] * sg[:, MIX_Q:2 * MIX_Q]).astype(BF16)
    ya = (a_ref[...].astype(F32) * sg[:, 2 * MIX_Q:]).astype(BF16)
    acc = jnp.dot(yf, w_ref[0:MIX_Q, :], preferred_element_type=F32)
    acc += jnp.dot(yp, w_ref[MIX_Q:2 * MIX_Q, :], preferred_element_type=F32)
    acc += jnp.dot(ya, w_ref[2 * MIX_Q:, :], preferred_element_type=F32)
    return x_ref[...] + gate * acc


def _inproj_kernel(x_ref, *rest, d, ctx_row, rope, fused):
    row = ctx_row if ctx_row is not None else pl.program_id(0)
    if fused:
        prev, rest = rest[:6], rest[6:]
    mod_ref, ng_ref, w_ref, qkg_ref, ones_ref = rest[:5]
    rest = rest[5:]
    if rope:
        cos_ref, sup_ref, sdn_ref = rest[:3]
        rest = rest[3:]
    if fused:
        xnew_ref, rest = rest[0], rest[1:]
        xf = _residual_update(x_ref, *prev, row, d)
        xnew_ref[...] = xf
    else:
        xf = x_ref[...]
    uv_ref, bp_ref, qt_ref, k_ref, vt_ref, sg_ref = rest
    mod = mod_ref[pl.ds(row, 1), :]
    shift, scale = mod[:, :d], mod[:, d:2 * d]
    ms = jnp.mean(xf * xf, axis=-1, keepdims=True)
    h = (xf * lax.rsqrt(ms + EPS) * (ng_ref[...] * (1.0 + scale)) + shift).astype(BF16)

    def proj(lo, hi):
        return jnp.dot(h, w_ref[:, lo:hi], preferred_element_type=F32)

    pair = 2 * LANES
    n_pairs = ATTN_W // pair
    ones_blk = ones_ref[...]
    q_raw = [proj(Q_OFF + p * pair, Q_OFF + (p + 1) * pair) for p in range(n_pairs)]
    k_raw = [proj(K_OFF + p * pair, K_OFF + (p + 1) * pair) for p in range(n_pairs)]
    q_nrm = [_qk_normalise(t, ones_blk, qkg_ref[0:1, :]) for t in q_raw]
    k_nrm = [_qk_normalise(t, ones_blk, qkg_ref[1:2, :]) for t in k_raw]
    v_raw = [proj(VV_OFF + p * pair, VV_OFF + (p + 1) * pair) for p in range(n_pairs)]
    g = proj(G_OFF, W_EFF)
    sg_ref[...] = (g * jax.nn.sigmoid(g)).astype(sg_ref.dtype)
    uv_ref[...] = proj(U_OFF, P_OFF)
    bp_ref[...] = proj(P_OFF, Q_OFF)

    if rope:
        cos, sup, sdn = cos_ref[...], sup_ref[...], sdn_ref[...]
    for p in range(n_pairs):
        qp, kp, vp = q_nrm[p], k_nrm[p], v_raw[p]
        for half in range(2):
            sl = slice(p * pair + half * LANES, p * pair + (half + 1) * LANES)
            hs = slice(half * LANES, (half + 1) * LANES)
            qt, kt = qp[:, hs], kp[:, hs]
            if rope:
                qt = _rotate(qt, cos, sup, sdn)
                kt = _rotate(kt, cos, sup, sdn)
            qt_ref[sl, :] = (qt * Q_SCALE).T.astype(qt_ref.dtype)
            k_ref[:, sl] = kt.astype(k_ref.dtype)
            vt_ref[sl, :] = vp[:, hs].T.astype(vt_ref.dtype)


def _in_projection(x, mod, norm_g, w_eff, qk_gain, ones_blk, rope_tabs, *, layer, ctx_row, tl, prev=None):
    b, l, d = x.shape
    rope = rope_tabs is not None
    fused = prev is not None
    row_blk = lambda width: pl.BlockSpec((None, tl, width), lambda bi, i: (bi, i, 0))
    full = lambda arr: pl.BlockSpec(arr.shape, lambda bi, i: (0,) * arr.ndim)

    def of_layer(arr, which=layer):
        return pl.BlockSpec((None,) + arr.shape[1:], lambda bi, i: (which,) + (0,) * (arr.ndim - 1))

    in_specs, args = [row_blk(d)], [x]
    if fused:
        four, pool, attn, sg_prev, w_out = prev
        in_specs += [row_blk(MIX_Q), row_blk(MIX_Q), row_blk(ATTN_W), row_blk(d), of_layer(mod, layer - 1),
                     of_layer(w_out, layer - 1)]
        args += [four, pool, attn, sg_prev, mod, w_out]
    in_specs += [of_layer(mod), full(norm_g), of_layer(w_eff), full(qk_gain), full(ones_blk)]
    args += [mod, norm_g, w_eff, qk_gain, ones_blk]
    if rope:
        in_specs += [pl.BlockSpec((tl, LANES), lambda bi, i: (i, 0))] * 3
        args += list(rope_tabs)
    out_shape = (
        jax.ShapeDtypeStruct((b, l, 2 * MIX_Q), F32),
        jax.ShapeDtypeStruct((b, l, MIX_Q), F32),
        jax.ShapeDtypeStruct((b, ATTN_W, l), BF16),
        jax.ShapeDtypeStruct((b, l, ATTN_W), BF16),
        jax.ShapeDtypeStruct((b, ATTN_W, l), BF16),
        jax.ShapeDtypeStruct((b, l, d), BF16),
    )
    col_blk = pl.BlockSpec((None, ATTN_W, tl), lambda bi, i: (bi, 0, i))
    out_specs = (row_blk(2 * MIX_Q), row_blk(MIX_Q), col_blk, row_blk(ATTN_W), col_blk, row_blk(d))
    if fused:
        out_shape = (jax.ShapeDtypeStruct((b, l, d), F32),) + out_shape
        out_specs = (row_blk(d),) + out_specs
    name = "in_projection_ctx" if ctx_row is not None else "in_projection"
    return pl.pallas_call(
        functools.partial(_inproj_kernel, d=d, ctx_row=ctx_row, rope=rope, fused=fused),
        grid=(b, l // tl),
        in_specs=in_specs,
        out_specs=out_specs,
        out_shape=out_shape,
        compiler_params=_cparams("parallel", "parallel"),
        name="residual_" + name if fused else name,
    )(*args)


def _attn_kernel(*refs, n_seg, seg_len, tk, lam_init):
    qt_ref = refs[0]
    kv_refs = refs[1:1 + 2 * n_seg]
    lam_ref, sub_ref, o_ref, m_ref, l_ref, acc_ref = refs[1 + 2 * n_seg:]
    qt = qt_ref[...]
    first_map = lax.broadcasted_iota(jnp.int32, qt.shape, 0) < QK_DIM
    zero = jnp.zeros_like(qt)
    ws = (jnp.where(first_map, qt, zero), jnp.where(first_map, zero, qt))

    m_ref[...] = jnp.full(m_ref.shape, -jnp.inf, F32)
    l_ref[...] = jnp.zeros(l_ref.shape, F32)
    acc_ref[...] = jnp.zeros(acc_ref.shape, F32)

    def chunk(k_ref, vt_ref, start, size):
        k = k_ref[pl.ds(start, size), :]
        vt = vt_ref[:, pl.ds(start, size)]
        for mp in range(2):
            s = jnp.dot(k, ws[mp], preferred_element_type=F32)
            m_old = m_ref[mp]
            m_new = jnp.maximum(m_old, jnp.max(s, axis=0, keepdims=True))
            alpha = jnp.exp2(m_old - m_new)
            e = jnp.exp2(s - m_new)
            l_ref[mp] = alpha * l_ref[mp] + jnp.sum(e, axis=0, keepdims=True)
            acc_ref[mp] = alpha * acc_ref[mp] + jnp.dot(vt, e.astype(BF16), preferred_element_type=F32)
            m_ref[mp] = m_new

    for si in range(n_seg):
        k_ref, vt_ref = kv_refs[2 * si], kv_refs[2 * si + 1]
        length = seg_len[si]
        size = min(tk, length)
        n_chunks = length // size
        if n_chunks == 1:
            chunk(k_ref, vt_ref, 0, size)
        else:
            def body(j, carry, k_ref=k_ref, vt_ref=vt_ref, size=size):
                chunk(k_ref, vt_ref, pl.multiple_of(j * size, size), size)
                return carry
            lax.fori_loop(0, n_chunks, body, 0)

    lv = lam_ref[...]
    lam = (jnp.exp(jnp.sum(lv[0:1] * lv[1:2], axis=-1, keepdims=True))
           - jnp.exp(jnp.sum(lv[2:3] * lv[3:4], axis=-1, keepdims=True)) + lam_init)
    o = acc_ref[0] / l_ref[0] - lam * (acc_ref[1] / l_ref[1])
    ms = jnp.mean(o * o, axis=0, keepdims=True)
    o = o * lax.rsqrt(ms + EPS) * (sub_ref[...] * (1.0 - lam_init))
    o_ref[...] = o.T.astype(o_ref.dtype)


def _attn_unshifted_kernel(*refs, n_seg, seg_len, tk, lam_init):
    qt_ref = refs[0]
    kv_refs = refs[1:1 + 2 * n_seg]
    lam_ref, sub_ref, o_ref, l_ref, acc_ref, sa_ref, sb_ref = refs[1 + 2 * n_seg:]
    qt = qt_ref[...]
    tq = qt.shape[1]
    first_map = lax.broadcasted_iota(jnp.int32, qt.shape, 0) < QK_DIM
    zero = jnp.zeros_like(qt)
    w = jnp.concatenate([jnp.where(first_map, qt, zero), jnp.where(first_map, zero, qt)], axis=1)

    l_ref[...] = jnp.zeros(l_ref.shape, F32)
    acc_ref[...] = jnp.zeros(acc_ref.shape, F32)

    def scores(s_ref, k_ref, start, size):
        s_ref[0:size, :] = jnp.dot(k_ref[pl.ds(start, size), :], w, preferred_element_type=F32)

    def consume(s_ref, vt_ref, start, size):
        e = jnp.exp2(s_ref[0:size, :])
        l_ref[...] += jnp.sum(e.reshape(size // 8, 8, 2 * tq), axis=0)
        acc_ref[...] += jnp.dot(vt_ref[:, pl.ds(start, size)], e.astype(BF16), preferred_element_type=F32)

    chunks = []
    for i in range(n_seg):
        size = min(tk, seg_len[i])
        assert seg_len[i] % size == 0
        chunks += [(kv_refs[2 * i], kv_refs[2 * i + 1], c * size, size) for c in range(seg_len[i] // size)]
    bufs = (sa_ref, sb_ref)
    scores(bufs[0], chunks[0][0], chunks[0][2], chunks[0][3])
    for i, (_, vt_ref, start, size) in enumerate(chunks):
        if i + 1 < len(chunks):
            nxt = chunks[i + 1]
            scores(bufs[(i + 1) % 2], nxt[0], nxt[2], nxt[3])
        consume(bufs[i % 2], vt_ref, start, size)

    lv = lam_ref[...]
    lam = (jnp.exp(jnp.sum(lv[0:1] * lv[1:2], axis=-1, keepdims=True))
           - jnp.exp(jnp.sum(lv[2:3] * lv[3:4], axis=-1, keepdims=True)) + lam_init)
    lsum = jnp.sum(l_ref[...], axis=0, keepdims=True)
    acc = acc_ref[...]
    o = acc[:, :tq] / lsum[:, :tq] - lam * (acc[:, tq:] / lsum[:, tq:])
    ms = jnp.mean(o * o, axis=0, keepdims=True)
    o = o * lax.rsqrt(ms + EPS) * (sub_ref[...] * (1.0 - lam_init))
    o_ref[...] = o.T.astype(o_ref.dtype)


def _attention(qt, segs, lam_vecs, subln_col, lam_init, *, tq, tk, shifted):
    b, _, lq = qt.shape
    in_specs = [pl.BlockSpec((None, HEAD_W, tq), lambda bi, h, i: (bi, h, i))]
    args = [qt]
    seg_len = []
    for k, vt in segs:
        lk = k.shape[1]
        seg_len.append(lk)
        in_specs.append(pl.BlockSpec((None, lk, HEAD_W), lambda bi, h, i: (bi, 0, h)))
        in_specs.append(pl.BlockSpec((None, HEAD_W, lk), lambda bi, h, i: (bi, h, 0)))
        args += [k, vt]
    in_specs += [pl.BlockSpec(lam_vecs.shape, lambda bi, h, i: (0, 0)),
                 pl.BlockSpec(subln_col.shape, lambda bi, h, i: (0, 0))]
    args += [lam_vecs, subln_col]
    if shifted:
        body = _attn_kernel
        scratch = [pltpu.VMEM((2, 1, tq), F32), pltpu.VMEM((2, 1, tq), F32), pltpu.VMEM((2, HEAD_W, tq), F32)]
    else:
        body = _attn_unshifted_kernel
        s_rows = min(tk, max(seg_len))
        scratch = [pltpu.VMEM((8, 2 * tq), F32), pltpu.VMEM((HEAD_W, 2 * tq), F32),
                   pltpu.VMEM((s_rows, 2 * tq), F32), pltpu.VMEM((s_rows, 2 * tq), F32)]
    return pl.pallas_call(
        functools.partial(body, n_seg=len(segs), seg_len=tuple(seg_len), tk=tk, lam_init=lam_init),
        grid=(b, N_HEADS, lq // tq),
        in_specs=in_specs,
        out_specs=pl.BlockSpec((None, tq, HEAD_W), lambda bi, h, i: (bi, i, h)),
        out_shape=jax.ShapeDtypeStruct((b, lq, ATTN_W), BF16),
        scratch_shapes=scratch,
        compiler_params=_cparams("parallel", "parallel", "arbitrary"),
        name="diff_attention_shifted" if shifted else "diff_attention",
    )(*args)


def _attention_dispatch(score_bound, *args, **kw):
    return lax.cond(score_bound <= MAX_UNSHIFTED_SCORE,
                    lambda: _attention(*args, shifted=False, **kw),
                    lambda: _attention(*args, shifted=True, **kw))


def _fourier_kernel(u_ref, v_ref, m1_ref, c2_ref, s2_ref, o_ref, yr_ref, yi_ref, *, n1, n2):
    def stage1(j, carry):
        xu = u_ref[pl.ds(j, n1, stride=n2), :]
        xv = v_ref[pl.ds(j, n1, stride=n2), :]
        x = jnp.concatenate([xu, xv], axis=1).astype(BF16)
        p = jnp.dot(m1_ref[j], x, preferred_element_type=F32)
        rows = pl.ds(pl.multiple_of(j * n1, n1), n1)
        yr_ref[rows, :] = p[:n1, :LANES] + p[n1:, LANES:]
        yi_ref[rows, :] = p[n1:, :LANES] - p[:n1, LANES:]
        return carry

    lax.fori_loop(0, n2, stage1, 0, unroll=FOURIER_UNROLL)
    c2, s2 = c2_ref[...], s2_ref[...]

    def stage2(k1, carry):
        rows = pl.ds(k1, n2, stride=n1)
        xr = (jnp.dot(c2, yr_ref[rows, :].astype(BF16), preferred_element_type=F32)
              + jnp.dot(s2, yi_ref[rows, :].astype(BF16), preferred_element_type=F32))
        o_ref[rows, :] = xr
        return carry

    lax.fori_loop(0, n1, stage2, 0, unroll=FOURIER_UNROLL)


def _fourier_latent(uv, n1, n2):
    b, l, _ = uv.shape
    m1, c2, s2 = _stage_tables(n1, n2)
    m1, c2, s2 = (jnp.asarray(t).astype(BF16) for t in (m1, c2, s2))
    halves = MIX_Q // LANES
    return pl.pallas_call(
        functools.partial(_fourier_kernel, n1=n1, n2=n2),
        grid=(b, halves),
        in_specs=[
            pl.BlockSpec((None, l, LANES), lambda bi, c: (bi, 0, c)),
            pl.BlockSpec((None, l, LANES), lambda bi, c: (bi, 0, c + halves)),
            pl.BlockSpec(m1.shape, lambda bi, c: (0, 0, 0)),
            pl.BlockSpec(c2.shape, lambda bi, c: (0, 0)),
            pl.BlockSpec(s2.shape, lambda bi, c: (0, 0)),
        ],
        out_specs=pl.BlockSpec((None, l, LANES), lambda bi, c: (bi, 0, c)),
        out_shape=jax.ShapeDtypeStruct((b, l, MIX_Q), F32),
        scratch_shapes=[pltpu.VMEM((l, LANES), F32), pltpu.VMEM((l, LANES), F32)],
        compiler_params=_cparams("parallel", "parallel"),
        name="fourier_positions",
    )(uv, uv, m1, c2, s2)


def _fourier_dense_kernel(uv_ref, c_ref, s_ref, o_ref):
    uv = uv_ref[...].astype(BF16)
    o_ref[...] = (jnp.dot(c_ref[...], uv[:, :MIX_Q], preferred_element_type=F32)
                  - jnp.dot(s_ref[...], uv[:, MIX_Q:], preferred_element_type=F32))


def _fourier_dense(uv):
    b, l, _ = uv.shape
    n = np.arange(l)
    ang = 2.0 * np.pi * np.outer(n, n) / l
    c = jnp.asarray((np.cos(ang) / math.sqrt(l)).astype(np.float32)).astype(BF16)
    s = jnp.asarray((np.sin(ang) / math.sqrt(l)).astype(np.float32)).astype(BF16)
    return pl.pallas_call(
        _fourier_dense_kernel,
        grid=(b,),
        in_specs=[pl.BlockSpec((None, l, 2 * MIX_Q), lambda bi: (bi, 0, 0)),
                  pl.BlockSpec((l, l), lambda bi: (0, 0)),
                  pl.BlockSpec((l, l), lambda bi: (0, 0))],
        out_specs=pl.BlockSpec((None, l, MIX_Q), lambda bi: (bi, 0, 0)),
        out_shape=jax.ShapeDtypeStruct((b, l, MIX_Q), F32),
        compiler_params=_cparams("parallel"),
        name="fourier_positions_ctx",
    )(uv, c, s)


POOL_HALO = 8


def _pool_kernel(b_ref, o_ref, pad_ref, *, l, rows):
    zeros = jnp.zeros((POOL_HALO, LANES), F32)
    lane = lax.broadcasted_iota(jnp.int32, (rows, LANES), 1)
    low_group = lane < GROUP_W
    n_chunks = l // rows
    for tile in range(MIX_Q // LANES):
        w_lo, w_hi = POOL_WINDOWS[2 * tile], POOL_WINDOWS[2 * tile + 1]
        pad_ref[0:POOL_HALO, :] = zeros
        pad_ref[POOL_HALO + l:, :] = zeros
        pad_ref[POOL_HALO:POOL_HALO + l, :] = b_ref[:, tile * LANES:(tile + 1) * LANES]

        def chunk(t0, edge, tile=tile, w_lo=w_lo, w_hi=w_hi):
            shifted = lambda dlt: pad_ref[pl.ds(t0 + POOL_HALO + dlt, rows), :]
            lo_range = range(-(w_lo // 2), w_lo // 2)
            inner = functools.reduce(lambda a, b: a + b, [shifted(d) for d in lo_range])
            outer = functools.reduce(lambda a, b: a + b,
                                     [shifted(d) for d in range(-(w_hi // 2), w_hi // 2) if d not in lo_range], inner)
            sums = jnp.where(low_group, inner, outer)
            if edge:
                pos = t0 + lax.broadcasted_iota(jnp.int32, (rows, LANES), 0)
                half = jnp.where(low_group, w_lo // 2, w_hi // 2)
                cnt = jnp.minimum(pos + (half - 1), l - 1) - jnp.maximum(pos - half, 0) + 1
                mean = sums / cnt.astype(F32)
            else:
                mean = sums * jnp.where(low_group, 1.0 / w_lo, 1.0 / w_hi)
            o_ref[pl.ds(t0, rows), tile * LANES:(tile + 1) * LANES] = mean - shifted(0)

        assert rows >= POOL_HALO
        chunk(0, True)
        if n_chunks > 1:
            def body(ci, carry, chunk=chunk):
                chunk(pl.multiple_of(ci * rows, rows), False)
                return carry

            lax.fori_loop(1, n_chunks - 1, body, 0)
            chunk((n_chunks - 1) * rows, True)


def _pool(bp):
    b, l, _ = bp.shape
    rows = 256
    return pl.pallas_call(
        functools.partial(_pool_kernel, l=l, rows=rows),
        grid=(b,),
        in_specs=[pl.BlockSpec((None, l, MIX_Q), lambda bi: (bi, 0, 0))],
        out_specs=pl.BlockSpec((None, l, MIX_Q), lambda bi: (bi, 0, 0)),
        out_shape=jax.ShapeDtypeStruct((b, l, MIX_Q), F32),
        scratch_shapes=[pltpu.VMEM((l + 2 * POOL_HALO, LANES), F32)],
        compiler_params=_cparams("parallel"),
        name="pool_windows",
    )(bp)


def _outproj_kernel(x_ref, f_ref, p_ref, a_ref, sg_ref, mod_ref, w_ref, o_ref, *, d, ctx_row):
    row = ctx_row if ctx_row is not None else pl.program_id(0)
    o_ref[...] = _residual_update(x_ref, f_ref, p_ref, a_ref, sg_ref, mod_ref, w_ref, row, d)


def _out_projection(x, four, pool, attn, sg, mod, w_out, *, layer, ctx_row, tl):
    b, l, d = x.shape
    row_blk = lambda width: pl.BlockSpec((None, tl, width), lambda bi, i: (bi, i, 0))
    of_layer = lambda arr: pl.BlockSpec((None,) + arr.shape[1:], lambda bi, i: (layer,) + (0,) * (arr.ndim - 1))
    return pl.pallas_call(
        functools.partial(_outproj_kernel, d=d, ctx_row=ctx_row),
        grid=(b, l // tl),
        in_specs=[row_blk(d), row_blk(MIX_Q), row_blk(MIX_Q), row_blk(ATTN_W), row_blk(d), of_layer(mod),
                  of_layer(w_out)],
        out_specs=row_blk(d),
        out_shape=jax.ShapeDtypeStruct((b, l, d), F32),
        compiler_params=_cparams("parallel", "parallel"),
        name="out_projection_ctx" if ctx_row is not None else "out_projection",
    )(x, four, pool, attn, sg, mod, w_out)


def kernel(x, c, ctx, c_ctx, norm_g, w_mod, b_mod, w_in, w_fourier, w_pool, pool_scale, qk_norm_g, lam_vecs,
           subln_g, w_out):
    b, l, d = x.shape
    depth = w_in.shape[0]
    ctx_len = ctx.shape[1]
    ctx_row = b

    cc = jnp.concatenate([c, c_ctx[None, :], jnp.zeros((8 - b - 1, d), F32)], axis=0)
    mod = _modulation(cc, w_mod, b_mod)
    w_eff = _fold_weights(w_in, w_fourier, w_pool, pool_scale)
    w_out_bf = _cast_bf16(w_out)
    rope_tabs = _rope_tables(l // GRID_W)
    lane_chunk = lax.broadcasted_iota(jnp.int32, (2 * LANES, 2 * LANES), 0) // QK_DIM
    ones_blk = (lane_chunk == lane_chunk.T).astype(BF16)
    n1 = 64
    n2 = l // n1

    pending = None
    for layer in range(depth):
        lam_init = 0.8 - 0.6 * math.exp(-0.3 * layer)
        update_ctx = layer < depth - 1
        ng = norm_g[layer][None, :]
        qk_gain = jnp.tile(qk_norm_g[layer], (1, 2 * LANES // QK_DIM))
        sub_g = subln_g[layer][:, None]
        lv = lam_vecs[layer]

        outs = _in_projection(x, mod, ng, w_eff, qk_gain, ones_blk, rope_tabs, layer=layer, ctx_row=None,
                              tl=512, prev=pending)
        if pending is not None:
            x, outs = outs[0], outs[1:]
        uv, bp, qt, k, vt, sg = outs
        uv_c, bp_c, qt_c, k_c, vt_c, sg_c = _in_projection(ctx, mod, ng, w_eff, qk_gain, ones_blk, None,
                                                           layer=layer, ctx_row=ctx_row, tl=ctx_len)
        gains = jnp.max(jnp.abs(qk_norm_g[layer]), axis=-1)
        score_bound = QK_DIM * gains[0] * gains[1] * (Q_SCALE * BF16_ROUNDING_SLACK)
        attn = _attention_dispatch(score_bound, qt, [(k_c, vt_c), (k, vt)], lv, sub_g, lam_init, tq=ATTN_TQ,
                                  tk=ATTN_TK)
        four = _fourier_latent(uv, n1, n2)
        pool = _pool(bp)
        if update_ctx:
            attn_c = _attention_dispatch(score_bound, qt_c, [(k_c, vt_c)], lv, sub_g, lam_init, tq=ctx_len, tk=512)
            ctx = _out_projection(ctx, _fourier_dense(uv_c), _pool(bp_c), attn_c, sg_c, mod, w_out_bf,
                                  layer=layer, ctx_row=ctx_row, tl=ctx_len)
        if layer + 1 < depth:
            pending = (four, pool, attn, sg, w_out_bf)
        else:
            x = _out_projection(x, four, pool, attn, sg, mod, w_out_bf, layer=layer, ctx_row=None, tl=1024)
    return x
```

```python
import functools
import math

import numpy as np
import jax
import jax.numpy as jnp
from jax import lax
from jax.experimental import pallas as pl
from jax.experimental.pallas import tpu as pltpu

GRID_W = 64
FOURIER_GROUPS = 4
GROUP_W = 64
MIX_Q = 256
POOL_WINDOWS = (2, 4, 8, 16)
QK_DIM = 64
HEAD_W = 2 * QK_DIM
N_HEADS = 4
ATTN_W = N_HEADS * HEAD_W
ROPE_THETA = 10000.0
EPS = 1e-6
Q_SCALE = QK_DIM ** -0.5 * math.log2(math.e)
FOURIER_UNROLL = 8
ATTN_TK = 512
ATTN_TQ = 1024
MAX_UNSHIFTED_SCORE = 40.0
BF16_ROUNDING_SLACK = 1.02
U_OFF, V_OFF, P_OFF, Q_OFF, K_OFF, VV_OFF, G_OFF, W_EFF = 0, 256, 512, 768, 1280, 1792, 2304, 3328

LANES = 128
VMEM_LIMIT = 56 * 1024 * 1024
HI = lax.Precision.HIGHEST
F32 = jnp.float32
BF16 = jnp.bfloat16


def _cparams(*sem):
    return pltpu.CompilerParams(dimension_semantics=sem, vmem_limit_bytes=VMEM_LIMIT)


def _channel_dft():
    c = np.arange(GROUP_W)
    ang = 2.0 * np.pi * np.outer(c, c) / GROUP_W
    s = 1.0 / math.sqrt(GROUP_W)
    return np.cos(ang) * s, np.sin(ang) * s


def _stage_tables(n1, n2):
    n = n1 * n2
    k1 = np.arange(n1)
    i1 = np.arange(n1)
    j = np.arange(n2)
    theta = 2.0 * np.pi * (j[:, None, None] * k1[None, :, None] / n + k1[None, :, None] * i1[None, None, :] / n1)
    m1 = np.concatenate([np.cos(theta), -np.sin(theta)], axis=1) / math.sqrt(n)
    ang2 = 2.0 * np.pi * np.outer(j, j) / n2
    return m1.astype(np.float32), np.cos(ang2).astype(np.float32), np.sin(ang2).astype(np.float32)


def _rope_tables(rows):
    row = jnp.repeat(jnp.arange(rows), GRID_W).astype(F32)
    col = jnp.tile(jnp.arange(GRID_W), rows).astype(F32)
    half = QK_DIM // 2
    inv_freq = ROPE_THETA ** (-jnp.arange(0, half, 2, dtype=F32) / half)
    ang_r = row[:, None] * inv_freq[None, :]
    ang_c = col[:, None] * inv_freq[None, :]
    ang = jnp.concatenate([ang_r, ang_r, ang_c, ang_c], axis=-1)
    cos, sin = jnp.cos(ang), jnp.sin(ang)
    quarter = QK_DIM // 4
    first = (jnp.arange(QK_DIM) % half) < quarter
    sin_up = jnp.where(first, -sin, 0.0)
    sin_dn = jnp.where(first, 0.0, sin)
    rep = LANES // QK_DIM
    return tuple(jnp.tile(t, (1, rep)) for t in (cos, sin_up, sin_dn))


def _fold_kernel(w_ref, wf_ref, wp_ref, ps_ref, cd_ref, sd_ref, o_ref):
    cd, sd = cd_ref[...], sd_ref[...]
    for g in range(FOURIER_GROUPS):
        sl = slice(g * GROUP_W, (g + 1) * GROUP_W)
        wa = w_ref[:, sl]
        wf = wf_ref[g]
        cw = jnp.dot(cd, wf, precision=HI, preferred_element_type=F32)
        sw = jnp.dot(sd, wf, precision=HI, preferred_element_type=F32)
        o_ref[:, U_OFF + g * GROUP_W:U_OFF + (g + 1) * GROUP_W] = jnp.dot(
            wa, cw, precision=HI, preferred_element_type=F32).astype(o_ref.dtype)
        o_ref[:, V_OFF + g * GROUP_W:V_OFF + (g + 1) * GROUP_W] = jnp.dot(
            wa, sw, precision=HI, preferred_element_type=F32).astype(o_ref.dtype)
        wb = w_ref[:, MIX_Q + g * GROUP_W:MIX_Q + (g + 1) * GROUP_W]
        pw = jnp.dot(wb, wp_ref[g], precision=HI, preferred_element_type=F32) * ps_ref[:, sl]
        o_ref[:, P_OFF + g * GROUP_W:P_OFF + (g + 1) * GROUP_W] = pw.astype(o_ref.dtype)
    o_ref[:, Q_OFF:] = w_ref[:, 2 * MIX_Q:].astype(o_ref.dtype)


def _fold_weights(w_in, w_fourier, w_pool, pool_scale):
    depth, d, in_w = w_in.shape
    tr = 256
    cd, sd = _channel_dft()
    return pl.pallas_call(
        _fold_kernel,
        grid=(depth, d // tr),
        in_specs=[
            pl.BlockSpec((None, tr, in_w), lambda l, i: (l, i, 0)),
            pl.BlockSpec((None, FOURIER_GROUPS, GROUP_W, GROUP_W), lambda l, i: (l, 0, 0, 0)),
            pl.BlockSpec((None, FOURIER_GROUPS, GROUP_W, GROUP_W), lambda l, i: (l, 0, 0, 0)),
            pl.BlockSpec((None, 1, MIX_Q), lambda l, i: (l, 0, 0)),
            pl.BlockSpec((GROUP_W, GROUP_W), lambda l, i: (0, 0)),
            pl.BlockSpec((GROUP_W, GROUP_W), lambda l, i: (0, 0)),
        ],
        out_specs=pl.BlockSpec((None, tr, W_EFF), lambda l, i: (l, i, 0)),
        out_shape=jax.ShapeDtypeStruct((depth, d, W_EFF), BF16),
        compiler_params=_cparams("parallel", "parallel"),
        name="fold_weights",
    )(w_in, w_fourier, w_pool, pool_scale[:, None, :], jnp.asarray(cd, F32), jnp.asarray(sd, F32))


def _cast_kernel(w_ref, o_ref):
    o_ref[...] = w_ref[...].astype(o_ref.dtype)


def _cast_bf16(w):
    depth, r, c = w.shape
    return pl.pallas_call(
        _cast_kernel,
        grid=(depth,),
        in_specs=[pl.BlockSpec((None, r, c), lambda l: (l, 0, 0))],
        out_specs=pl.BlockSpec((None, r, c), lambda l: (l, 0, 0)),
        out_shape=jax.ShapeDtypeStruct(w.shape, BF16),
        compiler_params=_cparams("parallel"),
        name="cast_w_out",
    )(w)


def _mod_kernel(c_ref, w_ref, b_ref, o_ref):
    cc = c_ref[...]
    sc = cc * jax.nn.sigmoid(cc)
    o_ref[...] = jnp.dot(sc, w_ref[...], precision=HI, preferred_element_type=F32) + b_ref[...]


def _modulation(cc, w_mod, b_mod):
    depth, d, n = w_mod.shape
    tn = 1024
    rows = cc.shape[0]
    return pl.pallas_call(
        _mod_kernel,
        grid=(depth, n // tn),
        in_specs=[
            pl.BlockSpec((rows, d), lambda l, j: (0, 0)),
            pl.BlockSpec((None, d, tn), lambda l, j: (l, 0, j)),
            pl.BlockSpec((None, 1, tn), lambda l, j: (l, 0, j)),
        ],
        out_specs=pl.BlockSpec((None, rows, tn), lambda l, j: (l, 0, j)),
        out_shape=jax.ShapeDtypeStruct((depth, rows, n), F32),
        compiler_params=_cparams("parallel", "parallel"),
        name="modulation",
    )(cc, w_mod, b_mod[:, None, :])


def _qk_normalise(t, ones_blk, gain):
    ss = jnp.dot((t * t).astype(BF16), ones_blk, preferred_element_type=F32)
    return t * lax.rsqrt(ss * (1.0 / QK_DIM) + EPS) * gain


def _rotate(t, cos, sin_up, sin_dn):
    quarter = QK_DIM // 4
    up = pltpu.roll(t, LANES - quarter, 1)
    dn = pltpu.roll(t, quarter, 1)
    return t * cos + up * sin_up + dn * sin_dn


def _residual_update(x_ref, f_ref, p_ref, a_ref, sg_ref, mod_ref, w_ref, row, d):
    gate = mod_ref[pl.ds(row, 1), 2 * d:]
    sg = sg_ref[...].astype(F32)
    yf = (f_ref[...] * sg[:, :MIX_Q]).astype(BF16)
    yp = (p_ref[...] * sg[:, MIX_Q:2 * MIX_Q]).astype(BF16)
    ya = (a_ref[...].astype(F32) * sg[:, 2 * MIX_Q:]).astype(BF16)
    acc = jnp.dot(yf, w_ref[0:MIX_Q, :], preferred_element_type=F32)
    acc += jnp.dot(yp, w_ref[MIX_Q:2 * MIX_Q, :], preferred_element_type=F32)
    acc += jnp.dot(ya, w_ref[2 * MIX_Q:, :], preferred_element_type=F32)
    return x_ref[...] + gate * acc


def _inproj_kernel(x_ref, *rest, d, ctx_row, rope, fused):
    row = ctx_row if ctx_row is not None else pl.program_id(0)
    if fused:
        prev, rest = rest[:6], rest[6:]
    mod_ref, ng_ref, w_ref, qkg_ref, ones_ref = rest[:5]
    rest = rest[5:]
    if rope:
        cos_ref, sup_ref, sdn_ref = rest[:3]
        rest = rest[3:]
    if fused:
        xnew_ref, rest = rest[0], rest[1:]
        xf = _residual_update(x_ref, *prev, row, d)
        xnew_ref[...] = xf
    else:
        xf = x_ref[...]
    uv_ref, bp_ref, qt_ref, k_ref, vt_ref, sg_ref = rest
    mod = mod_ref[pl.ds(row, 1), :]
    shift, scale = mod[:, :d], mod[:, d:2 * d]
    ms = jnp.mean(xf * xf, axis=-1, keepdims=True)
    h = (xf * lax.rsqrt(ms + EPS) * (ng_ref[...] * (1.0 + scale)) + shift).astype(BF16)

    def proj(lo, hi):
        return jnp.dot(h, w_ref[:, lo:hi], preferred_element_type=F32)

    pair = 2 * LANES
    n_pairs = ATTN_W // pair
    ones_blk = ones_ref[...]
    q_raw = [proj(Q_OFF + p * pair, Q_OFF + (p + 1) * pair) for p in range(n_pairs)]
    k_raw = [proj(K_OFF + p * pair, K_OFF + (p + 1) * pair) for p in range(n_pairs)]
    q_nrm = [_qk_normalise(t, ones_blk, qkg_ref[0:1, :]) for t in q_raw]
    k_nrm = [_qk_normalise(t, ones_blk, qkg_ref[1:2, :]) for t in k_raw]
    v_raw = [proj(VV_OFF + p * pair, VV_OFF + (p + 1) * pair) for p in range(n_pairs)]
    g = proj(G_OFF, W_EFF)
    sg_ref[...] = (g * jax.nn.sigmoid(g)).astype(sg_ref.dtype)
    uv_ref[...] = proj(U_OFF, P_OFF)
    bp_ref[...] = proj(P_OFF, Q_OFF)

    if rope:
        cos, sup, sdn = cos_ref[...], sup_ref[...], sdn_ref[...]
    for p in range(n_pairs):
        qp, kp, vp = q_nrm[p], k_nrm[p], v_raw[p]
        for half in range(2):
            sl = slice(p * pair + half * LANES, p * pair + (half + 1) * LANES)
            hs = slice(half * LANES, (half + 1) * LANES)
            qt, kt = qp[:, hs], kp[:, hs]
            if rope:
                qt = _rotate(qt, cos, sup, sdn)
                kt = _rotate(kt, cos, sup, sdn)
            qt_ref[sl, :] = (qt * Q_SCALE).T.astype(qt_ref.dtype)
            k_ref[:, sl] = kt.astype(k_ref.dtype)
            vt_ref[sl, :] = vp[:, hs].T.astype(vt_ref.dtype)


def _in_projection(x, mod, norm_g, w_eff, qk_gain, ones_blk, rope_tabs, *, layer, ctx_row, tl, prev=None):
    b, l, d = x.shape
    rope = rope_tabs is not None
    fused = prev is not None
    row_blk = lambda width: pl.BlockSpec((None, tl, width), lambda bi, i: (bi, i, 0))
    full = lambda arr: pl.BlockSpec(arr.shape, lambda bi, i: (0,) * arr.ndim)

    def of_layer(arr, which=layer):
        return pl.BlockSpec((None,) + arr.shape[1:], lambda bi, i: (which,) + (0,) * (arr.ndim - 1))

    in_specs, args = [row_blk(d)], [x]
    if fused:
        four, pool, attn, sg_prev, w_out = prev
        in_specs += [row_blk(MIX_Q), row_blk(MIX_Q), row_blk(ATTN_W), row_blk(d), of_layer(mod, layer - 1),
                     of_layer(w_out, layer - 1)]
        args += [four, pool, attn, sg_prev, mod, w_out]
    in_specs += [of_layer(mod), full(norm_g), of_layer(w_eff), full(qk_gain), full(ones_blk)]
    args += [mod, norm_g, w_eff, qk_gain, ones_blk]
    if rope:
        in_specs += [pl.BlockSpec((tl, LANES), lambda bi, i: (i, 0))] * 3
        args += list(rope_tabs)
    out_shape = (
        jax.ShapeDtypeStruct((b, l, 2 * MIX_Q), F32),
        jax.ShapeDtypeStruct((b, l, MIX_Q), F32),
        jax.ShapeDtypeStruct((b, ATTN_W, l), BF16),
        jax.ShapeDtypeStruct((b, l, ATTN_W), BF16),
        jax.ShapeDtypeStruct((b, ATTN_W, l), BF16),
        jax.ShapeDtypeStruct((b, l, d), BF16),
    )
    col_blk = pl.BlockSpec((None, ATTN_W, tl), lambda bi, i: (bi, 0, i))
    out_specs = (row_blk(2 * MIX_Q), row_blk(MIX_Q), col_blk, row_blk(ATTN_W), col_blk, row_blk(d))
    if fused:
        out_shape = (jax.ShapeDtypeStruct((b, l, d), F32),) + out_shape
        out_specs = (row_blk(d),) + out_specs
    name = "in_projection_ctx" if ctx_row is not None else "in_projection"
    return pl.pallas_call(
        functools.partial(_inproj_kernel, d=d, ctx_row=ctx_row, rope=rope, fused=fused),
        grid=(b, l // tl),
        in_specs=in_specs,
        out_specs=out_specs,
        out_shape=out_shape,
        compiler_params=_cparams("parallel", "parallel"),
        name="residual_" + name if fused else name,
    )(*args)


def _attn_kernel(*refs, n_seg, seg_len, tk, lam_init):
    qt_ref = refs[0]
    kv_refs = refs[1:1 + 2 * n_seg]
    lam_ref, sub_ref, o_ref, m_ref, l_ref, acc_ref = refs[1 + 2 * n_seg:]
    qt = qt_ref[...]
    first_map = lax.broadcasted_iota(jnp.int32, qt.shape, 0) < QK_DIM
    zero = jnp.zeros_like(qt)
    ws = (jnp.where(first_map, qt, zero), jnp.where(first_map, zero, qt))

    m_ref[...] = jnp.full(m_ref.shape, -jnp.inf, F32)
    l_ref[...] = jnp.zeros(l_ref.shape, F32)
    acc_ref[...] = jnp.zeros(acc_ref.shape, F32)

    def chunk(k_ref, vt_ref, start, size):
        k = k_ref[pl.ds(start, size), :]
        vt = vt_ref[:, pl.ds(start, size)]
        for mp in range(2):
            s = jnp.dot(k, ws[mp], preferred_element_type=F32)
            m_old = m_ref[mp]
            m_new = jnp.maximum(m_old, jnp.max(s, axis=0, keepdims=True))
            alpha = jnp.exp2(m_old - m_new)
            e = jnp.exp2(s - m_new)
            l_ref[mp] = alpha * l_ref[mp] + jnp.sum(e, axis=0, keepdims=True)
            acc_ref[mp] = alpha * acc_ref[mp] + jnp.dot(vt, e.astype(BF16), preferred_element_type=F32)
            m_ref[mp] = m_new

    for si in range(n_seg):
        k_ref, vt_ref = kv_refs[2 * si], kv_refs[2 * si + 1]
        length = seg_len[si]
        size = min(tk, length)
        n_chunks = length // size
        if n_chunks == 1:
            chunk(k_ref, vt_ref, 0, size)
        else:
            def body(j, carry, k_ref=k_ref, vt_ref=vt_ref, size=size):
                chunk(k_ref, vt_ref, pl.multiple_of(j * size, size), size)
                return carry
            lax.fori_loop(0, n_chunks, body, 0)

    lv = lam_ref[...]
    lam = (jnp.exp(jnp.sum(lv[0:1] * lv[1:2], axis=-1, keepdims=True))
           - jnp.exp(jnp.sum(lv[2:3] * lv[3:4], axis=-1, keepdims=True)) + lam_init)
    o = acc_ref[0] / l_ref[0] - lam * (acc_ref[1] / l_ref[1])
    ms = jnp.mean(o * o, axis=0, keepdims=True)
    o = o * lax.rsqrt(ms + EPS) * (sub_ref[...] * (1.0 - lam_init))
    o_ref[...] = o.T.astype(o_ref.dtype)


def _attn_unshifted_kernel(*refs, n_seg, seg_len, tk, lam_init):
    qt_ref = refs[0]
    kv_refs = refs[1:1 + 2 * n_seg]
    lam_ref, sub_ref, o_ref, l_ref, acc_ref, sa_ref, sb_ref = refs[1 + 2 * n_seg:]
    qt = qt_ref[...]
    tq = qt.shape[1]
    first_map = lax.broadcasted_iota(jnp.int32, qt.shape, 0) < QK_DIM
    zero = jnp.zeros_like(qt)
    w = jnp.concatenate([jnp.where(first_map, qt, zero), jnp.where(first_map, zero, qt)], axis=1)

    l_ref[...] = jnp.zeros(l_ref.shape, F32)
    acc_ref[...] = jnp.zeros(acc_ref.shape, F32)

    def scores(s_ref, k_ref, start, size):
        s_ref[0:size, :] = jnp.dot(k_ref[pl.ds(start, size), :], w, preferred_element_type=F32)

    def consume(s_ref, vt_ref, start, size):
        e = jnp.exp2(s_ref[0:size, :])
        l_ref[...] += jnp.sum(e.reshape(size // 8, 8, 2 * tq), axis=0)
        acc_ref[...] += jnp.dot(vt_ref[:, pl.ds(start, size)], e.astype(BF16), preferred_element_type=F32)

    chunks = []
    for i in range(n_seg):
        size = min(tk, seg_len[i])
        assert seg_len[i] % size == 0
        chunks += [(kv_refs[2 * i], kv_refs[2 * i + 1], c * size, size) for c in range(seg_len[i] // size)]
    bufs = (sa_ref, sb_ref)
    scores(bufs[0], chunks[0][0], chunks[0][2], chunks[0][3])
    for i, (_, vt_ref, start, size) in enumerate(chunks):
        if i + 1 < len(chunks):
            nxt = chunks[i + 1]
            scores(bufs[(i + 1) % 2], nxt[0], nxt[2], nxt[3])
        consume(bufs[i % 2], vt_ref, start, size)

    lv = lam_ref[...]
    lam = (jnp.exp(jnp.sum(lv[0:1] * lv[1:2], axis=-1, keepdims=True))
           - jnp.exp(jnp.sum(lv[2:3] * lv[3:4], axis=-1, keepdims=True)) + lam_init)
    lsum = jnp.sum(l_ref[...], axis=0, keepdims=True)
    acc = acc_ref[...]
    o = acc[:, :tq] / lsum[:, :tq] - lam * (acc[:, tq:] / lsum[:, tq:])
    ms = jnp.mean(o * o, axis=0, keepdims=True)
    o = o * lax.rsqrt(ms + EPS) * (sub_ref[...] * (1.0 - lam_init))
    o_ref[...] = o.T.astype(o_ref.dtype)


def _attention(qt, segs, lam_vecs, subln_col, lam_init, *, tq, tk, shifted):
    b, _, lq = qt.shape
    in_specs = [pl.BlockSpec((None, HEAD_W, tq), lambda bi, h, i: (bi, h, i))]
    args = [qt]
    seg_len = []
    for k, vt in segs:
        lk = k.shape[1]
        seg_len.append(lk)
        in_specs.append(pl.BlockSpec((None, lk, HEAD_W), lambda bi, h, i: (bi, 0, h)))
        in_specs.append(pl.BlockSpec((None, HEAD_W, lk), lambda bi, h, i: (bi, h, 0)))
        args += [k, vt]
    in_specs += [pl.BlockSpec(lam_vecs.shape, lambda bi, h, i: (0, 0)),
                 pl.BlockSpec(subln_col.shape, lambda bi, h, i: (0, 0))]
    args += [lam_vecs, subln_col]
    if shifted:
        body = _attn_kernel
        scratch = [pltpu.VMEM((2, 1, tq), F32), pltpu.VMEM((2, 1, tq), F32), pltpu.VMEM((2, HEAD_W, tq), F32)]
    else:
        body = _attn_unshifted_kernel
        s_rows = min(tk, max(seg_len))
        scratch = [pltpu.VMEM((8, 2 * tq), F32), pltpu.VMEM((HEAD_W, 2 * tq), F32),
                   pltpu.VMEM((s_rows, 2 * tq), F32), pltpu.VMEM((s_rows, 2 * tq), F32)]
    return pl.pallas_call(
        functools.partial(body, n_seg=len(segs), seg_len=tuple(seg_len), tk=tk, lam_init=lam_init),
        grid=(b, N_HEADS, lq // tq),
        in_specs=in_specs,
        out_specs=pl.BlockSpec((None, tq, HEAD_W), lambda bi, h, i: (bi, i, h)),
        out_shape=jax.ShapeDtypeStruct((b, lq, ATTN_W), BF16),
        scratch_shapes=scratch,
        compiler_params=_cparams("parallel", "parallel", "arbitrary"),
        name="diff_attention_shifted" if shifted else "diff_attention",
    )(*args)


def _attention_dispatch(score_bound, *args, **kw):
    return lax.cond(score_bound <= MAX_UNSHIFTED_SCORE,
                    lambda: _attention(*args, shifted=False, **kw),
                    lambda: _attention(*args, shifted=True, **kw))


def _fourier_kernel(u_ref, v_ref, m1_ref, c2_ref, s2_ref, o_ref, yr_ref, yi_ref, *, n1, n2):
    def stage1(j, carry):
        xu = u_ref[pl.ds(j, n1, stride=n2), :]
        xv = v_ref[pl.ds(j, n1, stride=n2), :]
        x = jnp.concatenate([xu, xv], axis=1).astype(BF16)
        p = jnp.dot(m1_ref[j], x, preferred_element_type=F32)
        rows = pl.ds(pl.multiple_of(j * n1, n1), n1)
        yr_ref[rows, :] = p[:n1, :LANES] + p[n1:, LANES:]
        yi_ref[rows, :] = p[n1:, :LANES] - p[:n1, LANES:]
        return carry

    lax.fori_loop(0, n2, stage1, 0, unroll=FOURIER_UNROLL)
    c2, s2 = c2_ref[...], s2_ref[...]

    def stage2(k1, carry):
        rows = pl.ds(k1, n2, stride=n1)
        xr = (jnp.dot(c2, yr_ref[rows, :].astype(BF16), preferred_element_type=F32)
              + jnp.dot(s2, yi_ref[rows, :].astype(BF16), preferred_element_type=F32))
        o_ref[rows, :] = xr
        return carry

    lax.fori_loop(0, n1, stage2, 0, unroll=FOURIER_UNROLL)


def _fourier_latent(uv, n1, n2):
    b, l, _ = uv.shape
    m1, c2, s2 = _stage_tables(n1, n2)
    m1, c2, s2 = (jnp.asarray(t).astype(BF16) for t in (m1, c2, s2))
    halves = MIX_Q // LANES
    return pl.pallas_call(
        functools.partial(_fourier_kernel, n1=n1, n2=n2),
        grid=(b, halves),
        in_specs=[
            pl.BlockSpec((None, l, LANES), lambda bi, c: (bi, 0, c)),
            pl.BlockSpec((None, l, LANES), lambda bi, c: (bi, 0, c + halves)),
            pl.BlockSpec(m1.shape, lambda bi, c: (0, 0, 0)),
            pl.BlockSpec(c2.shape, lambda bi, c: (0, 0)),
            pl.BlockSpec(s2.shape, lambda bi, c: (0, 0)),
        ],
        out_specs=pl.BlockSpec((None, l, LANES), lambda bi, c: (bi, 0, c)),
        out_shape=jax.ShapeDtypeStruct((b, l, MIX_Q), F32),
        scratch_shapes=[pltpu.VMEM((l, LANES), F32), pltpu.VMEM((l, LANES), F32)],
        compiler_params=_cparams("parallel", "parallel"),
        name="fourier_positions",
    )(uv, uv, m1, c2, s2)


def _fourier_dense_kernel(uv_ref, c_ref, s_ref, o_ref):
    uv = uv_ref[...].astype(BF16)
    o_ref[...] = (jnp.dot(c_ref[...], uv[:, :MIX_Q], preferred_element_type=F32)
                  - jnp.dot(s_ref[...], uv[:, MIX_Q:], preferred_element_type=F32))


def _fourier_dense(uv):
    b, l, _ = uv.shape
    n = np.arange(l)
    ang = 2.0 * np.pi * np.outer(n, n) / l
    c = jnp.asarray((np.cos(ang) / math.sqrt(l)).astype(np.float32)).astype(BF16)
    s = jnp.asarray((np.sin(ang) / math.sqrt(l)).astype(np.float32)).astype(BF16)
    return pl.pallas_call(
        _fourier_dense_kernel,
        grid=(b,),
        in_specs=[pl.BlockSpec((None, l, 2 * MIX_Q), lambda bi: (bi, 0, 0)),
                  pl.BlockSpec((l, l), lambda bi: (0, 0)),
                  pl.BlockSpec((l, l), lambda bi: (0, 0))],
        out_specs=pl.BlockSpec((None, l, MIX_Q), lambda bi: (bi, 0, 0)),
        out_shape=jax.ShapeDtypeStruct((b, l, MIX_Q), F32),
        compiler_params=_cparams("parallel"),
        name="fourier_positions_ctx",
    )(uv, c, s)


POOL_HALO = 8


def _pool_kernel(b_ref, o_ref, pad_ref, *, l, rows):
    zeros = jnp.zeros((POOL_HALO, LANES), F32)
    lane = lax.broadcasted_iota(jnp.int32, (rows, LANES), 1)
    low_group = lane < GROUP_W
    n_chunks = l // rows
    for tile in range(MIX_Q // LANES):
        w_lo, w_hi = POOL_WINDOWS[2 * tile], POOL_WINDOWS[2 * tile + 1]
        pad_ref[0:POOL_HALO, :] = zeros
        pad_ref[POOL_HALO + l:, :] = zeros
        pad_ref[POOL_HALO:POOL_HALO + l, :] = b_ref[:, tile * LANES:(tile + 1) * LANES]

        def chunk(t0, edge, tile=tile, w_lo=w_lo, w_hi=w_hi):
            shifted = lambda dlt: pad_ref[pl.ds(t0 + POOL_HALO + dlt, rows), :]
            lo_range = range(-(w_lo // 2), w_lo // 2)
            inner = functools.reduce(lambda a, b: a + b, [shifted(d) for d in lo_range])
            outer = functools.reduce(lambda a, b: a + b,
                                     [shifted(d) for d in range(-(w_hi // 2), w_hi // 2) if d not in lo_range], inner)
            sums = jnp.where(low_group, inner, outer)
            if edge:
                pos = t0 + lax.broadcasted_iota(jnp.int32, (rows, LANES), 0)
                half = jnp.where(low_group, w_lo // 2, w_hi // 2)
                cnt = jnp.minimum(pos + (half - 1), l - 1) - jnp.maximum(pos - half, 0) + 1
                mean = sums / cnt.astype(F32)
            else:
                mean = sums * jnp.where(low_group, 1.0 / w_lo, 1.0 / w_hi)
            o_ref[pl.ds(t0, rows), tile * LANES:(tile + 1) * LANES] = mean - shifted(0)

        assert rows >= POOL_HALO
        chunk(0, True)
        if n_chunks > 1:
            def body(ci, carry, chunk=chunk):
                chunk(pl.multiple_of(ci * rows, rows), False)
                return carry

            lax.fori_loop(1, n_chunks - 1, body, 0)
            chunk((n_chunks - 1) * rows, True)


def _pool(bp):
    b, l, _ = bp.shape
    rows = 256
    return pl.pallas_call(
        functools.partial(_pool_kernel, l=l, rows=rows),
        grid=(b,),
        in_specs=[pl.BlockSpec((None, l, MIX_Q), lambda bi: (bi, 0, 0))],
        out_specs=pl.BlockSpec((None, l, MIX_Q), lambda bi: (bi, 0, 0)),
        out_shape=jax.ShapeDtypeStruct((b, l, MIX_Q), F32),
        scratch_shapes=[pltpu.VMEM((l + 2 * POOL_HALO, LANES), F32)],
        compiler_params=_cparams("parallel"),
        name="pool_windows",
    )(bp)


def _outproj_kernel(x_ref, f_ref, p_ref, a_ref, sg_ref, mod_ref, w_ref, o_ref, *, d, ctx_row):
    row = ctx_row if ctx_row is not None else pl.program_id(0)
    o_ref[...] = _residual_update(x_ref, f_ref, p_ref, a_ref, sg_ref, mod_ref, w_ref, row, d)


def _out_projection(x, four, pool, attn, sg, mod, w_out, *, layer, ctx_row, tl):
    b, l, d = x.shape
    row_blk = lambda width: pl.BlockSpec((None, tl, width), lambda bi, i: (bi, i, 0))
    of_layer = lambda arr: pl.BlockSpec((None,) + arr.shape[1:], lambda bi, i: (layer,) + (0,) * (arr.ndim - 1))
    return pl.pallas_call(
        functools.partial(_outproj_kernel, d=d, ctx_row=ctx_row),
        grid=(b, l // tl),
        in_specs=[row_blk(d), row_blk(MIX_Q), row_blk(MIX_Q), row_blk(ATTN_W), row_blk(d), of_layer(mod),
                  of_layer(w_out)],
        out_specs=row_blk(d),
        out_shape=jax.ShapeDtypeStruct((b, l, d), F32),
        compiler_params=_cparams("parallel", "parallel"),
        name="out_projection_ctx" if ctx_row is not None else "out_projection",
    )(x, four, pool, attn, sg, mod, w_out)


def kernel(x, c, ctx, c_ctx, norm_g, w_mod, b_mod, w_in, w_fourier, w_pool, pool_scale, qk_norm_g, lam_vecs,
           subln_g, w_out):
    b, l, d = x.shape
    depth = w_in.shape[0]
    ctx_len = ctx.shape[1]
    ctx_row = b

    cc = jnp.concatenate([c, c_ctx[None, :], jnp.zeros((8 - b - 1, d), F32)], axis=0)
    mod = _modulation(cc, w_mod, b_mod)
    w_eff = _fold_weights(w_in, w_fourier, w_pool, pool_scale)
    w_out_bf = _cast_bf16(w_out)
    rope_tabs = _rope_tables(l // GRID_W)
    lane_chunk = lax.broadcasted_iota(jnp.int32, (2 * LANES, 2 * LANES), 0) // QK_DIM
    ones_blk = (lane_chunk == lane_chunk.T).astype(BF16)
    n1 = 64
    n2 = l // n1

    pending = None
    for layer in range(depth):
        lam_init = 0.8 - 0.6 * math.exp(-0.3 * layer)
        update_ctx = layer < depth - 1
        ng = norm_g[layer][None, :]
        qk_gain = jnp.tile(qk_norm_g[layer], (1, 2 * LANES // QK_DIM))
        sub_g = subln_g[layer][:, None]
        lv = lam_vecs[layer]

        outs = _in_projection(x, mod, ng, w_eff, qk_gain, ones_blk, rope_tabs, layer=layer, ctx_row=None,
                              tl=512, prev=pending)
        if pending is not None:
            x, outs = outs[0], outs[1:]
        uv, bp, qt, k, vt, sg = outs
        uv_c, bp_c, qt_c, k_c, vt_c, sg_c = _in_projection(ctx, mod, ng, w_eff, qk_gain, ones_blk, None,
                                                           layer=layer, ctx_row=ctx_row, tl=ctx_len)
        gains = jnp.max(jnp.abs(qk_norm_g[layer]), axis=-1)
        score_bound = QK_DIM * gains[0] * gains[1] * (Q_SCALE * BF16_ROUNDING_SLACK)
        attn = _attention_dispatch(score_bound, qt, [(k_c, vt_c), (k, vt)], lv, sub_g, lam_init, tq=ATTN_TQ,
                                  tk=ATTN_TK)
        four = _fourier_latent(uv, n1, n2)
        pool = _pool(bp)
        if update_ctx:
            attn_c = _attention_dispatch(score_bound, qt_c, [(k_c, vt_c)], lv, sub_g, lam_init, tq=ctx_len, tk=512)
            ctx = _out_projection(ctx, _fourier_dense(uv_c), _pool(bp_c), attn_c, sg_c, mod, w_out_bf,
                                  layer=layer, ctx_row=ctx_row, tl=ctx_len)
        if layer + 1 < depth:
            pending = (four, pool, attn, sg, w_out_bf)
        else:
            x = _out_projection(x, four, pool, attn, sg, mod, w_out_bf, layer=layer, ctx_row=None, tl=1024)
    return x
```

```python
import functools
import math

import numpy as np
import jax
import jax.numpy as jnp
from jax import lax
from jax.experimental import pallas as pl
from jax.experimental.pallas import tpu as pltpu

GRID_W = 64
FOURIER_GROUPS = 4
GROUP_W = 64
MIX_Q = 256
POOL_WINDOWS = (2, 4, 8, 16)
QK_DIM = 64
HEAD_W = 2 * QK_DIM
N_HEADS = 4
ATTN_W = N_HEADS * HEAD_W
ROPE_THETA = 10000.0
EPS = 1e-6
Q_SCALE = QK_DIM ** -0.5 * math.log2(math.e)
FOURIER_UNROLL = 8
ATTN_TK = 512
ATTN_TQ = 1024
SHIFTED_TQ = 512
MAX_UNSHIFTED_SCORE = 40.0
BF16_ROUNDING_SLACK = 1.02
U_OFF, V_OFF, P_OFF, Q_OFF, K_OFF, VV_OFF, G_OFF, W_EFF = 0, 256, 512, 768, 1280, 1792, 2304, 3328

LANES = 128
VMEM_LIMIT = 56 * 1024 * 1024
HI = lax.Precision.HIGHEST
F32 = jnp.float32
BF16 = jnp.bfloat16


def _cparams(*sem):
    return pltpu.CompilerParams(dimension_semantics=sem, vmem_limit_bytes=VMEM_LIMIT)


def _channel_dft():
    c = np.arange(GROUP_W)
    ang = 2.0 * np.pi * np.outer(c, c) / GROUP_W
    s = 1.0 / math.sqrt(GROUP_W)
    return np.cos(ang) * s, np.sin(ang) * s


def _stage_tables(n1, n2):
    n = n1 * n2
    k1 = np.arange(n1)
    i1 = np.arange(n1)
    j = np.arange(n2)
    theta = 2.0 * np.pi * (j[:, None, None] * k1[None, :, None] / n + k1[None, :, None] * i1[None, None, :] / n1)
    m1 = np.concatenate([np.cos(theta), -np.sin(theta)], axis=1) / math.sqrt(n)
    ang2 = 2.0 * np.pi * np.outer(j, j) / n2
    return m1.astype(np.float32), np.cos(ang2).astype(np.float32), np.sin(ang2).astype(np.float32)


def _rope_tables(rows):
    row = jnp.repeat(jnp.arange(rows), GRID_W).astype(F32)
    col = jnp.tile(jnp.arange(GRID_W), rows).astype(F32)
    half = QK_DIM // 2
    inv_freq = ROPE_THETA ** (-jnp.arange(0, half, 2, dtype=F32) / half)
    ang_r = row[:, None] * inv_freq[None, :]
    ang_c = col[:, None] * inv_freq[None, :]
    ang = jnp.concatenate([ang_r, ang_r, ang_c, ang_c], axis=-1)
    cos, sin = jnp.cos(ang), jnp.sin(ang)
    quarter = QK_DIM // 4
    first = (jnp.arange(QK_DIM) % half) < quarter
    sin_up = jnp.where(first, -sin, 0.0)
    sin_dn = jnp.where(first, 0.0, sin)
    rep = LANES // QK_DIM
    return tuple(jnp.tile(t, (1, rep)) for t in (cos, sin_up, sin_dn))


def _fold_kernel(w_ref, wf_ref, wp_ref, ps_ref, cd_ref, sd_ref, o_ref):
    cd, sd = cd_ref[...], sd_ref[...]
    for g in range(FOURIER_GROUPS):
        sl = slice(g * GROUP_W, (g + 1) * GROUP_W)
        wa = w_ref[:, sl]
        wf = wf_ref[g]
        cw = jnp.dot(cd, wf, precision=HI, preferred_element_type=F32)
        sw = jnp.dot(sd, wf, precision=HI, preferred_element_type=F32)
        o_ref[:, U_OFF + g * GROUP_W:U_OFF + (g + 1) * GROUP_W] = jnp.dot(
            wa, cw, precision=HI, preferred_element_type=F32).astype(o_ref.dtype)
        o_ref[:, V_OFF + g * GROUP_W:V_OFF + (g + 1) * GROUP_W] = jnp.dot(
            wa, sw, precision=HI, preferred_element_type=F32).astype(o_ref.dtype)
        wb = w_ref[:, MIX_Q + g * GROUP_W:MIX_Q + (g + 1) * GROUP_W]
        pw = jnp.dot(wb, wp_ref[g], precision=HI, preferred_element_type=F32) * ps_ref[:, sl]
        o_ref[:, P_OFF + g * GROUP_W:P_OFF + (g + 1) * GROUP_W] = pw.astype(o_ref.dtype)
    o_ref[:, Q_OFF:] = w_ref[:, 2 * MIX_Q:].astype(o_ref.dtype)


def _fold_weights(w_in, w_fourier, w_pool, pool_scale):
    depth, d, in_w = w_in.shape
    tr = 256
    cd, sd = _channel_dft()
    return pl.pallas_call(
        _fold_kernel,
        grid=(depth, d // tr),
        in_specs=[
            pl.BlockSpec((None, tr, in_w), lambda l, i: (l, i, 0)),
            pl.BlockSpec((None, FOURIER_GROUPS, GROUP_W, GROUP_W), lambda l, i: (l, 0, 0, 0)),
            pl.BlockSpec((None, FOURIER_GROUPS, GROUP_W, GROUP_W), lambda l, i: (l, 0, 0, 0)),
            pl.BlockSpec((None, 1, MIX_Q), lambda l, i: (l, 0, 0)),
            pl.BlockSpec((GROUP_W, GROUP_W), lambda l, i: (0, 0)),
            pl.BlockSpec((GROUP_W, GROUP_W), lambda l, i: (0, 0)),
        ],
        out_specs=pl.BlockSpec((None, tr, W_EFF), lambda l, i: (l, i, 0)),
        out_shape=jax.ShapeDtypeStruct((depth, d, W_EFF), BF16),
        compiler_params=_cparams("parallel", "parallel"),
        name="fold_weights",
    )(w_in, w_fourier, w_pool, pool_scale[:, None, :], jnp.asarray(cd, F32), jnp.asarray(sd, F32))


def _cast_kernel(w_ref, o_ref):
    o_ref[...] = w_ref[...].astype(o_ref.dtype)


def _cast_bf16(w):
    depth, r, c = w.shape
    return pl.pallas_call(
        _cast_kernel,
        grid=(depth,),
        in_specs=[pl.BlockSpec((None, r, c), lambda l: (l, 0, 0))],
        out_specs=pl.BlockSpec((None, r, c), lambda l: (l, 0, 0)),
        out_shape=jax.ShapeDtypeStruct(w.shape, BF16),
        compiler_params=_cparams("parallel"),
        name="cast_w_out",
    )(w)


def _mod_kernel(c_ref, w_ref, b_ref, o_ref):
    cc = c_ref[...]
    sc = cc * jax.nn.sigmoid(cc)
    o_ref[...] = jnp.dot(sc, w_ref[...], precision=HI, preferred_element_type=F32) + b_ref[...]


def _modulation(cc, w_mod, b_mod):
    depth, d, n = w_mod.shape
    tn = 1024
    rows = cc.shape[0]
    return pl.pallas_call(
        _mod_kernel,
        grid=(depth, n // tn),
        in_specs=[
            pl.BlockSpec((rows, d), lambda l, j: (0, 0)),
            pl.BlockSpec((None, d, tn), lambda l, j: (l, 0, j)),
            pl.BlockSpec((None, 1, tn), lambda l, j: (l, 0, j)),
        ],
        out_specs=pl.BlockSpec((None, rows, tn), lambda l, j: (l, 0, j)),
        out_shape=jax.ShapeDtypeStruct((depth, rows, n), F32),
        compiler_params=_cparams("parallel", "parallel"),
        name="modulation",
    )(cc, w_mod, b_mod[:, None, :])


def _qk_normalise(t, ones_blk, gain):
    ss = jnp.dot((t * t).astype(BF16), ones_blk, preferred_element_type=F32)
    return t * lax.rsqrt(ss * (1.0 / QK_DIM) + EPS) * gain


def _rotate(t, cos, sin_up, sin_dn):
    quarter = QK_DIM // 4
    up = pltpu.roll(t, LANES - quarter, 1)
    dn = pltpu.roll(t, quarter, 1)
    return t * cos + up * sin_up + dn * sin_dn


def _residual_update(x_ref, f_ref, p_ref, a_ref, sg_ref, mod_ref, w_ref, row, d):
    gate = mod_ref[pl.ds(row, 1), 2 * d:]
    sg = sg_ref[...].astype(F32)
    yf = (f_ref[...] * sg[:, :MIX_Q]).astype(BF16)
    yp = (p_ref[...] * sg[:, MIX_Q:2 * MIX_Q]).astype(BF16)
    ya = (a_ref[...].astype(F32) * sg[:, 2 * MIX_Q:]).astype(BF16)
    acc = jnp.dot(yf, w_ref[0:MIX_Q, :], preferred_element_type=F32)
    acc += jnp.dot(yp, w_ref[MIX_Q:2 * MIX_Q, :], preferred_element_type=F32)
    acc += jnp.dot(ya, w_ref[2 * MIX_Q:, :], preferred_element_type=F32)
    return x_ref[...] + gate * acc


def _inproj_kernel(x_ref, *rest, d, ctx_row, rope, fused):
    row = ctx_row if ctx_row is not None else pl.program_id(0)
    if fused:
        prev, rest = rest[:6], rest[6:]
    mod_ref, ng_ref, w_ref, qkg_ref, ones_ref = rest[:5]
    rest = rest[5:]
    if rope:
        cos_ref, sup_ref, sdn_ref = rest[:3]
        rest = rest[3:]
    if fused:
        xnew_ref, rest = rest[0], rest[1:]
        xf = _residual_update(x_ref, *prev, row, d)
        xnew_ref[...] = xf
    else:
        xf = x_ref[...]
    uv_ref, bp_ref, qt_ref, k_ref, vt_ref, sg_ref = rest
    mod = mod_ref[pl.ds(row, 1), :]
    shift, scale = mod[:, :d], mod[:, d:2 * d]
    ms = jnp.mean(xf * xf, axis=-1, keepdims=True)
    h = (xf * lax.rsqrt(ms + EPS) * (ng_ref[...] * (1.0 + scale)) + shift).astype(BF16)

    def proj(lo, hi):
        return jnp.dot(h, w_ref[:, lo:hi], preferred_element_type=F32)

    pair = 2 * LANES
    n_pairs = ATTN_W // pair
    ones_blk = ones_ref[...]
    q_raw = [proj(Q_OFF + p * pair, Q_OFF + (p + 1) * pair) for p in range(n_pairs)]
    k_raw = [proj(K_OFF + p * pair, K_OFF + (p + 1) * pair) for p in range(n_pairs)]
    q_nrm = [_qk_normalise(t, ones_blk, qkg_ref[0:1, :]) for t in q_raw]
    k_nrm = [_qk_normalise(t, ones_blk, qkg_ref[1:2, :]) for t in k_raw]
    v_raw = [proj(VV_OFF + p * pair, VV_OFF + (p + 1) * pair) for p in range(n_pairs)]
    g = proj(G_OFF, W_EFF)
    sg_ref[...] = (g * jax.nn.sigmoid(g)).astype(sg_ref.dtype)
    uv_ref[...] = proj(U_OFF, P_OFF)
    bp_ref[...] = proj(P_OFF, Q_OFF)

    if rope:
        cos, sup, sdn = cos_ref[...], sup_ref[...], sdn_ref[...]
    for p in range(n_pairs):
        qp, kp, vp = q_nrm[p], k_nrm[p], v_raw[p]
        for half in range(2):
            sl = slice(p * pair + half * LANES, p * pair + (half + 1) * LANES)
            hs = slice(half * LANES, (half + 1) * LANES)
            qt, kt = qp[:, hs], kp[:, hs]
            if rope:
                qt = _rotate(qt, cos, sup, sdn)
                kt = _rotate(kt, cos, sup, sdn)
            qt_ref[sl, :] = (qt * Q_SCALE).T.astype(qt_ref.dtype)
            k_ref[:, sl] = kt.astype(k_ref.dtype)
            vt_ref[sl, :] = vp[:, hs].T.astype(vt_ref.dtype)


def _in_projection(x, mod, norm_g, w_eff, qk_gain, ones_blk, rope_tabs, *, layer, ctx_row, tl, prev=None):
    b, l, d = x.shape
    rope = rope_tabs is not None
    fused = prev is not None
    row_blk = lambda width: pl.BlockSpec((None, tl, width), lambda bi, i: (bi, i, 0))
    full = lambda arr: pl.BlockSpec(arr.shape, lambda bi, i: (0,) * arr.ndim)

    def of_layer(arr, which=layer):
        return pl.BlockSpec((None,) + arr.shape[1:], lambda bi, i: (which,) + (0,) * (arr.ndim - 1))

    in_specs, args = [row_blk(d)], [x]
    if fused:
        four, pool, attn, sg_prev, w_out = prev
        in_specs += [row_blk(MIX_Q), row_blk(MIX_Q), row_blk(ATTN_W), row_blk(d), of_layer(mod, layer - 1),
                     of_layer(w_out, layer - 1)]
        args += [four, pool, attn, sg_prev, mod, w_out]
    in_specs += [of_layer(mod), full(norm_g), of_layer(w_eff), full(qk_gain), full(ones_blk)]
    args += [mod, norm_g, w_eff, qk_gain, ones_blk]
    if rope:
        in_specs += [pl.BlockSpec((tl, LANES), lambda bi, i: (i, 0))] * 3
        args += list(rope_tabs)
    out_shape = (
        jax.ShapeDtypeStruct((b, l, 2 * MIX_Q), F32),
        jax.ShapeDtypeStruct((b, l, MIX_Q), F32),
        jax.ShapeDtypeStruct((b, ATTN_W, l), BF16),
        jax.ShapeDtypeStruct((b, l, ATTN_W), BF16),
        jax.ShapeDtypeStruct((b, ATTN_W, l), BF16),
        jax.ShapeDtypeStruct((b, l, d), BF16),
    )
    col_blk = pl.BlockSpec((None, ATTN_W, tl), lambda bi, i: (bi, 0, i))
    out_specs = (row_blk(2 * MIX_Q), row_blk(MIX_Q), col_blk, row_blk(ATTN_W), col_blk, row_blk(d))
    if fused:
        out_shape = (jax.ShapeDtypeStruct((b, l, d), F32),) + out_shape
        out_specs = (row_blk(d),) + out_specs
    name = "in_projection_ctx" if ctx_row is not None else "in_projection"
    return pl.pallas_call(
        functools.partial(_inproj_kernel, d=d, ctx_row=ctx_row, rope=rope, fused=fused),
        grid=(b, l // tl),
        in_specs=in_specs,
        out_specs=out_specs,
        out_shape=out_shape,
        compiler_params=_cparams("parallel", "parallel"),
        name="residual_" + name if fused else name,
    )(*args)


def _attn_kernel(*refs, n_seg, seg_len, tk, lam_init):
    qt_ref = refs[0]
    kv_refs = refs[1:1 + 2 * n_seg]
    lam_ref, sub_ref, o_ref, m_ref, l_ref, acc_ref = refs[1 + 2 * n_seg:]
    qt = qt_ref[...]
    first_map = lax.broadcasted_iota(jnp.int32, qt.shape, 0) < QK_DIM
    zero = jnp.zeros_like(qt)
    ws = (jnp.where(first_map, qt, zero), jnp.where(first_map, zero, qt))

    m_ref[...] = jnp.full(m_ref.shape, -jnp.inf, F32)
    l_ref[...] = jnp.zeros(l_ref.shape, F32)
    acc_ref[...] = jnp.zeros(acc_ref.shape, F32)

    def chunk(k_ref, vt_ref, start, size):
        k = k_ref[pl.ds(start, size), :]
        vt = vt_ref[:, pl.ds(start, size)]
        for mp in range(2):
            s = jnp.dot(k, ws[mp], preferred_element_type=F32)
            m_old = m_ref[mp]
            m_new = jnp.maximum(m_old, jnp.max(s, axis=0, keepdims=True))
            alpha = jnp.exp2(m_old - m_new)
            e = jnp.exp2(s - m_new)
            l_ref[mp] = alpha * l_ref[mp] + jnp.sum(e, axis=0, keepdims=True)
            acc_ref[mp] = alpha * acc_ref[mp] + jnp.dot(vt, e.astype(BF16), preferred_element_type=F32)
            m_ref[mp] = m_new

    for si in range(n_seg):
        k_ref, vt_ref = kv_refs[2 * si], kv_refs[2 * si + 1]
        length = seg_len[si]
        size = min(tk, length)
        n_chunks = length // size
        if n_chunks == 1:
            chunk(k_ref, vt_ref, 0, size)
        else:
            def body(j, carry, k_ref=k_ref, vt_ref=vt_ref, size=size):
                chunk(k_ref, vt_ref, pl.multiple_of(j * size, size), size)
                return carry
            lax.fori_loop(0, n_chunks, body, 0)

    lv = lam_ref[...]
    lam = (jnp.exp(jnp.sum(lv[0:1] * lv[1:2], axis=-1, keepdims=True))
           - jnp.exp(jnp.sum(lv[2:3] * lv[3:4], axis=-1, keepdims=True)) + lam_init)
    o = acc_ref[0] / l_ref[0] - lam * (acc_ref[1] / l_ref[1])
    ms = jnp.mean(o * o, axis=0, keepdims=True)
    o = o * lax.rsqrt(ms + EPS) * (sub_ref[...] * (1.0 - lam_init))
    o_ref[...] = o.T.astype(o_ref.dtype)


def _attn_unshifted_kernel(*refs, n_seg, seg_len, tq, tk, lam_init):
    qt_ref = refs[0]
    kv_refs = refs[1:1 + 2 * n_seg]
    lam_ref, sub_ref, o_ref, l_ref, acc_ref, sa_ref, sb_ref = refs[1 + 2 * n_seg:]
    first_map = lax.broadcasted_iota(jnp.int32, (HEAD_W, tq), 0) < QK_DIM
    lv = lam_ref[...]
    lam = (jnp.exp(jnp.sum(lv[0:1] * lv[1:2], axis=-1, keepdims=True))
           - jnp.exp(jnp.sum(lv[2:3] * lv[3:4], axis=-1, keepdims=True)) + lam_init)
    out_gain = sub_ref[...] * (1.0 - lam_init)

    chunks = []
    for i in range(n_seg):
        size = min(tk, seg_len[i])
        assert seg_len[i] % size == 0
        chunks += [(kv_refs[2 * i], kv_refs[2 * i + 1], c * size, size) for c in range(seg_len[i] // size)]
    bufs = (sa_ref, sb_ref)

    def query_tile(qi, carry):
        q0 = pl.multiple_of(qi * tq, tq)
        qt = qt_ref[:, pl.ds(q0, tq)]
        zero = jnp.zeros_like(qt)
        w = jnp.concatenate([jnp.where(first_map, qt, zero), jnp.where(first_map, zero, qt)], axis=1)
        l_ref[...] = jnp.zeros(l_ref.shape, F32)
        acc_ref[...] = jnp.zeros(acc_ref.shape, F32)

        def scores(s_ref, k_ref, start, size):
            s_ref[0:size, :] = jnp.dot(k_ref[pl.ds(start, size), :], w, preferred_element_type=F32)

        def consume(s_ref, vt_ref, start, size):
            e = jnp.exp2(s_ref[0:size, :])
            l_ref[...] += jnp.sum(e.reshape(size // 8, 8, 2 * tq), axis=0)
            acc_ref[...] += jnp.dot(vt_ref[:, pl.ds(start, size)], e.astype(BF16), preferred_element_type=F32)

        scores(bufs[0], chunks[0][0], chunks[0][2], chunks[0][3])
        for i, (_, vt_ref, start, size) in enumerate(chunks):
            if i + 1 < len(chunks):
                nxt = chunks[i + 1]
                scores(bufs[(i + 1) % 2], nxt[0], nxt[2], nxt[3])
            consume(bufs[i % 2], vt_ref, start, size)

        lsum = jnp.sum(l_ref[...], axis=0, keepdims=True)
        acc = acc_ref[...]
        o = acc[:, :tq] / lsum[:, :tq] - lam * (acc[:, tq:] / lsum[:, tq:])
        ms = jnp.mean(o * o, axis=0, keepdims=True)
        o = o * lax.rsqrt(ms + EPS) * out_gain
        o_ref[pl.ds(q0, tq), :] = o.T.astype(o_ref.dtype)
        return carry

    lax.fori_loop(0, qt_ref.shape[1] // tq, query_tile, 0)


def _attention(qt, segs, lam_vecs, subln_col, lam_init, *, tq, tk, shifted):
    b, _, lq = qt.shape
    tq = min(tq, lq)
    seg_len = tuple(k.shape[1] for k, _ in segs)
    const = lambda arr: pl.BlockSpec(arr.shape, lambda bi, h, *i: (0, 0))
    if shifted:
        tq = min(tq, SHIFTED_TQ)
        grid = (b, N_HEADS, lq // tq)
        q_spec = pl.BlockSpec((None, HEAD_W, tq), lambda bi, h, i: (bi, h, i))
        o_spec = pl.BlockSpec((None, tq, HEAD_W), lambda bi, h, i: (bi, i, h))
        body = functools.partial(_attn_kernel, n_seg=len(segs), seg_len=seg_len, tk=tk, lam_init=lam_init)
        scratch = [pltpu.VMEM((2, 1, tq), F32), pltpu.VMEM((2, 1, tq), F32), pltpu.VMEM((2, HEAD_W, tq), F32)]
        semantics = ("parallel", "parallel", "arbitrary")
    else:
        grid = (b, N_HEADS)
        q_spec = pl.BlockSpec((None, HEAD_W, lq), lambda bi, h: (bi, h, 0))
        o_spec = pl.BlockSpec((None, lq, HEAD_W), lambda bi, h: (bi, 0, h))
        body = functools.partial(_attn_unshifted_kernel, n_seg=len(segs), seg_len=seg_len, tq=tq, tk=tk,
                                 lam_init=lam_init)
        s_rows = min(tk, max(seg_len))
        scratch = [pltpu.VMEM((8, 2 * tq), F32), pltpu.VMEM((HEAD_W, 2 * tq), F32),
                   pltpu.VMEM((s_rows, 2 * tq), F32), pltpu.VMEM((s_rows, 2 * tq), F32)]
        semantics = ("parallel", "parallel")
    in_specs, args = [q_spec], [qt]
    for (k, vt), lk in zip(segs, seg_len):
        in_specs.append(pl.BlockSpec((None, lk, HEAD_W), lambda bi, h, *i: (bi, 0, h)))
        in_specs.append(pl.BlockSpec((None, HEAD_W, lk), lambda bi, h, *i: (bi, h, 0)))
        args += [k, vt]
    in_specs += [const(lam_vecs), const(subln_col)]
    args += [lam_vecs, subln_col]
    return pl.pallas_call(
        body,
        grid=grid,
        in_specs=in_specs,
        out_specs=o_spec,
        out_shape=jax.ShapeDtypeStruct((b, lq, ATTN_W), BF16),
        scratch_shapes=scratch,
        compiler_params=_cparams(*semantics),
        name="diff_attention_shifted" if shifted else "diff_attention",
    )(*args)


def _attention_dispatch(score_bound, *args, **kw):
    return lax.cond(score_bound <= MAX_UNSHIFTED_SCORE,
                    lambda: _attention(*args, shifted=False, **kw),
                    lambda: _attention(*args, shifted=True, **kw))


def _fourier_kernel(u_ref, v_ref, m1_ref, c2_ref, s2_ref, o_ref, yr_ref, yi_ref, *, n1, n2):
    def stage1(j, carry):
        xu = u_ref[pl.ds(j, n1, stride=n2), :]
        xv = v_ref[pl.ds(j, n1, stride=n2), :]
        x = jnp.concatenate([xu, xv], axis=1).astype(BF16)
        p = jnp.dot(m1_ref[j], x, preferred_element_type=F32)
        rows = pl.ds(pl.multiple_of(j * n1, n1), n1)
        yr_ref[rows, :] = p[:n1, :LANES] + p[n1:, LANES:]
        yi_ref[rows, :] = p[n1:, :LANES] - p[:n1, LANES:]
        return carry

    lax.fori_loop(0, n2, stage1, 0, unroll=FOURIER_UNROLL)
    c2, s2 = c2_ref[...], s2_ref[...]

    def stage2(k1, carry):
        rows = pl.ds(k1, n2, stride=n1)
        xr = (jnp.dot(c2, yr_ref[rows, :].astype(BF16), preferred_element_type=F32)
              + jnp.dot(s2, yi_ref[rows, :].astype(BF16), preferred_element_type=F32))
        o_ref[rows, :] = xr
        return carry

    lax.fori_loop(0, n1, stage2, 0, unroll=FOURIER_UNROLL)


def _fourier_latent(uv, n1, n2):
    b, l, _ = uv.shape
    m1, c2, s2 = _stage_tables(n1, n2)
    m1, c2, s2 = (jnp.asarray(t).astype(BF16) for t in (m1, c2, s2))
    halves = MIX_Q // LANES
    return pl.pallas_call(
        functools.partial(_fourier_kernel, n1=n1, n2=n2),
        grid=(b, halves),
        in_specs=[
            pl.BlockSpec((None, l, LANES), lambda bi, c: (bi, 0, c)),
            pl.BlockSpec((None, l, LANES), lambda bi, c: (bi, 0, c + halves)),
            pl.BlockSpec(m1.shape, lambda bi, c: (0, 0, 0)),
            pl.BlockSpec(c2.shape, lambda bi, c: (0, 0)),
            pl.BlockSpec(s2.shape, lambda bi, c: (0, 0)),
        ],
        out_specs=pl.BlockSpec((None, l, LANES), lambda bi, c: (bi, 0, c)),
        out_shape=jax.ShapeDtypeStruct((b, l, MIX_Q), F32),
        scratch_shapes=[pltpu.VMEM((l, LANES), F32), pltpu.VMEM((l, LANES), F32)],
        compiler_params=_cparams("parallel", "parallel"),
        name="fourier_positions",
    )(uv, uv, m1, c2, s2)


def _fourier_dense_kernel(uv_ref, c_ref, s_ref, o_ref):
    uv = uv_ref[...].astype(BF16)
    o_ref[...] = (jnp.dot(c_ref[...], uv[:, :MIX_Q], preferred_element_type=F32)
                  - jnp.dot(s_ref[...], uv[:, MIX_Q:], preferred_element_type=F32))


def _fourier_dense(uv):
    b, l, _ = uv.shape
    n = np.arange(l)
    ang = 2.0 * np.pi * np.outer(n, n) / l
    c = jnp.asarray((np.cos(ang) / math.sqrt(l)).astype(np.float32)).astype(BF16)
    s = jnp.asarray((np.sin(ang) / math.sqrt(l)).astype(np.float32)).astype(BF16)
    return pl.pallas_call(
        _fourier_dense_kernel,
        grid=(b,),
        in_specs=[pl.BlockSpec((None, l, 2 * MIX_Q), lambda bi: (bi, 0, 0)),
                  pl.BlockSpec((l, l), lambda bi: (0, 0)),
                  pl.BlockSpec((l, l), lambda bi: (0, 0))],
        out_specs=pl.BlockSpec((None, l, MIX_Q), lambda bi: (bi, 0, 0)),
        out_shape=jax.ShapeDtypeStruct((b, l, MIX_Q), F32),
        compiler_params=_cparams("parallel"),
        name="fourier_positions_ctx",
    )(uv, c, s)


POOL_HALO = 8


def _pool_kernel(b_ref, o_ref, pad_ref, *, l, rows):
    zeros = jnp.zeros((POOL_HALO, LANES), F32)
    lane = lax.broadcasted_iota(jnp.int32, (rows, LANES), 1)
    low_group = lane < GROUP_W
    n_chunks = l // rows
    for tile in range(MIX_Q // LANES):
        w_lo, w_hi = POOL_WINDOWS[2 * tile], POOL_WINDOWS[2 * tile + 1]
        pad_ref[0:POOL_HALO, :] = zeros
        pad_ref[POOL_HALO + l:, :] = zeros
        pad_ref[POOL_HALO:POOL_HALO + l, :] = b_ref[:, tile * LANES:(tile + 1) * LANES]

        def chunk(t0, edge, tile=tile, w_lo=w_lo, w_hi=w_hi):
            shifted = lambda dlt: pad_ref[pl.ds(t0 + POOL_HALO + dlt, rows), :]
            lo_range = range(-(w_lo // 2), w_lo // 2)
            inner = functools.reduce(lambda a, b: a + b, [shifted(d) for d in lo_range])
            outer = functools.reduce(lambda a, b: a + b,
                                     [shifted(d) for d in range(-(w_hi // 2), w_hi // 2) if d not in lo_range], inner)
            sums = jnp.where(low_group, inner, outer)
            if edge:
                pos = t0 + lax.broadcasted_iota(jnp.int32, (rows, LANES), 0)
                half = jnp.where(low_group, w_lo // 2, w_hi // 2)
                cnt = jnp.minimum(pos + (half - 1), l - 1) - jnp.maximum(pos - half, 0) + 1
                mean = sums / cnt.astype(F32)
            else:
                mean = sums * jnp.where(low_group, 1.0 / w_lo, 1.0 / w_hi)
            o_ref[pl.ds(t0, rows), tile * LANES:(tile + 1) * LANES] = mean - shifted(0)

        assert rows >= POOL_HALO
        chunk(0, True)
        if n_chunks > 1:
            def body(ci, carry, chunk=chunk):
                chunk(pl.multiple_of(ci * rows, rows), False)
                return carry

            lax.fori_loop(1, n_chunks - 1, body, 0)
            chunk((n_chunks - 1) * rows, True)


def _pool(bp):
    b, l, _ = bp.shape
    rows = 256
    return pl.pallas_call(
        functools.partial(_pool_kernel, l=l, rows=rows),
        grid=(b,),
        in_specs=[pl.BlockSpec((None, l, MIX_Q), lambda bi: (bi, 0, 0))],
        out_specs=pl.BlockSpec((None, l, MIX_Q), lambda bi: (bi, 0, 0)),
        out_shape=jax.ShapeDtypeStruct((b, l, MIX_Q), F32),
        scratch_shapes=[pltpu.VMEM((l + 2 * POOL_HALO, LANES), F32)],
        compiler_params=_cparams("parallel"),
        name="pool_windows",
    )(bp)


def _outproj_kernel(x_ref, f_ref, p_ref, a_ref, sg_ref, mod_ref, w_ref, o_ref, *, d, ctx_row):
    row = ctx_row if ctx_row is not None else pl.program_id(0)
    o_ref[...] = _residual_update(x_ref, f_ref, p_ref, a_ref, sg_ref, mod_ref, w_ref, row, d)


def _out_projection(x, four, pool, attn, sg, mod, w_out, *, layer, ctx_row, tl):
    b, l, d = x.shape
    row_blk = lambda width: pl.BlockSpec((None, tl, width), lambda bi, i: (bi, i, 0))
    of_layer = lambda arr: pl.BlockSpec((None,) + arr.shape[1:], lambda bi, i: (layer,) + (0,) * (arr.ndim - 1))
    return pl.pallas_call(
        functools.partial(_outproj_kernel, d=d, ctx_row=ctx_row),
        grid=(b, l // tl),
        in_specs=[row_blk(d), row_blk(MIX_Q), row_blk(MIX_Q), row_blk(ATTN_W), row_blk(d), of_layer(mod),
                  of_layer(w_out)],
        out_specs=row_blk(d),
        out_shape=jax.ShapeDtypeStruct((b, l, d), F32),
        compiler_params=_cparams("parallel", "parallel"),
        name="out_projection_ctx" if ctx_row is not None else "out_projection",
    )(x, four, pool, attn, sg, mod, w_out)


def kernel(x, c, ctx, c_ctx, norm_g, w_mod, b_mod, w_in, w_fourier, w_pool, pool_scale, qk_norm_g, lam_vecs,
           subln_g, w_out):
    b, l, d = x.shape
    depth = w_in.shape[0]
    ctx_len = ctx.shape[1]
    ctx_row = b

    cc = jnp.concatenate([c, c_ctx[None, :], jnp.zeros((8 - b - 1, d), F32)], axis=0)
    mod = _modulation(cc, w_mod, b_mod)
    w_eff = _fold_weights(w_in, w_fourier, w_pool, pool_scale)
    w_out_bf = _cast_bf16(w_out)
    rope_tabs = _rope_tables(l // GRID_W)
    lane_chunk = lax.broadcasted_iota(jnp.int32, (2 * LANES, 2 * LANES), 0) // QK_DIM
    ones_blk = (lane_chunk == lane_chunk.T).astype(BF16)
    n1 = 64
    n2 = l // n1

    pending = None
    for layer in range(depth):
        lam_init = 0.8 - 0.6 * math.exp(-0.3 * layer)
        update_ctx = layer < depth - 1
        ng = norm_g[layer][None, :]
        qk_gain = jnp.tile(qk_norm_g[layer], (1, 2 * LANES // QK_DIM))
        sub_g = subln_g[layer][:, None]
        lv = lam_vecs[layer]

        outs = _in_projection(x, mod, ng, w_eff, qk_gain, ones_blk, rope_tabs, layer=layer, ctx_row=None,
                              tl=512, prev=pending)
        if pending is not None:
            x, outs = outs[0], outs[1:]
        uv, bp, qt, k, vt, sg = outs
        uv_c, bp_c, qt_c, k_c, vt_c, sg_c = _in_projection(ctx, mod, ng, w_eff, qk_gain, ones_blk, None,
                                                           layer=layer, ctx_row=ctx_row, tl=ctx_len)
        gains = jnp.max(jnp.abs(qk_norm_g[layer]), axis=-1)
        score_bound = QK_DIM * gains[0] * gains[1] * (Q_SCALE * BF16_ROUNDING_SLACK)
        attn = _attention_dispatch(score_bound, qt, [(k_c, vt_c), (k, vt)], lv, sub_g, lam_init, tq=ATTN_TQ,
                                  tk=ATTN_TK)
        four = _fourier_latent(uv, n1, n2)
        pool = _pool(bp)
        if update_ctx:
            attn_c = _attention_dispatch(score_bound, qt_c, [(k_c, vt_c)], lv, sub_g, lam_init, tq=ctx_len, tk=512)
            ctx = _out_projection(ctx, _fourier_dense(uv_c), _pool(bp_c), attn_c, sg_c, mod, w_out_bf,
                                  layer=layer, ctx_row=ctx_row, tl=ctx_len)
        if layer + 1 < depth:
            pending = (four, pool, attn, sg, w_out_bf)
        else:
            x = _out_projection(x, four, pool, attn, sg, mod, w_out_bf, layer=layer, ctx_row=None, tl=1024)
    return x
```

```python
import functools
import math

import numpy as np
import jax
import jax.numpy as jnp
from jax import lax
from jax.experimental import pallas as pl
from jax.experimental.pallas import tpu as pltpu

GRID_W = 64
FOURIER_GROUPS = 4
GROUP_W = 64
MIX_Q = 256
POOL_WINDOWS = (2, 4, 8, 16)
QK_DIM = 64
HEAD_W = 2 * QK_DIM
N_HEADS = 4
ATTN_W = N_HEADS * HEAD_W
ROPE_THETA = 10000.0
EPS = 1e-6
Q_SCALE = QK_DIM ** -0.5 * math.log2(math.e)
FOURIER_UNROLL = 8
ATTN_TK = 512
ATTN_TQ = 1024
SHIFTED_TQ = 512
MAX_UNSHIFTED_SCORE = 40.0
BF16_ROUNDING_SLACK = 1.02
U_OFF, V_OFF, P_OFF, Q_OFF, K_OFF, VV_OFF, G_OFF, W_EFF = 0, 256, 512, 768, 1280, 1792, 2304, 3328

LANES = 128
VMEM_LIMIT = 56 * 1024 * 1024
HI = lax.Precision.HIGHEST
F32 = jnp.float32
BF16 = jnp.bfloat16


def _cparams(*sem):
    return pltpu.CompilerParams(dimension_semantics=sem, vmem_limit_bytes=VMEM_LIMIT)


def _channel_dft():
    c = np.arange(GROUP_W)
    ang = 2.0 * np.pi * np.outer(c, c) / GROUP_W
    s = 1.0 / math.sqrt(GROUP_W)
    return np.cos(ang) * s, np.sin(ang) * s


def _stage_tables(n1, n2):
    n = n1 * n2
    k1 = np.arange(n1)
    i1 = np.arange(n1)
    j = np.arange(n2)
    theta = 2.0 * np.pi * (j[:, None, None] * k1[None, :, None] / n + k1[None, :, None] * i1[None, None, :] / n1)
    m1 = np.concatenate([np.cos(theta), -np.sin(theta)], axis=1) / math.sqrt(n)
    ang2 = 2.0 * np.pi * np.outer(j, j) / n2
    return m1.astype(np.float32), np.cos(ang2).astype(np.float32), np.sin(ang2).astype(np.float32)


def _rope_tables(rows):
    row = jnp.repeat(jnp.arange(rows), GRID_W).astype(F32)
    col = jnp.tile(jnp.arange(GRID_W), rows).astype(F32)
    half = QK_DIM // 2
    inv_freq = ROPE_THETA ** (-jnp.arange(0, half, 2, dtype=F32) / half)
    ang_r = row[:, None] * inv_freq[None, :]
    ang_c = col[:, None] * inv_freq[None, :]
    ang = jnp.concatenate([ang_r, ang_r, ang_c, ang_c], axis=-1)
    cos, sin = jnp.cos(ang), jnp.sin(ang)
    quarter = QK_DIM // 4
    first = (jnp.arange(QK_DIM) % half) < quarter
    sin_up = jnp.where(first, -sin, 0.0)
    sin_dn = jnp.where(first, 0.0, sin)
    rep = LANES // QK_DIM
    return tuple(jnp.tile(t, (1, rep)) for t in (cos, sin_up, sin_dn))


def _fold_kernel(w_ref, wf_ref, wp_ref, ps_ref, cd_ref, sd_ref, o_ref):
    cd, sd = cd_ref[...], sd_ref[...]
    for g in range(FOURIER_GROUPS):
        sl = slice(g * GROUP_W, (g + 1) * GROUP_W)
        wa = w_ref[:, sl]
        wf = wf_ref[g]
        cw = jnp.dot(cd, wf, precision=HI, preferred_element_type=F32)
        sw = jnp.dot(sd, wf, precision=HI, preferred_element_type=F32)
        o_ref[:, U_OFF + g * GROUP_W:U_OFF + (g + 1) * GROUP_W] = jnp.dot(
            wa, cw, precision=HI, preferred_element_type=F32).astype(o_ref.dtype)
        o_ref[:, V_OFF + g * GROUP_W:V_OFF + (g + 1) * GROUP_W] = jnp.dot(
            wa, sw, precision=HI, preferred_element_type=F32).astype(o_ref.dtype)
        wb = w_ref[:, MIX_Q + g * GROUP_W:MIX_Q + (g + 1) * GROUP_W]
        pw = jnp.dot(wb, wp_ref[g], precision=HI, preferred_element_type=F32) * ps_ref[:, sl]
        o_ref[:, P_OFF + g * GROUP_W:P_OFF + (g + 1) * GROUP_W] = pw.astype(o_ref.dtype)
    o_ref[:, Q_OFF:] = w_ref[:, 2 * MIX_Q:].astype(o_ref.dtype)


def _fold_weights(w_in, w_fourier, w_pool, pool_scale):
    depth, d, in_w = w_in.shape
    tr = 256
    cd, sd = _channel_dft()
    return pl.pallas_call(
        _fold_kernel,
        grid=(depth, d // tr),
        in_specs=[
            pl.BlockSpec((None, tr, in_w), lambda l, i: (l, i, 0)),
            pl.BlockSpec((None, FOURIER_GROUPS, GROUP_W, GROUP_W), lambda l, i: (l, 0, 0, 0)),
            pl.BlockSpec((None, FOURIER_GROUPS, GROUP_W, GROUP_W), lambda l, i: (l, 0, 0, 0)),
            pl.BlockSpec((None, 1, MIX_Q), lambda l, i: (l, 0, 0)),
            pl.BlockSpec((GROUP_W, GROUP_W), lambda l, i: (0, 0)),
            pl.BlockSpec((GROUP_W, GROUP_W), lambda l, i: (0, 0)),
        ],
        out_specs=pl.BlockSpec((None, tr, W_EFF), lambda l, i: (l, i, 0)),
        out_shape=jax.ShapeDtypeStruct((depth, d, W_EFF), BF16),
        compiler_params=_cparams("parallel", "parallel"),
        name="fold_weights",
    )(w_in, w_fourier, w_pool, pool_scale[:, None, :], jnp.asarray(cd, F32), jnp.asarray(sd, F32))


def _cast_kernel(w_ref, o_ref):
    o_ref[...] = w_ref[...].astype(o_ref.dtype)


def _cast_bf16(w):
    depth, r, c = w.shape
    return pl.pallas_call(
        _cast_kernel,
        grid=(depth,),
        in_specs=[pl.BlockSpec((None, r, c), lambda l: (l, 0, 0))],
        out_specs=pl.BlockSpec((None, r, c), lambda l: (l, 0, 0)),
        out_shape=jax.ShapeDtypeStruct(w.shape, BF16),
        compiler_params=_cparams("parallel"),
        name="cast_w_out",
    )(w)


def _mod_kernel(c_ref, w_ref, b_ref, o_ref):
    cc = c_ref[...]
    sc = cc * jax.nn.sigmoid(cc)
    o_ref[...] = jnp.dot(sc, w_ref[...], precision=HI, preferred_element_type=F32) + b_ref[...]


def _modulation(cc, w_mod, b_mod):
    depth, d, n = w_mod.shape
    tn = 1024
    rows = cc.shape[0]
    return pl.pallas_call(
        _mod_kernel,
        grid=(depth, n // tn),
        in_specs=[
            pl.BlockSpec((rows, d), lambda l, j: (0, 0)),
            pl.BlockSpec((None, d, tn), lambda l, j: (l, 0, j)),
            pl.BlockSpec((None, 1, tn), lambda l, j: (l, 0, j)),
        ],
        out_specs=pl.BlockSpec((None, rows, tn), lambda l, j: (l, 0, j)),
        out_shape=jax.ShapeDtypeStruct((depth, rows, n), F32),
        compiler_params=_cparams("parallel", "parallel"),
        name="modulation",
    )(cc, w_mod, b_mod[:, None, :])


def _qk_normalise(t, ones_blk, gain):
    ss = jnp.dot((t * t).astype(BF16), ones_blk, preferred_element_type=F32)
    return t * lax.rsqrt(ss * (1.0 / QK_DIM) + EPS) * gain


def _rotate(t, cos, sin_up, sin_dn):
    quarter = QK_DIM // 4
    up = pltpu.roll(t, LANES - quarter, 1)
    dn = pltpu.roll(t, quarter, 1)
    return t * cos + up * sin_up + dn * sin_dn


def _residual_update(x_ref, f_ref, p_ref, a_ref, sg_ref, mod_ref, w_ref, row, d, rows=slice(None)):
    gate = mod_ref[pl.ds(row, 1), 2 * d:]
    sg = sg_ref[rows, :].astype(F32)
    yf = (f_ref[rows, :] * sg[:, :MIX_Q]).astype(BF16)
    yp = (p_ref[rows, :] * sg[:, MIX_Q:2 * MIX_Q]).astype(BF16)
    ya = (a_ref[rows, :].astype(F32) * sg[:, 2 * MIX_Q:]).astype(BF16)
    acc = jnp.dot(yf, w_ref[0:MIX_Q, :], preferred_element_type=F32)
    acc += jnp.dot(yp, w_ref[MIX_Q:2 * MIX_Q, :], preferred_element_type=F32)
    acc += jnp.dot(ya, w_ref[2 * MIX_Q:, :], preferred_element_type=F32)
    return x_ref[rows, :] + gate * acc


def _inproj_kernel(x_ref, *rest, d, ctx_row, rope, fused):
    row = ctx_row if ctx_row is not None else pl.program_id(0)
    if fused:
        prev, rest = rest[:6], rest[6:]
    mod_ref, ng_ref, w_ref, qkg_ref, ones_ref = rest[:5]
    rest = rest[5:]
    if rope:
        cos_ref, sup_ref, sdn_ref = rest[:3]
        rest = rest[3:]
    mod = mod_ref[pl.ds(row, 1), :]
    shift, gain = mod[:, :d], ng_ref[...] * (1.0 + mod[:, d:2 * d])

    def modulated_norm(xf):
        ms = jnp.mean(xf * xf, axis=-1, keepdims=True)
        return (xf * lax.rsqrt(ms + EPS) * gain + shift).astype(BF16)

    if fused:
        xnew_ref, rest = rest[0], rest[1:]
        half = x_ref.shape[0] // 2
        h_parts = []
        for part in range(2):
            rows = slice(part * half, (part + 1) * half)
            xf = _residual_update(x_ref, *prev, row, d, rows=rows)
            xnew_ref[rows, :] = xf
            h_parts.append(modulated_norm(xf))
        h = jnp.concatenate(h_parts, axis=0)
    else:
        h_parts = [modulated_norm(x_ref[...])]
        h = h_parts[0]
    uv_ref, bp_ref, qt_ref, k_ref, vt_ref, sg_ref = rest

    def proj(lo, hi, by_parts=False):
        if by_parts and len(h_parts) > 1:
            return jnp.concatenate([jnp.dot(hp, w_ref[:, lo:hi], preferred_element_type=F32) for hp in h_parts],
                                   axis=0)
        return jnp.dot(h, w_ref[:, lo:hi], preferred_element_type=F32)

    pair = 2 * LANES
    n_pairs = ATTN_W // pair
    ones_blk = ones_ref[...]
    q_raw = [proj(Q_OFF + p * pair, Q_OFF + (p + 1) * pair, by_parts=True) for p in range(n_pairs)]
    k_raw = [proj(K_OFF + p * pair, K_OFF + (p + 1) * pair) for p in range(n_pairs)]
    q_nrm = [_qk_normalise(t, ones_blk, qkg_ref[0:1, :]) for t in q_raw]
    k_nrm = [_qk_normalise(t, ones_blk, qkg_ref[1:2, :]) for t in k_raw]
    v_raw = [proj(VV_OFF + p * pair, VV_OFF + (p + 1) * pair) for p in range(n_pairs)]
    g = proj(G_OFF, W_EFF)
    sg_ref[...] = (g * jax.nn.sigmoid(g)).astype(sg_ref.dtype)
    uv_ref[...] = proj(U_OFF, P_OFF)
    bp_ref[...] = proj(P_OFF, Q_OFF)

    if rope:
        cos, sup, sdn = cos_ref[...], sup_ref[...], sdn_ref[...]
    for p in range(n_pairs):
        qp, kp, vp = q_nrm[p], k_nrm[p], v_raw[p]
        for half in range(2):
            sl = slice(p * pair + half * LANES, p * pair + (half + 1) * LANES)
            hs = slice(half * LANES, (half + 1) * LANES)
            qt, kt = qp[:, hs], kp[:, hs]
            if rope:
                qt = _rotate(qt, cos, sup, sdn)
                kt = _rotate(kt, cos, sup, sdn)
            qt_ref[sl, :] = (qt * Q_SCALE).T.astype(qt_ref.dtype)
            k_ref[:, sl] = kt.astype(k_ref.dtype)
            vt_ref[sl, :] = vp[:, hs].T.astype(vt_ref.dtype)


def _in_projection(x, mod, norm_g, w_eff, qk_gain, ones_blk, rope_tabs, *, layer, ctx_row, tl, prev=None):
    b, l, d = x.shape
    rope = rope_tabs is not None
    fused = prev is not None
    row_blk = lambda width: pl.BlockSpec((None, tl, width), lambda bi, i: (bi, i, 0))
    full = lambda arr: pl.BlockSpec(arr.shape, lambda bi, i: (0,) * arr.ndim)

    def of_layer(arr, which=layer):
        return pl.BlockSpec((None,) + arr.shape[1:], lambda bi, i: (which,) + (0,) * (arr.ndim - 1))

    in_specs, args = [row_blk(d)], [x]
    if fused:
        four, pool, attn, sg_prev, w_out = prev
        in_specs += [row_blk(MIX_Q), row_blk(MIX_Q), row_blk(ATTN_W), row_blk(d), of_layer(mod, layer - 1),
                     of_layer(w_out, layer - 1)]
        args += [four, pool, attn, sg_prev, mod, w_out]
    in_specs += [of_layer(mod), full(norm_g), of_layer(w_eff), full(qk_gain), full(ones_blk)]
    args += [mod, norm_g, w_eff, qk_gain, ones_blk]
    if rope:
        in_specs += [pl.BlockSpec((tl, LANES), lambda bi, i: (i, 0))] * 3
        args += list(rope_tabs)
    out_shape = (
        jax.ShapeDtypeStruct((b, l, 2 * MIX_Q), F32),
        jax.ShapeDtypeStruct((b, l, MIX_Q), F32),
        jax.ShapeDtypeStruct((b, ATTN_W, l), BF16),
        jax.ShapeDtypeStruct((b, l, ATTN_W), BF16),
        jax.ShapeDtypeStruct((b, ATTN_W, l), BF16),
        jax.ShapeDtypeStruct((b, l, d), BF16),
    )
    col_blk = pl.BlockSpec((None, ATTN_W, tl), lambda bi, i: (bi, 0, i))
    out_specs = (row_blk(2 * MIX_Q), row_blk(MIX_Q), col_blk, row_blk(ATTN_W), col_blk, row_blk(d))
    if fused:
        out_shape = (jax.ShapeDtypeStruct((b, l, d), F32),) + out_shape
        out_specs = (row_blk(d),) + out_specs
    name = "in_projection_ctx" if ctx_row is not None else "in_projection"
    return pl.pallas_call(
        functools.partial(_inproj_kernel, d=d, ctx_row=ctx_row, rope=rope, fused=fused),
        grid=(b, l // tl),
        in_specs=in_specs,
        out_specs=out_specs,
        out_shape=out_shape,
        compiler_params=_cparams("parallel", "parallel"),
        name="residual_" + name if fused else name,
    )(*args)


def _attn_kernel(*refs, n_seg, seg_len, tk, lam_init):
    qt_ref = refs[0]
    kv_refs = refs[1:1 + 2 * n_seg]
    lam_ref, sub_ref, o_ref, m_ref, l_ref, acc_ref = refs[1 + 2 * n_seg:]
    qt = qt_ref[...]
    first_map = lax.broadcasted_iota(jnp.int32, qt.shape, 0) < QK_DIM
    zero = jnp.zeros_like(qt)
    ws = (jnp.where(first_map, qt, zero), jnp.where(first_map, zero, qt))

    m_ref[...] = jnp.full(m_ref.shape, -jnp.inf, F32)
    l_ref[...] = jnp.zeros(l_ref.shape, F32)
    acc_ref[...] = jnp.zeros(acc_ref.shape, F32)

    def chunk(k_ref, vt_ref, start, size):
        k = k_ref[pl.ds(start, size), :]
        vt = vt_ref[:, pl.ds(start, size)]
        for mp in range(2):
            s = jnp.dot(k, ws[mp], preferred_element_type=F32)
            m_old = m_ref[mp]
            m_new = jnp.maximum(m_old, jnp.max(s, axis=0, keepdims=True))
            alpha = jnp.exp2(m_old - m_new)
            e = jnp.exp2(s - m_new)
            l_ref[mp] = alpha * l_ref[mp] + jnp.sum(e, axis=0, keepdims=True)
            acc_ref[mp] = alpha * acc_ref[mp] + jnp.dot(vt, e.astype(BF16), preferred_element_type=F32)
            m_ref[mp] = m_new

    for si in range(n_seg):
        k_ref, vt_ref = kv_refs[2 * si], kv_refs[2 * si + 1]
        length = seg_len[si]
        size = min(tk, length)
        n_chunks = length // size
        if n_chunks == 1:
            chunk(k_ref, vt_ref, 0, size)
        else:
            def body(j, carry, k_ref=k_ref, vt_ref=vt_ref, size=size):
                chunk(k_ref, vt_ref, pl.multiple_of(j * size, size), size)
                return carry
            lax.fori_loop(0, n_chunks, body, 0)

    lv = lam_ref[...]
    lam = (jnp.exp(jnp.sum(lv[0:1] * lv[1:2], axis=-1, keepdims=True))
           - jnp.exp(jnp.sum(lv[2:3] * lv[3:4], axis=-1, keepdims=True)) + lam_init)
    o = acc_ref[0] / l_ref[0] - lam * (acc_ref[1] / l_ref[1])
    ms = jnp.mean(o * o, axis=0, keepdims=True)
    o = o * lax.rsqrt(ms + EPS) * (sub_ref[...] * (1.0 - lam_init))
    o_ref[...] = o.T.astype(o_ref.dtype)


def _attn_unshifted_kernel(*refs, n_seg, seg_len, tq, tk, lam_init):
    qt_ref = refs[0]
    kv_refs = refs[1:1 + 2 * n_seg]
    lam_ref, sub_ref, o_ref, l_ref, acc_ref, sa_ref, sb_ref = refs[1 + 2 * n_seg:]
    first_map = lax.broadcasted_iota(jnp.int32, (HEAD_W, tq), 0) < QK_DIM
    lv = lam_ref[...]
    lam = (jnp.exp(jnp.sum(lv[0:1] * lv[1:2], axis=-1, keepdims=True))
           - jnp.exp(jnp.sum(lv[2:3] * lv[3:4], axis=-1, keepdims=True)) + lam_init)
    out_gain = sub_ref[...] * (1.0 - lam_init)

    chunks = []
    for i in range(n_seg):
        size = min(tk, seg_len[i])
        assert seg_len[i] % size == 0
        chunks += [(kv_refs[2 * i], kv_refs[2 * i + 1], c * size, size) for c in range(seg_len[i] // size)]
    bufs = (sa_ref, sb_ref)

    def query_tile(qi, carry):
        q0 = pl.multiple_of(qi * tq, tq)
        qt = qt_ref[:, pl.ds(q0, tq)]
        zero = jnp.zeros_like(qt)
        w = jnp.concatenate([jnp.where(first_map, qt, zero), jnp.where(first_map, zero, qt)], axis=1)
        l_ref[...] = jnp.zeros(l_ref.shape, F32)
        acc_ref[...] = jnp.zeros(acc_ref.shape, F32)

        def scores(s_ref, k_ref, start, size):
            s_ref[0:size, :] = jnp.dot(k_ref[pl.ds(start, size), :], w, preferred_element_type=F32)

        def consume(s_ref, vt_ref, start, size):
            e = jnp.exp2(s_ref[0:size, :])
            l_ref[...] += jnp.sum(e.reshape(size // 8, 8, 2 * tq), axis=0)
            acc_ref[...] += jnp.dot(vt_ref[:, pl.ds(start, size)], e.astype(BF16), preferred_element_type=F32)

        scores(bufs[0], chunks[0][0], chunks[0][2], chunks[0][3])
        for i, (_, vt_ref, start, size) in enumerate(chunks):
            if i + 1 < len(chunks):
                nxt = chunks[i + 1]
                scores(bufs[(i + 1) % 2], nxt[0], nxt[2], nxt[3])
            consume(bufs[i % 2], vt_ref, start, size)

        inv = 1.0 / jnp.sum(l_ref[...], axis=0, keepdims=True)
        acc = acc_ref[...]
        o = acc[:, :tq] * inv[:, :tq] - (lam * inv[:, tq:]) * acc[:, tq:]
        ms = jnp.mean(o * o, axis=0, keepdims=True)
        o = o * lax.rsqrt(ms + EPS) * out_gain
        o_ref[pl.ds(q0, tq), :] = o.T.astype(o_ref.dtype)
        return carry

    lax.fori_loop(0, qt_ref.shape[1] // tq, query_tile, 0)


def _attention(qt, segs, lam_vecs, subln_col, lam_init, *, tq, tk, shifted):
    b, _, lq = qt.shape
    tq = min(tq, lq)
    seg_len = tuple(k.shape[1] for k, _ in segs)
    const = lambda arr: pl.BlockSpec(arr.shape, lambda bi, h, *i: (0, 0))
    if shifted:
        tq = min(tq, SHIFTED_TQ)
        grid = (b, N_HEADS, lq // tq)
        q_spec = pl.BlockSpec((None, HEAD_W, tq), lambda bi, h, i: (bi, h, i))
        o_spec = pl.BlockSpec((None, tq, HEAD_W), lambda bi, h, i: (bi, i, h))
        body = functools.partial(_attn_kernel, n_seg=len(segs), seg_len=seg_len, tk=tk, lam_init=lam_init)
        scratch = [pltpu.VMEM((2, 1, tq), F32), pltpu.VMEM((2, 1, tq), F32), pltpu.VMEM((2, HEAD_W, tq), F32)]
        semantics = ("parallel", "parallel", "arbitrary")
    else:
        grid = (b, N_HEADS)
        q_spec = pl.BlockSpec((None, HEAD_W, lq), lambda bi, h: (bi, h, 0))
        o_spec = pl.BlockSpec((None, lq, HEAD_W), lambda bi, h: (bi, 0, h))
        body = functools.partial(_attn_unshifted_kernel, n_seg=len(segs), seg_len=seg_len, tq=tq, tk=tk,
                                 lam_init=lam_init)
        s_rows = min(tk, max(seg_len))
        scratch = [pltpu.VMEM((8, 2 * tq), F32), pltpu.VMEM((HEAD_W, 2 * tq), F32),
                   pltpu.VMEM((s_rows, 2 * tq), F32), pltpu.VMEM((s_rows, 2 * tq), F32)]
        semantics = ("parallel", "parallel")
    in_specs, args = [q_spec], [qt]
    for (k, vt), lk in zip(segs, seg_len):
        in_specs.append(pl.BlockSpec((None, lk, HEAD_W), lambda bi, h, *i: (bi, 0, h)))
        in_specs.append(pl.BlockSpec((None, HEAD_W, lk), lambda bi, h, *i: (bi, h, 0)))
        args += [k, vt]
    in_specs += [const(lam_vecs), const(subln_col)]
    args += [lam_vecs, subln_col]
    return pl.pallas_call(
        body,
        grid=grid,
        in_specs=in_specs,
        out_specs=o_spec,
        out_shape=jax.ShapeDtypeStruct((b, lq, ATTN_W), BF16),
        scratch_shapes=scratch,
        compiler_params=_cparams(*semantics),
        name="diff_attention_shifted" if shifted else "diff_attention",
    )(*args)


def _attention_dispatch(score_bound, *args, **kw):
    return lax.cond(score_bound <= MAX_UNSHIFTED_SCORE,
                    lambda: _attention(*args, shifted=False, **kw),
                    lambda: _attention(*args, shifted=True, **kw))


def _fourier_kernel(u_ref, v_ref, m1_ref, c2_ref, s2_ref, o_ref, yr_ref, yi_ref, *, n1, n2):
    def stage1(j, carry):
        xu = u_ref[pl.ds(j, n1, stride=n2), :]
        xv = v_ref[pl.ds(j, n1, stride=n2), :]
        x = jnp.concatenate([xu, xv], axis=1).astype(BF16)
        p = jnp.dot(m1_ref[j], x, preferred_element_type=F32)
        rows = pl.ds(pl.multiple_of(j * n1, n1), n1)
        yr_ref[rows, :] = p[:n1, :LANES] + p[n1:, LANES:]
        yi_ref[rows, :] = p[n1:, :LANES] - p[:n1, LANES:]
        return carry

    lax.fori_loop(0, n2, stage1, 0, unroll=FOURIER_UNROLL)
    c2, s2 = c2_ref[...], s2_ref[...]

    def stage2(k1, carry):
        rows = pl.ds(k1, n2, stride=n1)
        xr = (jnp.dot(c2, yr_ref[rows, :].astype(BF16), preferred_element_type=F32)
              + jnp.dot(s2, yi_ref[rows, :].astype(BF16), preferred_element_type=F32))
        o_ref[rows, :] = xr
        return carry

    lax.fori_loop(0, n1, stage2, 0, unroll=FOURIER_UNROLL)


def _fourier_latent(uv, n1, n2):
    b, l, _ = uv.shape
    m1, c2, s2 = _stage_tables(n1, n2)
    m1, c2, s2 = (jnp.asarray(t).astype(BF16) for t in (m1, c2, s2))
    halves = MIX_Q // LANES
    return pl.pallas_call(
        functools.partial(_fourier_kernel, n1=n1, n2=n2),
        grid=(b, halves),
        in_specs=[
            pl.BlockSpec((None, l, LANES), lambda bi, c: (bi, 0, c)),
            pl.BlockSpec((None, l, LANES), lambda bi, c: (bi, 0, c + halves)),
            pl.BlockSpec(m1.shape, lambda bi, c: (0, 0, 0)),
            pl.BlockSpec(c2.shape, lambda bi, c: (0, 0)),
            pl.BlockSpec(s2.shape, lambda bi, c: (0, 0)),
        ],
        out_specs=pl.BlockSpec((None, l, LANES), lambda bi, c: (bi, 0, c)),
        out_shape=jax.ShapeDtypeStruct((b, l, MIX_Q), F32),
        scratch_shapes=[pltpu.VMEM((l, LANES), F32), pltpu.VMEM((l, LANES), F32)],
        compiler_params=_cparams("parallel", "parallel"),
        name="fourier_positions",
    )(uv, uv, m1, c2, s2)


def _fourier_dense_kernel(uv_ref, c_ref, s_ref, o_ref):
    uv = uv_ref[...].astype(BF16)
    o_ref[...] = (jnp.dot(c_ref[...], uv[:, :MIX_Q], preferred_element_type=F32)
                  - jnp.dot(s_ref[...], uv[:, MIX_Q:], preferred_element_type=F32))


def _fourier_dense(uv):
    b, l, _ = uv.shape
    n = np.arange(l)
    ang = 2.0 * np.pi * np.outer(n, n) / l
    c = jnp.asarray((np.cos(ang) / math.sqrt(l)).astype(np.float32)).astype(BF16)
    s = jnp.asarray((np.sin(ang) / math.sqrt(l)).astype(np.float32)).astype(BF16)
    return pl.pallas_call(
        _fourier_dense_kernel,
        grid=(b,),
        in_specs=[pl.BlockSpec((None, l, 2 * MIX_Q), lambda bi: (bi, 0, 0)),
                  pl.BlockSpec((l, l), lambda bi: (0, 0)),
                  pl.BlockSpec((l, l), lambda bi: (0, 0))],
        out_specs=pl.BlockSpec((None, l, MIX_Q), lambda bi: (bi, 0, 0)),
        out_shape=jax.ShapeDtypeStruct((b, l, MIX_Q), F32),
        compiler_params=_cparams("parallel"),
        name="fourier_positions_ctx",
    )(uv, c, s)


POOL_HALO = 8


def _pool_kernel(b_ref, o_ref, pad_ref, *, l, rows):
    zeros = jnp.zeros((POOL_HALO, LANES), F32)
    lane = lax.broadcasted_iota(jnp.int32, (rows, LANES), 1)
    low_group = lane < GROUP_W
    n_chunks = l // rows
    for tile in range(MIX_Q // LANES):
        w_lo, w_hi = POOL_WINDOWS[2 * tile], POOL_WINDOWS[2 * tile + 1]
        pad_ref[0:POOL_HALO, :] = zeros
        pad_ref[POOL_HALO + l:, :] = zeros
        pad_ref[POOL_HALO:POOL_HALO + l, :] = b_ref[:, tile * LANES:(tile + 1) * LANES]

        def chunk(t0, edge, tile=tile, w_lo=w_lo, w_hi=w_hi):
            shifted = lambda dlt: pad_ref[pl.ds(t0 + POOL_HALO + dlt, rows), :]
            lo_range = range(-(w_lo // 2), w_lo // 2)
            inner = functools.reduce(lambda a, b: a + b, [shifted(d) for d in lo_range])
            outer = functools.reduce(lambda a, b: a + b,
                                     [shifted(d) for d in range(-(w_hi // 2), w_hi // 2) if d not in lo_range], inner)
            sums = jnp.where(low_group, inner, outer)
            if edge:
                pos = t0 + lax.broadcasted_iota(jnp.int32, (rows, LANES), 0)
                half = jnp.where(low_group, w_lo // 2, w_hi // 2)
                cnt = jnp.minimum(pos + (half - 1), l - 1) - jnp.maximum(pos - half, 0) + 1
                mean = sums / cnt.astype(F32)
            else:
                mean = sums * jnp.where(low_group, 1.0 / w_lo, 1.0 / w_hi)
            o_ref[pl.ds(t0, rows), tile * LANES:(tile + 1) * LANES] = mean - shifted(0)

        assert rows >= POOL_HALO
        chunk(0, True)
        if n_chunks > 1:
            def body(ci, carry, chunk=chunk):
                chunk(pl.multiple_of(ci * rows, rows), False)
                return carry

            lax.fori_loop(1, n_chunks - 1, body, 0)
            chunk((n_chunks - 1) * rows, True)


def _pool(bp):
    b, l, _ = bp.shape
    rows = 256
    return pl.pallas_call(
        functools.partial(_pool_kernel, l=l, rows=rows),
        grid=(b,),
        in_specs=[pl.BlockSpec((None, l, MIX_Q), lambda bi: (bi, 0, 0))],
        out_specs=pl.BlockSpec((None, l, MIX_Q), lambda bi: (bi, 0, 0)),
        out_shape=jax.ShapeDtypeStruct((b, l, MIX_Q), F32),
        scratch_shapes=[pltpu.VMEM((l + 2 * POOL_HALO, LANES), F32)],
        compiler_params=_cparams("parallel"),
        name="pool_windows",
    )(bp)


def _outproj_kernel(x_ref, f_ref, p_ref, a_ref, sg_ref, mod_ref, w_ref, o_ref, *, d, ctx_row):
    row = ctx_row if ctx_row is not None else pl.program_id(0)
    o_ref[...] = _residual_update(x_ref, f_ref, p_ref, a_ref, sg_ref, mod_ref, w_ref, row, d)


def _out_projection(x, four, pool, attn, sg, mod, w_out, *, layer, ctx_row, tl):
    b, l, d = x.shape
    row_blk = lambda width: pl.BlockSpec((None, tl, width), lambda bi, i: (bi, i, 0))
    of_layer = lambda arr: pl.BlockSpec((None,) + arr.shape[1:], lambda bi, i: (layer,) + (0,) * (arr.ndim - 1))
    return pl.pallas_call(
        functools.partial(_outproj_kernel, d=d, ctx_row=ctx_row),
        grid=(b, l // tl),
        in_specs=[row_blk(d), row_blk(MIX_Q), row_blk(MIX_Q), row_blk(ATTN_W), row_blk(d), of_layer(mod),
                  of_layer(w_out)],
        out_specs=row_blk(d),
        out_shape=jax.ShapeDtypeStruct((b, l, d), F32),
        compiler_params=_cparams("parallel", "parallel"),
        name="out_projection_ctx" if ctx_row is not None else "out_projection",
    )(x, four, pool, attn, sg, mod, w_out)


def kernel(x, c, ctx, c_ctx, norm_g, w_mod, b_mod, w_in, w_fourier, w_pool, pool_scale, qk_norm_g, lam_vecs,
           subln_g, w_out):
    b, l, d = x.shape
    depth = w_in.shape[0]
    ctx_len = ctx.shape[1]
    ctx_row = b

    cc = jnp.concatenate([c, c_ctx[None, :], jnp.zeros((8 - b - 1, d), F32)], axis=0)
    mod = _modulation(cc, w_mod, b_mod)
    w_eff = _fold_weights(w_in, w_fourier, w_pool, pool_scale)
    w_out_bf = _cast_bf16(w_out)
    rope_tabs = _rope_tables(l // GRID_W)
    lane_chunk = lax.broadcasted_iota(jnp.int32, (2 * LANES, 2 * LANES), 0) // QK_DIM
    ones_blk = (lane_chunk == lane_chunk.T).astype(BF16)
    n1 = 64
    n2 = l // n1

    pending = None
    for layer in range(depth):
        lam_init = 0.8 - 0.6 * math.exp(-0.3 * layer)
        update_ctx = layer < depth - 1
        ng = norm_g[layer][None, :]
        qk_gain = jnp.tile(qk_norm_g[layer], (1, 2 * LANES // QK_DIM))
        sub_g = subln_g[layer][:, None]
        lv = lam_vecs[layer]

        outs = _in_projection(x, mod, ng, w_eff, qk_gain, ones_blk, rope_tabs, layer=layer, ctx_row=None,
                              tl=512, prev=pending)
        if pending is not None:
            x, outs = outs[0], outs[1:]
        uv, bp, qt, k, vt, sg = outs
        uv_c, bp_c, qt_c, k_c, vt_c, sg_c = _in_projection(ctx, mod, ng, w_eff, qk_gain, ones_blk, None,
                                                           layer=layer, ctx_row=ctx_row, tl=ctx_len)
        gains = jnp.max(jnp.abs(qk_norm_g[layer]), axis=-1)
        score_bound = QK_DIM * gains[0] * gains[1] * (Q_SCALE * BF16_ROUNDING_SLACK)
        attn = _attention_dispatch(score_bound, qt, [(k_c, vt_c), (k, vt)], lv, sub_g, lam_init, tq=ATTN_TQ,
                                  tk=ATTN_TK)
        four = _fourier_latent(uv, n1, n2)
        pool = _pool(bp)
        if update_ctx:
            attn_c = _attention_dispatch(score_bound, qt_c, [(k_c, vt_c)], lv, sub_g, lam_init, tq=ctx_len, tk=512)
            ctx = _out_projection(ctx, _fourier_dense(uv_c), _pool(bp_c), attn_c, sg_c, mod, w_out_bf,
                                  layer=layer, ctx_row=ctx_row, tl=ctx_len)
        if layer + 1 < depth:
            pending = (four, pool, attn, sg, w_out_bf)
        else:
            x = _out_projection(x, four, pool, attn, sg, mod, w_out_bf, layer=layer, ctx_row=None, tl=1024)
    return x
```

```python
import functools
import math

import numpy as np
import jax
import jax.numpy as jnp
from jax import lax
from jax.experimental import pallas as pl
from jax.experimental.pallas import tpu as pltpu

GRID_W = 64
FOURIER_GROUPS = 4
GROUP_W = 64
MIX_Q = 256
POOL_WINDOWS = (2, 4, 8, 16)
QK_DIM = 64
HEAD_W = 2 * QK_DIM
N_HEADS = 4
ATTN_W = N_HEADS * HEAD_W
ROPE_THETA = 10000.0
EPS = 1e-6
Q_SCALE = QK_DIM ** -0.5 * math.log2(math.e)
FOURIER_UNROLL = 8
ATTN_TK = 512
ATTN_TQ = 1024
SHIFTED_TQ = 512
MAX_UNSHIFTED_SCORE = 40.0
BF16_ROUNDING_SLACK = 1.02
U_OFF, V_OFF, P_OFF, Q_OFF, K_OFF, VV_OFF, G_OFF, W_EFF = 0, 256, 512, 768, 1280, 1792, 2304, 3328

LANES = 128
VMEM_LIMIT = 56 * 1024 * 1024
HI = lax.Precision.HIGHEST
F32 = jnp.float32
BF16 = jnp.bfloat16


def _cparams(*sem):
    return pltpu.CompilerParams(dimension_semantics=sem, vmem_limit_bytes=VMEM_LIMIT)


def _channel_dft():
    c = np.arange(GROUP_W)
    ang = 2.0 * np.pi * np.outer(c, c) / GROUP_W
    s = 1.0 / math.sqrt(GROUP_W)
    return np.cos(ang) * s, np.sin(ang) * s


SUBLANES = 8


def _stage_tables(n1, n2):
    n = n1 * n2
    k1 = np.arange(n1)
    i1 = np.arange(n1)
    j = np.arange(n2)
    theta = 2.0 * np.pi * (j[:, None, None] * k1[None, :, None] / n + k1[None, :, None] * i1[None, None, :] / n1)
    m1 = np.stack([np.cos(theta), -np.sin(theta)], axis=1) / math.sqrt(n)
    m1 = m1.reshape(n2 // SUBLANES, SUBLANES, 2, n1, n1)
    blk = np.zeros((n2 // SUBLANES, 2, n1, SUBLANES, n1, SUBLANES), np.float32)
    for r in range(SUBLANES):
        blk[:, :, :, r, :, r] = m1[:, r]
    blk = blk.reshape(n2 // SUBLANES, 2 * n1 * SUBLANES, n1 * SUBLANES)
    ang2 = 2.0 * np.pi * np.outer(j, j) / n2
    return blk, np.cos(ang2).astype(np.float32), np.sin(ang2).astype(np.float32)


def _rope_tables(rows):
    row = jnp.repeat(jnp.arange(rows), GRID_W).astype(F32)
    col = jnp.tile(jnp.arange(GRID_W), rows).astype(F32)
    half = QK_DIM // 2
    inv_freq = ROPE_THETA ** (-jnp.arange(0, half, 2, dtype=F32) / half)
    ang_r = row[:, None] * inv_freq[None, :]
    ang_c = col[:, None] * inv_freq[None, :]
    ang = jnp.concatenate([ang_r, ang_r, ang_c, ang_c], axis=-1)
    cos, sin = jnp.cos(ang), jnp.sin(ang)
    quarter = QK_DIM // 4
    first = (jnp.arange(QK_DIM) % half) < quarter
    sin_up = jnp.where(first, -sin, 0.0)
    sin_dn = jnp.where(first, 0.0, sin)
    rep = LANES // QK_DIM
    return tuple(jnp.tile(t, (1, rep)) for t in (cos, sin_up, sin_dn))


def _fold_kernel(w_ref, wf_ref, wp_ref, ps_ref, cd_ref, sd_ref, o_ref):
    cd, sd = cd_ref[...], sd_ref[...]
    for g in range(FOURIER_GROUPS):
        sl = slice(g * GROUP_W, (g + 1) * GROUP_W)
        wa = w_ref[:, sl]
        wf = wf_ref[g]
        cw = jnp.dot(cd, wf, precision=HI, preferred_element_type=F32)
        sw = jnp.dot(sd, wf, precision=HI, preferred_element_type=F32)
        o_ref[:, U_OFF + g * GROUP_W:U_OFF + (g + 1) * GROUP_W] = jnp.dot(
            wa, cw, precision=HI, preferred_element_type=F32).astype(o_ref.dtype)
        o_ref[:, V_OFF + g * GROUP_W:V_OFF + (g + 1) * GROUP_W] = jnp.dot(
            wa, sw, precision=HI, preferred_element_type=F32).astype(o_ref.dtype)
        wb = w_ref[:, MIX_Q + g * GROUP_W:MIX_Q + (g + 1) * GROUP_W]
        pw = jnp.dot(wb, wp_ref[g], precision=HI, preferred_element_type=F32) * ps_ref[:, sl]
        o_ref[:, P_OFF + g * GROUP_W:P_OFF + (g + 1) * GROUP_W] = pw.astype(o_ref.dtype)
    o_ref[:, Q_OFF:] = w_ref[:, 2 * MIX_Q:].astype(o_ref.dtype)


def _fold_weights(w_in, w_fourier, w_pool, pool_scale):
    depth, d, in_w = w_in.shape
    tr = 256
    cd, sd = _channel_dft()
    return pl.pallas_call(
        _fold_kernel,
        grid=(depth, d // tr),
        in_specs=[
            pl.BlockSpec((None, tr, in_w), lambda l, i: (l, i, 0)),
            pl.BlockSpec((None, FOURIER_GROUPS, GROUP_W, GROUP_W), lambda l, i: (l, 0, 0, 0)),
            pl.BlockSpec((None, FOURIER_GROUPS, GROUP_W, GROUP_W), lambda l, i: (l, 0, 0, 0)),
            pl.BlockSpec((None, 1, MIX_Q), lambda l, i: (l, 0, 0)),
            pl.BlockSpec((GROUP_W, GROUP_W), lambda l, i: (0, 0)),
            pl.BlockSpec((GROUP_W, GROUP_W), lambda l, i: (0, 0)),
        ],
        out_specs=pl.BlockSpec((None, tr, W_EFF), lambda l, i: (l, i, 0)),
        out_shape=jax.ShapeDtypeStruct((depth, d, W_EFF), BF16),
        compiler_params=_cparams("parallel", "parallel"),
        name="fold_weights",
    )(w_in, w_fourier, w_pool, pool_scale[:, None, :], jnp.asarray(cd, F32), jnp.asarray(sd, F32))


def _cast_kernel(w_ref, o_ref):
    o_ref[...] = w_ref[...].astype(o_ref.dtype)


def _cast_bf16(w):
    depth, r, c = w.shape
    return pl.pallas_call(
        _cast_kernel,
        grid=(depth,),
        in_specs=[pl.BlockSpec((None, r, c), lambda l: (l, 0, 0))],
        out_specs=pl.BlockSpec((None, r, c), lambda l: (l, 0, 0)),
        out_shape=jax.ShapeDtypeStruct(w.shape, BF16),
        compiler_params=_cparams("parallel"),
        name="cast_w_out",
    )(w)


def _mod_kernel(c_ref, w_ref, b_ref, o_ref):
    cc = c_ref[...]
    sc = cc * jax.nn.sigmoid(cc)
    o_ref[...] = jnp.dot(sc, w_ref[...], precision=HI, preferred_element_type=F32) + b_ref[...]


def _modulation(cc, w_mod, b_mod):
    depth, d, n = w_mod.shape
    tn = 1024
    rows = cc.shape[0]
    return pl.pallas_call(
        _mod_kernel,
        grid=(depth, n // tn),
        in_specs=[
            pl.BlockSpec((rows, d), lambda l, j: (0, 0)),
            pl.BlockSpec((None, d, tn), lambda l, j: (l, 0, j)),
            pl.BlockSpec((None, 1, tn), lambda l, j: (l, 0, j)),
        ],
        out_specs=pl.BlockSpec((None, rows, tn), lambda l, j: (l, 0, j)),
        out_shape=jax.ShapeDtypeStruct((depth, rows, n), F32),
        compiler_params=_cparams("parallel", "parallel"),
        name="modulation",
    )(cc, w_mod, b_mod[:, None, :])


def _qk_normalise(t, ones_blk, gain):
    ss = jnp.dot((t * t).astype(BF16), ones_blk, preferred_element_type=F32)
    return t * lax.rsqrt(ss * (1.0 / QK_DIM) + EPS) * gain


def _rotate(t, cos, sin_up, sin_dn):
    quarter = QK_DIM // 4
    up = pltpu.roll(t, LANES - quarter, 1)
    dn = pltpu.roll(t, quarter, 1)
    return t * cos + up * sin_up + dn * sin_dn


def _residual_update(x_ref, f_ref, p_ref, a_ref, sg_ref, mod_ref, w_ref, row, d, rows=slice(None)):
    gate = mod_ref[pl.ds(row, 1), 2 * d:]
    sg = sg_ref[rows, :].astype(F32)
    yf = (f_ref[rows, :] * sg[:, :MIX_Q]).astype(BF16)
    yp = (p_ref[rows, :] * sg[:, MIX_Q:2 * MIX_Q]).astype(BF16)
    ya = (a_ref[rows, :].astype(F32) * sg[:, 2 * MIX_Q:]).astype(BF16)
    acc = jnp.dot(yf, w_ref[0:MIX_Q, :], preferred_element_type=F32)
    acc += jnp.dot(yp, w_ref[MIX_Q:2 * MIX_Q, :], preferred_element_type=F32)
    acc += jnp.dot(ya, w_ref[2 * MIX_Q:, :], preferred_element_type=F32)
    return x_ref[rows, :] + gate * acc


def _inproj_kernel(x_ref, *rest, d, ctx_row, rope, fused):
    row = ctx_row if ctx_row is not None else pl.program_id(0)
    if fused:
        prev, rest = rest[:6], rest[6:]
    mod_ref, ng_ref, w_ref, qkg_ref, ones_ref = rest[:5]
    rest = rest[5:]
    if rope:
        cos_ref, sup_ref, sdn_ref = rest[:3]
        rest = rest[3:]
    mod = mod_ref[pl.ds(row, 1), :]
    shift, gain = mod[:, :d], ng_ref[...] * (1.0 + mod[:, d:2 * d])

    def modulated_norm(xf):
        ms = jnp.mean(xf * xf, axis=-1, keepdims=True)
        return (xf * lax.rsqrt(ms + EPS) * gain + shift).astype(BF16)

    if fused:
        xnew_ref, rest = rest[0], rest[1:]
    half = x_ref.shape[0] // 2
    h_parts = []
    for part in range(2):
        rows = slice(part * half, (part + 1) * half)
        if fused:
            xf = _residual_update(x_ref, *prev, row, d, rows=rows)
            xnew_ref[rows, :] = xf
        else:
            xf = x_ref[rows, :]
        h_parts.append(modulated_norm(xf))
    h = jnp.concatenate(h_parts, axis=0)
    uv_ref, bp_ref, qt_ref, k_ref, vt_ref, sg_ref = rest

    def proj(lo, hi, by_parts=False):
        if by_parts:
            return jnp.concatenate([jnp.dot(hp, w_ref[:, lo:hi], preferred_element_type=F32) for hp in h_parts],
                                   axis=0)
        return jnp.dot(h, w_ref[:, lo:hi], preferred_element_type=F32)

    pair = 2 * LANES
    n_pairs = ATTN_W // pair
    ones_blk = ones_ref[...]
    q_raw = [proj(Q_OFF + p * pair, Q_OFF + (p + 1) * pair, by_parts=True) for p in range(n_pairs)]
    k_raw = [proj(K_OFF + p * pair, K_OFF + (p + 1) * pair) for p in range(n_pairs)]
    q_nrm = [_qk_normalise(t, ones_blk, qkg_ref[0:1, :]) for t in q_raw]
    k_nrm = [_qk_normalise(t, ones_blk, qkg_ref[1:2, :]) for t in k_raw]
    v_raw = [proj(VV_OFF + p * pair, VV_OFF + (p + 1) * pair) for p in range(n_pairs)]
    g = proj(G_OFF, W_EFF)
    sg_ref[...] = (g * jax.nn.sigmoid(g)).astype(sg_ref.dtype)
    uv_ref[...] = proj(U_OFF, P_OFF)
    bp_ref[...] = proj(P_OFF, Q_OFF)

    if rope:
        cos, sup, sdn = cos_ref[...], sup_ref[...], sdn_ref[...]
    for p in range(n_pairs):
        qp, kp, vp = q_nrm[p], k_nrm[p], v_raw[p]
        for half in range(2):
            sl = slice(p * pair + half * LANES, p * pair + (half + 1) * LANES)
            hs = slice(half * LANES, (half + 1) * LANES)
            qt, kt = qp[:, hs], kp[:, hs]
            if rope:
                qt = _rotate(qt, cos, sup, sdn)
                kt = _rotate(kt, cos, sup, sdn)
            qt_ref[sl, :] = (qt * Q_SCALE).T.astype(qt_ref.dtype)
            k_ref[:, sl] = kt.astype(k_ref.dtype)
            vt_ref[sl, :] = vp[:, hs].T.astype(vt_ref.dtype)


def _in_projection(x, mod, norm_g, w_eff, qk_gain, ones_blk, rope_tabs, *, layer, ctx_row, tl, prev=None):
    b, l, d = x.shape
    rope = rope_tabs is not None
    fused = prev is not None
    row_blk = lambda width: pl.BlockSpec((None, tl, width), lambda bi, i: (bi, i, 0))
    full = lambda arr: pl.BlockSpec(arr.shape, lambda bi, i: (0,) * arr.ndim)

    def of_layer(arr, which=layer):
        return pl.BlockSpec((None,) + arr.shape[1:], lambda bi, i: (which,) + (0,) * (arr.ndim - 1))

    in_specs, args = [row_blk(d)], [x]
    if fused:
        four, pool, attn, sg_prev, w_out = prev
        in_specs += [row_blk(MIX_Q), row_blk(MIX_Q), row_blk(ATTN_W), row_blk(d), of_layer(mod, layer - 1),
                     of_layer(w_out, layer - 1)]
        args += [four, pool, attn, sg_prev, mod, w_out]
    in_specs += [of_layer(mod), full(norm_g), of_layer(w_eff), full(qk_gain), full(ones_blk)]
    args += [mod, norm_g, w_eff, qk_gain, ones_blk]
    if rope:
        in_specs += [pl.BlockSpec((tl, LANES), lambda bi, i: (i, 0))] * 3
        args += list(rope_tabs)
    out_shape = (
        jax.ShapeDtypeStruct((b, l, 2 * MIX_Q), F32),
        jax.ShapeDtypeStruct((b, l, MIX_Q), F32),
        jax.ShapeDtypeStruct((b, ATTN_W, l), BF16),
        jax.ShapeDtypeStruct((b, l, ATTN_W), BF16),
        jax.ShapeDtypeStruct((b, ATTN_W, l), BF16),
        jax.ShapeDtypeStruct((b, l, d), BF16),
    )
    col_blk = pl.BlockSpec((None, ATTN_W, tl), lambda bi, i: (bi, 0, i))
    out_specs = (row_blk(2 * MIX_Q), row_blk(MIX_Q), col_blk, row_blk(ATTN_W), col_blk, row_blk(d))
    if fused:
        out_shape = (jax.ShapeDtypeStruct((b, l, d), F32),) + out_shape
        out_specs = (row_blk(d),) + out_specs
    name = "in_projection_ctx" if ctx_row is not None else "in_projection"
    return pl.pallas_call(
        functools.partial(_inproj_kernel, d=d, ctx_row=ctx_row, rope=rope, fused=fused),
        grid=(b, l // tl),
        in_specs=in_specs,
        out_specs=out_specs,
        out_shape=out_shape,
        compiler_params=_cparams("parallel", "parallel"),
        name="residual_" + name if fused else name,
    )(*args)


def _attn_kernel(*refs, n_seg, seg_len, tk, lam_init):
    qt_ref = refs[0]
    kv_refs = refs[1:1 + 2 * n_seg]
    lam_ref, sub_ref, o_ref, m_ref, l_ref, acc_ref = refs[1 + 2 * n_seg:]
    qt = qt_ref[...]
    first_map = lax.broadcasted_iota(jnp.int32, qt.shape, 0) < QK_DIM
    zero = jnp.zeros_like(qt)
    ws = (jnp.where(first_map, qt, zero), jnp.where(first_map, zero, qt))

    m_ref[...] = jnp.full(m_ref.shape, -jnp.inf, F32)
    l_ref[...] = jnp.zeros(l_ref.shape, F32)
    acc_ref[...] = jnp.zeros(acc_ref.shape, F32)

    def chunk(k_ref, vt_ref, start, size):
        k = k_ref[pl.ds(start, size), :]
        vt = vt_ref[:, pl.ds(start, size)]
        for mp in range(2):
            s = jnp.dot(k, ws[mp], preferred_element_type=F32)
            m_old = m_ref[mp]
            m_new = jnp.maximum(m_old, jnp.max(s, axis=0, keepdims=True))
            alpha = jnp.exp2(m_old - m_new)
            e = jnp.exp2(s - m_new)
            l_ref[mp] = alpha * l_ref[mp] + jnp.sum(e, axis=0, keepdims=True)
            acc_ref[mp] = alpha * acc_ref[mp] + jnp.dot(vt, e.astype(BF16), preferred_element_type=F32)
            m_ref[mp] = m_new

    for si in range(n_seg):
        k_ref, vt_ref = kv_refs[2 * si], kv_refs[2 * si + 1]
        length = seg_len[si]
        size = min(tk, length)
        n_chunks = length // size
        if n_chunks == 1:
            chunk(k_ref, vt_ref, 0, size)
        else:
            def body(j, carry, k_ref=k_ref, vt_ref=vt_ref, size=size):
                chunk(k_ref, vt_ref, pl.multiple_of(j * size, size), size)
                return carry
            lax.fori_loop(0, n_chunks, body, 0)

    lv = lam_ref[...]
    lam = (jnp.exp(jnp.sum(lv[0:1] * lv[1:2], axis=-1, keepdims=True))
           - jnp.exp(jnp.sum(lv[2:3] * lv[3:4], axis=-1, keepdims=True)) + lam_init)
    o = acc_ref[0] / l_ref[0] - lam * (acc_ref[1] / l_ref[1])
    ms = jnp.mean(o * o, axis=0, keepdims=True)
    o = o * lax.rsqrt(ms + EPS) * (sub_ref[...] * (1.0 - lam_init))
    o_ref[...] = o.T.astype(o_ref.dtype)


def _attn_unshifted_kernel(*refs, n_seg, seg_len, tq, tk, lam_init):
    qt_ref = refs[0]
    kv_refs = refs[1:1 + 2 * n_seg]
    lam_ref, sub_ref, o_ref, l_ref, acc_ref, sa_ref, sb_ref = refs[1 + 2 * n_seg:]
    first_map = lax.broadcasted_iota(jnp.int32, (HEAD_W, tq), 0) < QK_DIM
    lv = lam_ref[...]
    lam = (jnp.exp(jnp.sum(lv[0:1] * lv[1:2], axis=-1, keepdims=True))
           - jnp.exp(jnp.sum(lv[2:3] * lv[3:4], axis=-1, keepdims=True)) + lam_init)
    out_gain = sub_ref[...] * (1.0 - lam_init)

    chunks = []
    for i in range(n_seg):
        size = min(tk, seg_len[i])
        assert seg_len[i] % size == 0
        chunks += [(kv_refs[2 * i], kv_refs[2 * i + 1], c * size, size) for c in range(seg_len[i] // size)]
    bufs = (sa_ref, sb_ref)

    def query_tile(qi, carry):
        q0 = pl.multiple_of(qi * tq, tq)
        qt = qt_ref[:, pl.ds(q0, tq)]
        zero = jnp.zeros_like(qt)
        w = jnp.concatenate([jnp.where(first_map, qt, zero), jnp.where(first_map, zero, qt)], axis=1)
        l_ref[...] = jnp.zeros(l_ref.shape, F32)
        acc_ref[...] = jnp.zeros(acc_ref.shape, F32)

        def scores(s_ref, k_ref, start, size):
            s_ref[0:size, :] = jnp.dot(k_ref[pl.ds(start, size), :], w, preferred_element_type=F32)

        def consume(s_ref, vt_ref, start, size):
            e = jnp.exp2(s_ref[0:size, :])
            l_ref[...] += jnp.sum(e.reshape(size // 8, 8, 2 * tq), axis=0)
            acc_ref[...] += jnp.dot(vt_ref[:, pl.ds(start, size)], e.astype(BF16), preferred_element_type=F32)

        scores(bufs[0], chunks[0][0], chunks[0][2], chunks[0][3])
        for i, (_, vt_ref, start, size) in enumerate(chunks):
            if i + 1 < len(chunks):
                nxt = chunks[i + 1]
                scores(bufs[(i + 1) % 2], nxt[0], nxt[2], nxt[3])
            consume(bufs[i % 2], vt_ref, start, size)

        inv = 1.0 / jnp.sum(l_ref[...], axis=0, keepdims=True)
        acc = acc_ref[...]
        o = acc[:, :tq] * inv[:, :tq] - (lam * inv[:, tq:]) * acc[:, tq:]
        ms = jnp.mean(o * o, axis=0, keepdims=True)
        o = o * lax.rsqrt(ms + EPS) * out_gain
        o_ref[pl.ds(q0, tq), :] = o.T.astype(o_ref.dtype)
        return carry

    lax.fori_loop(0, qt_ref.shape[1] // tq, query_tile, 0)


def _attention(qt, segs, lam_vecs, subln_col, lam_init, *, tq, tk, shifted):
    b, _, lq = qt.shape
    tq = min(tq, lq)
    seg_len = tuple(k.shape[1] for k, _ in segs)
    const = lambda arr: pl.BlockSpec(arr.shape, lambda bi, h, *i: (0, 0))
    if shifted:
        tq = min(tq, SHIFTED_TQ)
        grid = (b, N_HEADS, lq // tq)
        q_spec = pl.BlockSpec((None, HEAD_W, tq), lambda bi, h, i: (bi, h, i))
        o_spec = pl.BlockSpec((None, tq, HEAD_W), lambda bi, h, i: (bi, i, h))
        body = functools.partial(_attn_kernel, n_seg=len(segs), seg_len=seg_len, tk=tk, lam_init=lam_init)
        scratch = [pltpu.VMEM((2, 1, tq), F32), pltpu.VMEM((2, 1, tq), F32), pltpu.VMEM((2, HEAD_W, tq), F32)]
        semantics = ("parallel", "parallel", "arbitrary")
    else:
        grid = (b, N_HEADS)
        q_spec = pl.BlockSpec((None, HEAD_W, lq), lambda bi, h: (bi, h, 0))
        o_spec = pl.BlockSpec((None, lq, HEAD_W), lambda bi, h: (bi, 0, h))
        body = functools.partial(_attn_unshifted_kernel, n_seg=len(segs), seg_len=seg_len, tq=tq, tk=tk,
                                 lam_init=lam_init)
        s_rows = min(tk, max(seg_len))
        scratch = [pltpu.VMEM((8, 2 * tq), F32), pltpu.VMEM((HEAD_W, 2 * tq), F32),
                   pltpu.VMEM((s_rows, 2 * tq), F32), pltpu.VMEM((s_rows, 2 * tq), F32)]
        semantics = ("parallel", "parallel")
    in_specs, args = [q_spec], [qt]
    for (k, vt), lk in zip(segs, seg_len):
        in_specs.append(pl.BlockSpec((None, lk, HEAD_W), lambda bi, h, *i: (bi, 0, h)))
        in_specs.append(pl.BlockSpec((None, HEAD_W, lk), lambda bi, h, *i: (bi, h, 0)))
        args += [k, vt]
    in_specs += [const(lam_vecs), const(subln_col)]
    args += [lam_vecs, subln_col]
    return pl.pallas_call(
        body,
        grid=grid,
        in_specs=in_specs,
        out_specs=o_spec,
        out_shape=jax.ShapeDtypeStruct((b, lq, ATTN_W), BF16),
        scratch_shapes=scratch,
        compiler_params=_cparams(*semantics),
        name="diff_attention_shifted" if shifted else "diff_attention",
    )(*args)


def _attention_dispatch(score_bound, *args, **kw):
    return lax.cond(score_bound <= MAX_UNSHIFTED_SCORE,
                    lambda: _attention(*args, shifted=False, **kw),
                    lambda: _attention(*args, shifted=True, **kw))


def _fourier_kernel(u_ref, v_ref, blk_ref, c2_ref, s2_ref, o_ref, yr_ref, yi_ref, *, n1, n2, per_step):
    step = pl.program_id(2)
    m = n1 * SUBLANES
    for t in range(per_step):
        jb = step * per_step + t
        cols = pl.ds(pl.multiple_of(jb * SUBLANES, SUBLANES), SUBLANES)
        xu = u_ref[:, cols, :].reshape(m, LANES)
        xv = v_ref[:, cols, :].reshape(m, LANES)
        x = jnp.concatenate([xu, xv], axis=1).astype(BF16)
        p = jnp.dot(blk_ref[t], x, preferred_element_type=F32)
        yr_ref[jb] = p[:m, :LANES] + p[m:, LANES:]
        yi_ref[jb] = p[m:, :LANES] - p[:m, LANES:]

    @pl.when(step == pl.num_programs(2) - 1)
    def _():
        c2, s2 = c2_ref[...], s2_ref[...]

        def stage2(k1, carry):
            tile = pl.ds(pl.multiple_of(k1 * SUBLANES, SUBLANES), SUBLANES)
            yr = yr_ref[:, tile, :].reshape(n2, LANES).astype(BF16)
            yi = yi_ref[:, tile, :].reshape(n2, LANES).astype(BF16)
            xr = jnp.dot(c2, yr, preferred_element_type=F32) + jnp.dot(s2, yi, preferred_element_type=F32)
            o_ref[pl.ds(k1, n2, stride=n1), :] = xr
            return carry

        lax.fori_loop(0, n1, stage2, 0, unroll=FOURIER_UNROLL)


def _fourier_latent(uv, blk, c2, s2, n1, n2):
    b, l, w = uv.shape
    halves = MIX_Q // LANES
    n_blk = n2 // SUBLANES
    per_step = min(4, n_blk)
    uv4 = uv.reshape(b, n1, n2, w)
    return pl.pallas_call(
        functools.partial(_fourier_kernel, n1=n1, n2=n2, per_step=per_step),
        grid=(b, halves, n_blk // per_step),
        in_specs=[
            pl.BlockSpec((None, n1, n2, LANES), lambda bi, c, s: (bi, 0, 0, c)),
            pl.BlockSpec((None, n1, n2, LANES), lambda bi, c, s: (bi, 0, 0, c + halves)),
            pl.BlockSpec((per_step,) + blk.shape[1:], lambda bi, c, s: (s, 0, 0)),
            pl.BlockSpec(c2.shape, lambda bi, c, s: (0, 0)),
            pl.BlockSpec(s2.shape, lambda bi, c, s: (0, 0)),
        ],
        out_specs=pl.BlockSpec((None, l, LANES), lambda bi, c, s: (bi, 0, c)),
        out_shape=jax.ShapeDtypeStruct((b, l, MIX_Q), F32),
        scratch_shapes=[pltpu.VMEM((n_blk, n1 * SUBLANES, LANES), F32),
                        pltpu.VMEM((n_blk, n1 * SUBLANES, LANES), F32)],
        compiler_params=_cparams("parallel", "parallel", "arbitrary"),
        name="fourier_positions",
    )(uv4, uv4, blk, c2, s2)


def _fourier_dense_kernel(uv_ref, c_ref, s_ref, o_ref):
    uv = uv_ref[...].astype(BF16)
    o_ref[...] = (jnp.dot(c_ref[...], uv[:, :MIX_Q], preferred_element_type=F32)
                  - jnp.dot(s_ref[...], uv[:, MIX_Q:], preferred_element_type=F32))


def _fourier_dense(uv):
    b, l, _ = uv.shape
    n = np.arange(l)
    ang = 2.0 * np.pi * np.outer(n, n) / l
    c = jnp.asarray((np.cos(ang) / math.sqrt(l)).astype(np.float32)).astype(BF16)
    s = jnp.asarray((np.sin(ang) / math.sqrt(l)).astype(np.float32)).astype(BF16)
    return pl.pallas_call(
        _fourier_dense_kernel,
        grid=(b,),
        in_specs=[pl.BlockSpec((None, l, 2 * MIX_Q), lambda bi: (bi, 0, 0)),
                  pl.BlockSpec((l, l), lambda bi: (0, 0)),
                  pl.BlockSpec((l, l), lambda bi: (0, 0))],
        out_specs=pl.BlockSpec((None, l, MIX_Q), lambda bi: (bi, 0, 0)),
        out_shape=jax.ShapeDtypeStruct((b, l, MIX_Q), F32),
        compiler_params=_cparams("parallel"),
        name="fourier_positions_ctx",
    )(uv, c, s)


POOL_HALO = 8


def _pool_kernel(b_ref, o_ref, pad_ref, *, l, rows):
    zeros = jnp.zeros((POOL_HALO, LANES), F32)
    lane = lax.broadcasted_iota(jnp.int32, (rows, LANES), 1)
    low_group = lane < GROUP_W
    n_chunks = l // rows
    for tile in range(MIX_Q // LANES):
        w_lo, w_hi = POOL_WINDOWS[2 * tile], POOL_WINDOWS[2 * tile + 1]
        pad_ref[0:POOL_HALO, :] = zeros
        pad_ref[POOL_HALO + l:, :] = zeros
        pad_ref[POOL_HALO:POOL_HALO + l, :] = b_ref[:, tile * LANES:(tile + 1) * LANES]

        def chunk(t0, edge, tile=tile, w_lo=w_lo, w_hi=w_hi):
            shifted = lambda dlt: pad_ref[pl.ds(t0 + POOL_HALO + dlt, rows), :]
            lo_range = range(-(w_lo // 2), w_lo // 2)
            inner = functools.reduce(lambda a, b: a + b, [shifted(d) for d in lo_range])
            outer = functools.reduce(lambda a, b: a + b,
                                     [shifted(d) for d in range(-(w_hi // 2), w_hi // 2) if d not in lo_range], inner)
            sums = jnp.where(low_group, inner, outer)
            if edge:
                pos = t0 + lax.broadcasted_iota(jnp.int32, (rows, LANES), 0)
                half = jnp.where(low_group, w_lo // 2, w_hi // 2)
                cnt = jnp.minimum(pos + (half - 1), l - 1) - jnp.maximum(pos - half, 0) + 1
                mean = sums / cnt.astype(F32)
            else:
                mean = sums * jnp.where(low_group, 1.0 / w_lo, 1.0 / w_hi)
            o_ref[pl.ds(t0, rows), tile * LANES:(tile + 1) * LANES] = mean - shifted(0)

        assert rows >= POOL_HALO
        chunk(0, True)
        if n_chunks > 1:
            def body(ci, carry, chunk=chunk):
                chunk(pl.multiple_of(ci * rows, rows), False)
                return carry

            lax.fori_loop(1, n_chunks - 1, body, 0)
            chunk((n_chunks - 1) * rows, True)


def _pool(bp):
    b, l, _ = bp.shape
    rows = 256
    return pl.pallas_call(
        functools.partial(_pool_kernel, l=l, rows=rows),
        grid=(b,),
        in_specs=[pl.BlockSpec((None, l, MIX_Q), lambda bi: (bi, 0, 0))],
        out_specs=pl.BlockSpec((None, l, MIX_Q), lambda bi: (bi, 0, 0)),
        out_shape=jax.ShapeDtypeStruct((b, l, MIX_Q), F32),
        scratch_shapes=[pltpu.VMEM((l + 2 * POOL_HALO, LANES), F32)],
        compiler_params=_cparams("parallel"),
        name="pool_windows",
    )(bp)


def _outproj_kernel(x_ref, f_ref, p_ref, a_ref, sg_ref, mod_ref, w_ref, o_ref, *, d, ctx_row):
    row = ctx_row if ctx_row is not None else pl.program_id(0)
    o_ref[...] = _residual_update(x_ref, f_ref, p_ref, a_ref, sg_ref, mod_ref, w_ref, row, d)


def _out_projection(x, four, pool, attn, sg, mod, w_out, *, layer, ctx_row, tl):
    b, l, d = x.shape
    row_blk = lambda width: pl.BlockSpec((None, tl, width), lambda bi, i: (bi, i, 0))
    of_layer = lambda arr: pl.BlockSpec((None,) + arr.shape[1:], lambda bi, i: (layer,) + (0,) * (arr.ndim - 1))
    return pl.pallas_call(
        functools.partial(_outproj_kernel, d=d, ctx_row=ctx_row),
        grid=(b, l // tl),
        in_specs=[row_blk(d), row_blk(MIX_Q), row_blk(MIX_Q), row_blk(ATTN_W), row_blk(d), of_layer(mod),
                  of_layer(w_out)],
        out_specs=row_blk(d),
        out_shape=jax.ShapeDtypeStruct((b, l, d), F32),
        compiler_params=_cparams("parallel", "parallel"),
        name="out_projection_ctx" if ctx_row is not None else "out_projection",
    )(x, four, pool, attn, sg, mod, w_out)


def kernel(x, c, ctx, c_ctx, norm_g, w_mod, b_mod, w_in, w_fourier, w_pool, pool_scale, qk_norm_g, lam_vecs,
           subln_g, w_out):
    b, l, d = x.shape
    depth = w_in.shape[0]
    ctx_len = ctx.shape[1]
    ctx_row = b

    cc = jnp.concatenate([c, c_ctx[None, :], jnp.zeros((8 - b - 1, d), F32)], axis=0)
    mod = _modulation(cc, w_mod, b_mod)
    w_eff = _fold_weights(w_in, w_fourier, w_pool, pool_scale)
    w_out_bf = _cast_bf16(w_out)
    rope_tabs = _rope_tables(l // GRID_W)
    lane_chunk = lax.broadcasted_iota(jnp.int32, (2 * LANES, 2 * LANES), 0) // QK_DIM
    ones_blk = (lane_chunk == lane_chunk.T).astype(BF16)
    n1 = 64
    n2 = l // n1
    blk, c2, s2 = _stage_tables(n1, n2)
    blk = _cast_bf16(jnp.asarray(blk))
    c2, s2 = jnp.asarray(c2).astype(BF16), jnp.asarray(s2).astype(BF16)

    pending = None
    for layer in range(depth):
        lam_init = 0.8 - 0.6 * math.exp(-0.3 * layer)
        update_ctx = layer < depth - 1
        ng = norm_g[layer][None, :]
        qk_gain = jnp.tile(qk_norm_g[layer], (1, 2 * LANES // QK_DIM))
        sub_g = subln_g[layer][:, None]
        lv = lam_vecs[layer]

        outs = _in_projection(x, mod, ng, w_eff, qk_gain, ones_blk, rope_tabs, layer=layer, ctx_row=None,
                              tl=512, prev=pending)
        if pending is not None:
            x, outs = outs[0], outs[1:]
        uv, bp, qt, k, vt, sg = outs
        uv_c, bp_c, qt_c, k_c, vt_c, sg_c = _in_projection(ctx, mod, ng, w_eff, qk_gain, ones_blk, None,
                                                           layer=layer, ctx_row=ctx_row, tl=ctx_len)
        gains = jnp.max(jnp.abs(qk_norm_g[layer]), axis=-1)
        score_bound = QK_DIM * gains[0] * gains[1] * (Q_SCALE * BF16_ROUNDING_SLACK)
        attn = _attention_dispatch(score_bound, qt, [(k_c, vt_c), (k, vt)], lv, sub_g, lam_init, tq=ATTN_TQ,
                                  tk=ATTN_TK)
        four = _fourier_latent(uv, blk, c2, s2, n1, n2)
        pool = _pool(bp)
        if update_ctx:
            attn_c = _attention_dispatch(score_bound, qt_c, [(k_c, vt_c)], lv, sub_g, lam_init, tq=ctx_len, tk=512)
            ctx = _out_projection(ctx, _fourier_dense(uv_c), _pool(bp_c), attn_c, sg_c, mod, w_out_bf,
                                  layer=layer, ctx_row=ctx_row, tl=ctx_len)
        if layer + 1 < depth:
            pending = (four, pool, attn, sg, w_out_bf)
        else:
            x = _out_projection(x, four, pool, attn, sg, mod, w_out_bf, layer=layer, ctx_row=None, tl=1024)
    return x
```

```python
import functools
import math

import numpy as np
import jax
import jax.numpy as jnp
from jax import lax
from jax.experimental import pallas as pl
from jax.experimental.pallas import tpu as pltpu

GRID_W = 64
FOURIER_GROUPS = 4
GROUP_W = 64
MIX_Q = 256
POOL_WINDOWS = (2, 4, 8, 16)
QK_DIM = 64
HEAD_W = 2 * QK_DIM
N_HEADS = 4
ATTN_W = N_HEADS * HEAD_W
ROPE_THETA = 10000.0
EPS = 1e-6
Q_SCALE = QK_DIM ** -0.5 * math.log2(math.e)
FOURIER_UNROLL = 8
ATTN_TK = 512
ATTN_TQ = 1024
SHIFTED_TQ = 512
MAX_UNSHIFTED_SCORE = 40.0
BF16_ROUNDING_SLACK = 1.02
U_OFF, V_OFF, P_OFF, Q_OFF, K_OFF, VV_OFF, G_OFF, W_EFF = 0, 256, 512, 768, 1280, 1792, 2304, 3328

LANES = 128
VMEM_LIMIT = 56 * 1024 * 1024
HI = lax.Precision.HIGHEST
F32 = jnp.float32
BF16 = jnp.bfloat16


def _cparams(*sem):
    return pltpu.CompilerParams(dimension_semantics=sem, vmem_limit_bytes=VMEM_LIMIT)


def _channel_dft():
    c = np.arange(GROUP_W)
    ang = 2.0 * np.pi * np.outer(c, c) / GROUP_W
    s = 1.0 / math.sqrt(GROUP_W)
    return np.cos(ang) * s, np.sin(ang) * s


def _stage_tables(n1, n2):
    n = n1 * n2
    k1 = np.arange(n1)
    i1 = np.arange(n1)
    j = np.arange(n2)
    theta = 2.0 * np.pi * (j[:, None, None] * k1[None, :, None] / n + k1[None, :, None] * i1[None, None, :] / n1)
    m1 = np.concatenate([np.cos(theta), -np.sin(theta)], axis=1) / math.sqrt(n)
    ang2 = 2.0 * np.pi * np.outer(j, j) / n2
    return m1.astype(np.float32), np.cos(ang2).astype(np.float32), np.sin(ang2).astype(np.float32)


def _rope_tables(rows):
    row = jnp.repeat(jnp.arange(rows), GRID_W).astype(F32)
    col = jnp.tile(jnp.arange(GRID_W), rows).astype(F32)
    half = QK_DIM // 2
    inv_freq = ROPE_THETA ** (-jnp.arange(0, half, 2, dtype=F32) / half)
    ang_r = row[:, None] * inv_freq[None, :]
    ang_c = col[:, None] * inv_freq[None, :]
    ang = jnp.concatenate([ang_r, ang_r, ang_c, ang_c], axis=-1)
    cos, sin = jnp.cos(ang), jnp.sin(ang)
    quarter = QK_DIM // 4
    first = (jnp.arange(QK_DIM) % half) < quarter
    sin_up = jnp.where(first, -sin, 0.0)
    sin_dn = jnp.where(first, 0.0, sin)
    rep = LANES // QK_DIM
    return tuple(jnp.tile(t, (1, rep)) for t in (cos, sin_up, sin_dn))


def _fold_kernel(w_ref, wf_ref, wp_ref, ps_ref, cd_ref, sd_ref, o_ref):
    cd, sd = cd_ref[...], sd_ref[...]
    for g in range(FOURIER_GROUPS):
        sl = slice(g * GROUP_W, (g + 1) * GROUP_W)
        wa = w_ref[:, sl]
        wf = wf_ref[g]
        cw = jnp.dot(cd, wf, precision=HI, preferred_element_type=F32)
        sw = jnp.dot(sd, wf, precision=HI, preferred_element_type=F32)
        o_ref[:, U_OFF + g * GROUP_W:U_OFF + (g + 1) * GROUP_W] = jnp.dot(
            wa, cw, precision=HI, preferred_element_type=F32).astype(o_ref.dtype)
        o_ref[:, V_OFF + g * GROUP_W:V_OFF + (g + 1) * GROUP_W] = jnp.dot(
            wa, sw, precision=HI, preferred_element_type=F32).astype(o_ref.dtype)
        wb = w_ref[:, MIX_Q + g * GROUP_W:MIX_Q + (g + 1) * GROUP_W]
        pw = jnp.dot(wb, wp_ref[g], precision=HI, preferred_element_type=F32) * ps_ref[:, sl]
        o_ref[:, P_OFF + g * GROUP_W:P_OFF + (g + 1) * GROUP_W] = pw.astype(o_ref.dtype)
    o_ref[:, Q_OFF:] = w_ref[:, 2 * MIX_Q:].astype(o_ref.dtype)


def _fold_weights(w_in, w_fourier, w_pool, pool_scale):
    depth, d, in_w = w_in.shape
    tr = 256
    cd, sd = _channel_dft()
    return pl.pallas_call(
        _fold_kernel,
        grid=(depth, d // tr),
        in_specs=[
            pl.BlockSpec((None, tr, in_w), lambda l, i: (l, i, 0)),
            pl.BlockSpec((None, FOURIER_GROUPS, GROUP_W, GROUP_W), lambda l, i: (l, 0, 0, 0)),
            pl.BlockSpec((None, FOURIER_GROUPS, GROUP_W, GROUP_W), lambda l, i: (l, 0, 0, 0)),
            pl.BlockSpec((None, 1, MIX_Q), lambda l, i: (l, 0, 0)),
            pl.BlockSpec((GROUP_W, GROUP_W), lambda l, i: (0, 0)),
            pl.BlockSpec((GROUP_W, GROUP_W), lambda l, i: (0, 0)),
        ],
        out_specs=pl.BlockSpec((None, tr, W_EFF), lambda l, i: (l, i, 0)),
        out_shape=jax.ShapeDtypeStruct((depth, d, W_EFF), BF16),
        compiler_params=_cparams("parallel", "parallel"),
        name="fold_weights",
    )(w_in, w_fourier, w_pool, pool_scale[:, None, :], jnp.asarray(cd, F32), jnp.asarray(sd, F32))


def _cast_kernel(w_ref, o_ref):
    o_ref[...] = w_ref[...].astype(o_ref.dtype)


def _cast_bf16(w):
    depth, r, c = w.shape
    return pl.pallas_call(
        _cast_kernel,
        grid=(depth,),
        in_specs=[pl.BlockSpec((None, r, c), lambda l: (l, 0, 0))],
        out_specs=pl.BlockSpec((None, r, c), lambda l: (l, 0, 0)),
        out_shape=jax.ShapeDtypeStruct(w.shape, BF16),
        compiler_params=_cparams("parallel"),
        name="cast_w_out",
    )(w)


def _mod_kernel(c_ref, w_ref, b_ref, o_ref):
    cc = c_ref[...]
    sc = cc * jax.nn.sigmoid(cc)
    o_ref[...] = jnp.dot(sc, w_ref[...], precision=HI, preferred_element_type=F32) + b_ref[...]


def _modulation(cc, w_mod, b_mod):
    depth, d, n = w_mod.shape
    tn = 1024
    rows = cc.shape[0]
    return pl.pallas_call(
        _mod_kernel,
        grid=(depth, n // tn),
        in_specs=[
            pl.BlockSpec((rows, d), lambda l, j: (0, 0)),
            pl.BlockSpec((None, d, tn), lambda l, j: (l, 0, j)),
            pl.BlockSpec((None, 1, tn), lambda l, j: (l, 0, j)),
        ],
        out_specs=pl.BlockSpec((None, rows, tn), lambda l, j: (l, 0, j)),
        out_shape=jax.ShapeDtypeStruct((depth, rows, n), F32),
        compiler_params=_cparams("parallel", "parallel"),
        name="modulation",
    )(cc, w_mod, b_mod[:, None, :])


def _qk_normalise(t, ones_blk, gain):
    ss = jnp.dot((t * t).astype(BF16), ones_blk, preferred_element_type=F32)
    return t * lax.rsqrt(ss * (1.0 / QK_DIM) + EPS) * gain


def _rotate(t, cos, sin_up, sin_dn):
    quarter = QK_DIM // 4
    up = pltpu.roll(t, LANES - quarter, 1)
    dn = pltpu.roll(t, quarter, 1)
    return t * cos + up * sin_up + dn * sin_dn


def _residual_update(x_ref, f_ref, p_ref, a_ref, sg_ref, mod_ref, w_ref, row, d, rows=slice(None)):
    gate = mod_ref[pl.ds(row, 1), 2 * d:]
    sg = sg_ref[rows, :].astype(F32)
    yf = (f_ref[rows, :] * sg[:, :MIX_Q]).astype(BF16)
    yp = (p_ref[rows, :] * sg[:, MIX_Q:2 * MIX_Q]).astype(BF16)
    ya = (a_ref[rows, :].astype(F32) * sg[:, 2 * MIX_Q:]).astype(BF16)
    acc = jnp.dot(yf, w_ref[0:MIX_Q, :], preferred_element_type=F32)
    acc += jnp.dot(yp, w_ref[MIX_Q:2 * MIX_Q, :], preferred_element_type=F32)
    acc += jnp.dot(ya, w_ref[2 * MIX_Q:, :], preferred_element_type=F32)
    return x_ref[rows, :] + gate * acc


def _inproj_kernel(x_ref, *rest, d, ctx_row, rope, fused):
    row = ctx_row if ctx_row is not None else pl.program_id(0)
    if fused:
        prev, rest = rest[:6], rest[6:]
    mod_ref, ng_ref, w_ref, qkg_ref, ones_ref = rest[:5]
    rest = rest[5:]
    if rope:
        cos_ref, sup_ref, sdn_ref = rest[:3]
        rest = rest[3:]
    mod = mod_ref[pl.ds(row, 1), :]
    shift, gain = mod[:, :d], ng_ref[...] * (1.0 + mod[:, d:2 * d])

    def modulated_norm(xf):
        ms = jnp.mean(xf * xf, axis=-1, keepdims=True)
        return (xf * lax.rsqrt(ms + EPS) * gain + shift).astype(BF16)

    if fused:
        xnew_ref, rest = rest[0], rest[1:]
    half = x_ref.shape[0] // 2
    h_parts = []
    for part in range(2):
        rows = slice(part * half, (part + 1) * half)
        if fused:
            xf = _residual_update(x_ref, *prev, row, d, rows=rows)
            xnew_ref[rows, :] = xf
        else:
            xf = x_ref[rows, :]
        h_parts.append(modulated_norm(xf))
    h = jnp.concatenate(h_parts, axis=0)
    uv_ref, bp_ref, qt_ref, k_ref, vt_ref, sg_ref = rest

    def proj(lo, hi, by_parts=False):
        if by_parts:
            return jnp.concatenate([jnp.dot(hp, w_ref[:, lo:hi], preferred_element_type=F32) for hp in h_parts],
                                   axis=0)
        return jnp.dot(h, w_ref[:, lo:hi], preferred_element_type=F32)

    pair = 2 * LANES
    n_pairs = ATTN_W // pair
    ones_blk = ones_ref[...]
    q_raw = [proj(Q_OFF + p * pair, Q_OFF + (p + 1) * pair, by_parts=True) for p in range(n_pairs)]
    k_raw = [proj(K_OFF + p * pair, K_OFF + (p + 1) * pair) for p in range(n_pairs)]
    q_nrm = [_qk_normalise(t, ones_blk, qkg_ref[0:1, :]) for t in q_raw]
    k_nrm = [_qk_normalise(t, ones_blk, qkg_ref[1:2, :]) for t in k_raw]
    v_raw = [proj(VV_OFF + p * pair, VV_OFF + (p + 1) * pair) for p in range(n_pairs)]
    g = proj(G_OFF, W_EFF)
    sg_ref[...] = (g * jax.nn.sigmoid(g)).astype(sg_ref.dtype)
    uv_ref[...] = proj(U_OFF, P_OFF)
    bp_ref[...] = proj(P_OFF, Q_OFF)

    if rope:
        cos, sup, sdn = cos_ref[...], sup_ref[...], sdn_ref[...]
    for p in range(n_pairs):
        qp, kp, vp = q_nrm[p], k_nrm[p], v_raw[p]
        for half in range(2):
            sl = slice(p * pair + half * LANES, p * pair + (half + 1) * LANES)
            hs = slice(half * LANES, (half + 1) * LANES)
            qt, kt = qp[:, hs], kp[:, hs]
            if rope:
                qt = _rotate(qt, cos, sup, sdn)
                kt = _rotate(kt, cos, sup, sdn)
            qt_ref[sl, :] = (qt * Q_SCALE).T.astype(qt_ref.dtype)
            k_ref[:, sl] = kt.astype(k_ref.dtype)
            vt_ref[sl, :] = vp[:, hs].T.astype(vt_ref.dtype)


def _in_projection(x, mod, norm_g, w_eff, qk_gain, ones_blk, rope_tabs, *, layer, ctx_row, tl, prev=None):
    b, l, d = x.shape
    rope = rope_tabs is not None
    fused = prev is not None
    row_blk = lambda width: pl.BlockSpec((None, tl, width), lambda bi, i: (bi, i, 0))
    full = lambda arr: pl.BlockSpec(arr.shape, lambda bi, i: (0,) * arr.ndim)

    def of_layer(arr, which=layer):
        return pl.BlockSpec((None,) + arr.shape[1:], lambda bi, i: (which,) + (0,) * (arr.ndim - 1))

    in_specs, args = [row_blk(d)], [x]
    if fused:
        four, pool, attn, sg_prev, w_out = prev
        in_specs += [row_blk(MIX_Q), row_blk(MIX_Q), row_blk(ATTN_W), row_blk(d), of_layer(mod, layer - 1),
                     of_layer(w_out, layer - 1)]
        args += [four, pool, attn, sg_prev, mod, w_out]
    in_specs += [of_layer(mod), full(norm_g), of_layer(w_eff), full(qk_gain), full(ones_blk)]
    args += [mod, norm_g, w_eff, qk_gain, ones_blk]
    if rope:
        in_specs += [pl.BlockSpec((tl, LANES), lambda bi, i: (i, 0))] * 3
        args += list(rope_tabs)
    out_shape = (
        jax.ShapeDtypeStruct((b, l, 2 * MIX_Q), F32),
        jax.ShapeDtypeStruct((b, l, MIX_Q), F32),
        jax.ShapeDtypeStruct((b, ATTN_W, l), BF16),
        jax.ShapeDtypeStruct((b, l, ATTN_W), BF16),
        jax.ShapeDtypeStruct((b, ATTN_W, l), BF16),
        jax.ShapeDtypeStruct((b, l, d), BF16),
    )
    col_blk = pl.BlockSpec((None, ATTN_W, tl), lambda bi, i: (bi, 0, i))
    out_specs = (row_blk(2 * MIX_Q), row_blk(MIX_Q), col_blk, row_blk(ATTN_W), col_blk, row_blk(d))
    if fused:
        out_shape = (jax.ShapeDtypeStruct((b, l, d), F32),) + out_shape
        out_specs = (row_blk(d),) + out_specs
    name = "in_projection_ctx" if ctx_row is not None else "in_projection"
    return pl.pallas_call(
        functools.partial(_inproj_kernel, d=d, ctx_row=ctx_row, rope=rope, fused=fused),
        grid=(b, l // tl),
        in_specs=in_specs,
        out_specs=out_specs,
        out_shape=out_shape,
        compiler_params=_cparams("parallel", "parallel"),
        name="residual_" + name if fused else name,
    )(*args)


def _attn_kernel(*refs, n_seg, seg_len, tk, lam_init):
    qt_ref = refs[0]
    kv_refs = refs[1:1 + 2 * n_seg]
    lam_ref, sub_ref, o_ref, m_ref, l_ref, acc_ref = refs[1 + 2 * n_seg:]
    qt = qt_ref[...]
    first_map = lax.broadcasted_iota(jnp.int32, qt.shape, 0) < QK_DIM
    zero = jnp.zeros_like(qt)
    ws = (jnp.where(first_map, qt, zero), jnp.where(first_map, zero, qt))

    m_ref[...] = jnp.full(m_ref.shape, -jnp.inf, F32)
    l_ref[...] = jnp.zeros(l_ref.shape, F32)
    acc_ref[...] = jnp.zeros(acc_ref.shape, F32)

    def chunk(k_ref, vt_ref, start, size):
        k = k_ref[pl.ds(start, size), :]
        vt = vt_ref[:, pl.ds(start, size)]
        for mp in range(2):
            s = jnp.dot(k, ws[mp], preferred_element_type=F32)
            m_old = m_ref[mp]
            m_new = jnp.maximum(m_old, jnp.max(s, axis=0, keepdims=True))
            alpha = jnp.exp2(m_old - m_new)
            e = jnp.exp2(s - m_new)
            l_ref[mp] = alpha * l_ref[mp] + jnp.sum(e, axis=0, keepdims=True)
            acc_ref[mp] = alpha * acc_ref[mp] + jnp.dot(vt, e.astype(BF16), preferred_element_type=F32)
            m_ref[mp] = m_new

    for si in range(n_seg):
        k_ref, vt_ref = kv_refs[2 * si], kv_refs[2 * si + 1]
        length = seg_len[si]
        size = min(tk, length)
        n_chunks = length // size
        if n_chunks == 1:
            chunk(k_ref, vt_ref, 0, size)
        else:
            def body(j, carry, k_ref=k_ref, vt_ref=vt_ref, size=size):
                chunk(k_ref, vt_ref, pl.multiple_of(j * size, size), size)
                return carry
            lax.fori_loop(0, n_chunks, body, 0)

    lv = lam_ref[...]
    lam = (jnp.exp(jnp.sum(lv[0:1] * lv[1:2], axis=-1, keepdims=True))
           - jnp.exp(jnp.sum(lv[2:3] * lv[3:4], axis=-1, keepdims=True)) + lam_init)
    o = acc_ref[0] / l_ref[0] - lam * (acc_ref[1] / l_ref[1])
    ms = jnp.mean(o * o, axis=0, keepdims=True)
    o = o * lax.rsqrt(ms + EPS) * (sub_ref[...] * (1.0 - lam_init))
    o_ref[...] = o.T.astype(o_ref.dtype)


def _attn_unshifted_kernel(*refs, n_seg, seg_len, tq, tk, lam_init):
    qt_ref = refs[0]
    kv_refs = refs[1:1 + 2 * n_seg]
    lam_ref, sub_ref, o_ref, l_ref, acc_ref, sa_ref, sb_ref = refs[1 + 2 * n_seg:]
    first_map = lax.broadcasted_iota(jnp.int32, (HEAD_W, tq), 0) < QK_DIM
    lv = lam_ref[...]
    lam = (jnp.exp(jnp.sum(lv[0:1] * lv[1:2], axis=-1, keepdims=True))
           - jnp.exp(jnp.sum(lv[2:3] * lv[3:4], axis=-1, keepdims=True)) + lam_init)
    out_gain = sub_ref[...] * (1.0 - lam_init)

    chunks = []
    for i in range(n_seg):
        size = min(tk, seg_len[i])
        assert seg_len[i] % size == 0
        chunks += [(kv_refs[2 * i], kv_refs[2 * i + 1], c * size, size) for c in range(seg_len[i] // size)]
    bufs = (sa_ref, sb_ref)

    def finish_tile(qi):
        inv = 1.0 / jnp.sum(l_ref[...], axis=0, keepdims=True)
        acc = acc_ref[...]
        o = acc[:, :tq] * inv[:, :tq] - (lam * inv[:, tq:]) * acc[:, tq:]
        ms = jnp.mean(o * o, axis=0, keepdims=True)
        o = o * lax.rsqrt(ms + EPS) * out_gain
        o_ref[pl.ds(pl.multiple_of(qi * tq, tq), tq), :] = o.T.astype(o_ref.dtype)

    def query_tile(qi, carry):
        qt = qt_ref[:, pl.ds(pl.multiple_of(qi * tq, tq), tq)]
        zero = jnp.zeros_like(qt)
        w = jnp.concatenate([jnp.where(first_map, qt, zero), jnp.where(first_map, zero, qt)], axis=1)

        def scores(s_ref, k_ref, start, size):
            s_ref[0:size, :] = jnp.dot(k_ref[pl.ds(start, size), :], w, preferred_element_type=F32)

        def consume(s_ref, vt_ref, start, size):
            e = jnp.exp2(s_ref[0:size, :])
            l_ref[...] += jnp.sum(e.reshape(size // 8, 8, 2 * tq), axis=0)
            acc_ref[...] += jnp.dot(vt_ref[:, pl.ds(start, size)], e.astype(BF16), preferred_element_type=F32)

        scores(bufs[0], chunks[0][0], chunks[0][2], chunks[0][3])
        finish_tile(jnp.maximum(qi - 1, 0))
        l_ref[...] = jnp.zeros(l_ref.shape, F32)
        acc_ref[...] = jnp.zeros(acc_ref.shape, F32)
        for i, (_, vt_ref, start, size) in enumerate(chunks):
            if i + 1 < len(chunks):
                nxt = chunks[i + 1]
                scores(bufs[(i + 1) % 2], nxt[0], nxt[2], nxt[3])
            consume(bufs[i % 2], vt_ref, start, size)
        return carry

    n_tiles = qt_ref.shape[1] // tq
    l_ref[...] = jnp.ones(l_ref.shape, F32)
    acc_ref[...] = jnp.zeros(acc_ref.shape, F32)
    lax.fori_loop(0, n_tiles, query_tile, 0)
    finish_tile(n_tiles - 1)


def _attention(qt, segs, lam_vecs, subln_col, lam_init, *, tq, tk, shifted):
    b, _, lq = qt.shape
    tq = min(tq, lq)
    seg_len = tuple(k.shape[1] for k, _ in segs)
    const = lambda arr: pl.BlockSpec(arr.shape, lambda bi, h, *i: (0, 0))
    if shifted:
        tq = min(tq, SHIFTED_TQ)
        grid = (b, N_HEADS, lq // tq)
        q_spec = pl.BlockSpec((None, HEAD_W, tq), lambda bi, h, i: (bi, h, i))
        o_spec = pl.BlockSpec((None, tq, HEAD_W), lambda bi, h, i: (bi, i, h))
        body = functools.partial(_attn_kernel, n_seg=len(segs), seg_len=seg_len, tk=tk, lam_init=lam_init)
        scratch = [pltpu.VMEM((2, 1, tq), F32), pltpu.VMEM((2, 1, tq), F32), pltpu.VMEM((2, HEAD_W, tq), F32)]
        semantics = ("parallel", "parallel", "arbitrary")
    else:
        grid = (b, N_HEADS)
        q_spec = pl.BlockSpec((None, HEAD_W, lq), lambda bi, h: (bi, h, 0))
        o_spec = pl.BlockSpec((None, lq, HEAD_W), lambda bi, h: (bi, 0, h))
        body = functools.partial(_attn_unshifted_kernel, n_seg=len(segs), seg_len=seg_len, tq=tq, tk=tk,
                                 lam_init=lam_init)
        s_rows = min(tk, max(seg_len))
        scratch = [pltpu.VMEM((8, 2 * tq), F32), pltpu.VMEM((HEAD_W, 2 * tq), F32),
                   pltpu.VMEM((s_rows, 2 * tq), F32), pltpu.VMEM((s_rows, 2 * tq), F32)]
        semantics = ("parallel", "parallel")
    in_specs, args = [q_spec], [qt]
    for (k, vt), lk in zip(segs, seg_len):
        in_specs.append(pl.BlockSpec((None, lk, HEAD_W), lambda bi, h, *i: (bi, 0, h)))
        in_specs.append(pl.BlockSpec((None, HEAD_W, lk), lambda bi, h, *i: (bi, h, 0)))
        args += [k, vt]
    in_specs += [const(lam_vecs), const(subln_col)]
    args += [lam_vecs, subln_col]
    return pl.pallas_call(
        body,
        grid=grid,
        in_specs=in_specs,
        out_specs=o_spec,
        out_shape=jax.ShapeDtypeStruct((b, lq, ATTN_W), BF16),
        scratch_shapes=scratch,
        compiler_params=_cparams(*semantics),
        name="diff_attention_shifted" if shifted else "diff_attention",
    )(*args)


def _attention_dispatch(score_bound, *args, **kw):
    return lax.cond(score_bound <= MAX_UNSHIFTED_SCORE,
                    lambda: _attention(*args, shifted=False, **kw),
                    lambda: _attention(*args, shifted=True, **kw))


def _fourier_kernel(u_ref, v_ref, m1_ref, c2_ref, s2_ref, o_ref, yr_ref, yi_ref, *, n1, n2):
    def stage1(j, carry):
        xu = u_ref[pl.ds(j, n1, stride=n2), :]
        xv = v_ref[pl.ds(j, n1, stride=n2), :]
        x = jnp.concatenate([xu, xv], axis=1).astype(BF16)
        p = jnp.dot(m1_ref[j], x, preferred_element_type=F32)
        rows = pl.ds(pl.multiple_of(j * n1, n1), n1)
        yr_ref[rows, :] = p[:n1, :LANES] + p[n1:, LANES:]
        yi_ref[rows, :] = p[n1:, :LANES] - p[:n1, LANES:]
        return carry

    lax.fori_loop(0, n2, stage1, 0, unroll=FOURIER_UNROLL)
    c2, s2 = c2_ref[...], s2_ref[...]

    def stage2(k1, carry):
        rows = pl.ds(k1, n2, stride=n1)
        xr = (jnp.dot(c2, yr_ref[rows, :].astype(BF16), preferred_element_type=F32)
              + jnp.dot(s2, yi_ref[rows, :].astype(BF16), preferred_element_type=F32))
        o_ref[rows, :] = xr
        return carry

    lax.fori_loop(0, n1, stage2, 0, unroll=FOURIER_UNROLL)


def _fourier_latent(uv, n1, n2):
    b, l, _ = uv.shape
    m1, c2, s2 = _stage_tables(n1, n2)
    m1, c2, s2 = (jnp.asarray(t).astype(BF16) for t in (m1, c2, s2))
    halves = MIX_Q // LANES
    return pl.pallas_call(
        functools.partial(_fourier_kernel, n1=n1, n2=n2),
        grid=(b, halves),
        in_specs=[
            pl.BlockSpec((None, l, LANES), lambda bi, c: (bi, 0, c)),
            pl.BlockSpec((None, l, LANES), lambda bi, c: (bi, 0, c + halves)),
            pl.BlockSpec(m1.shape, lambda bi, c: (0, 0, 0)),
            pl.BlockSpec(c2.shape, lambda bi, c: (0, 0)),
            pl.BlockSpec(s2.shape, lambda bi, c: (0, 0)),
        ],
        out_specs=pl.BlockSpec((None, l, LANES), lambda bi, c: (bi, 0, c)),
        out_shape=jax.ShapeDtypeStruct((b, l, MIX_Q), F32),
        scratch_shapes=[pltpu.VMEM((l, LANES), F32), pltpu.VMEM((l, LANES), F32)],
        compiler_params=_cparams("parallel", "parallel"),
        name="fourier_positions",
    )(uv, uv, m1, c2, s2)


def _fourier_dense_kernel(uv_ref, c_ref, s_ref, o_ref):
    uv = uv_ref[...].astype(BF16)
    o_ref[...] = (jnp.dot(c_ref[...], uv[:, :MIX_Q], preferred_element_type=F32)
                  - jnp.dot(s_ref[...], uv[:, MIX_Q:], preferred_element_type=F32))


def _fourier_dense(uv):
    b, l, _ = uv.shape
    n = np.arange(l)
    ang = 2.0 * np.pi * np.outer(n, n) / l
    c = jnp.asarray((np.cos(ang) / math.sqrt(l)).astype(np.float32)).astype(BF16)
    s = jnp.asarray((np.sin(ang) / math.sqrt(l)).astype(np.float32)).astype(BF16)
    return pl.pallas_call(
        _fourier_dense_kernel,
        grid=(b,),
        in_specs=[pl.BlockSpec((None, l, 2 * MIX_Q), lambda bi: (bi, 0, 0)),
                  pl.BlockSpec((l, l), lambda bi: (0, 0)),
                  pl.BlockSpec((l, l), lambda bi: (0, 0))],
        out_specs=pl.BlockSpec((None, l, MIX_Q), lambda bi: (bi, 0, 0)),
        out_shape=jax.ShapeDtypeStruct((b, l, MIX_Q), F32),
        compiler_params=_cparams("parallel"),
        name="fourier_positions_ctx",
    )(uv, c, s)


POOL_HALO = 8


def _pool_kernel(b_ref, o_ref, pad_ref, *, l, rows):
    zeros = jnp.zeros((POOL_HALO, LANES), F32)
    lane = lax.broadcasted_iota(jnp.int32, (rows, LANES), 1)
    low_group = lane < GROUP_W
    n_chunks = l // rows
    for tile in range(MIX_Q // LANES):
        w_lo, w_hi = POOL_WINDOWS[2 * tile], POOL_WINDOWS[2 * tile + 1]
        pad_ref[0:POOL_HALO, :] = zeros
        pad_ref[POOL_HALO + l:, :] = zeros
        pad_ref[POOL_HALO:POOL_HALO + l, :] = b_ref[:, tile * LANES:(tile + 1) * LANES]

        def chunk(t0, edge, tile=tile, w_lo=w_lo, w_hi=w_hi):
            shifted = lambda dlt: pad_ref[pl.ds(t0 + POOL_HALO + dlt, rows), :]
            lo_range = range(-(w_lo // 2), w_lo // 2)
            inner = functools.reduce(lambda a, b: a + b, [shifted(d) for d in lo_range])
            outer = functools.reduce(lambda a, b: a + b,
                                     [shifted(d) for d in range(-(w_hi // 2), w_hi // 2) if d not in lo_range], inner)
            sums = jnp.where(low_group, inner, outer)
            if edge:
                pos = t0 + lax.broadcasted_iota(jnp.int32, (rows, LANES), 0)
                half = jnp.where(low_group, w_lo // 2, w_hi // 2)
                cnt = jnp.minimum(pos + (half - 1), l - 1) - jnp.maximum(pos - half, 0) + 1
                mean = sums / cnt.astype(F32)
            else:
                mean = sums * jnp.where(low_group, 1.0 / w_lo, 1.0 / w_hi)
            o_ref[pl.ds(t0, rows), tile * LANES:(tile + 1) * LANES] = mean - shifted(0)

        assert rows >= POOL_HALO
        chunk(0, True)
        if n_chunks > 1:
            def body(ci, carry, chunk=chunk):
                chunk(pl.multiple_of(ci * rows, rows), False)
                return carry

            lax.fori_loop(1, n_chunks - 1, body, 0)
            chunk((n_chunks - 1) * rows, True)


def _pool(bp):
    b, l, _ = bp.shape
    rows = 256
    return pl.pallas_call(
        functools.partial(_pool_kernel, l=l, rows=rows),
        grid=(b,),
        in_specs=[pl.BlockSpec((None, l, MIX_Q), lambda bi: (bi, 0, 0))],
        out_specs=pl.BlockSpec((None, l, MIX_Q), lambda bi: (bi, 0, 0)),
        out_shape=jax.ShapeDtypeStruct((b, l, MIX_Q), F32),
        scratch_shapes=[pltpu.VMEM((l + 2 * POOL_HALO, LANES), F32)],
        compiler_params=_cparams("parallel"),
        name="pool_windows",
    )(bp)


def _outproj_kernel(x_ref, f_ref, p_ref, a_ref, sg_ref, mod_ref, w_ref, o_ref, *, d, ctx_row):
    row = ctx_row if ctx_row is not None else pl.program_id(0)
    o_ref[...] = _residual_update(x_ref, f_ref, p_ref, a_ref, sg_ref, mod_ref, w_ref, row, d)


def _out_projection(x, four, pool, attn, sg, mod, w_out, *, layer, ctx_row, tl):
    b, l, d = x.shape
    row_blk = lambda width: pl.BlockSpec((None, tl, width), lambda bi, i: (bi, i, 0))
    of_layer = lambda arr: pl.BlockSpec((None,) + arr.shape[1:], lambda bi, i: (layer,) + (0,) * (arr.ndim - 1))
    return pl.pallas_call(
        functools.partial(_outproj_kernel, d=d, ctx_row=ctx_row),
        grid=(b, l // tl),
        in_specs=[row_blk(d), row_blk(MIX_Q), row_blk(MIX_Q), row_blk(ATTN_W), row_blk(d), of_layer(mod),
                  of_layer(w_out)],
        out_specs=row_blk(d),
        out_shape=jax.ShapeDtypeStruct((b, l, d), F32),
        compiler_params=_cparams("parallel", "parallel"),
        name="out_projection_ctx" if ctx_row is not None else "out_projection",
    )(x, four, pool, attn, sg, mod, w_out)


def kernel(x, c, ctx, c_ctx, norm_g, w_mod, b_mod, w_in, w_fourier, w_pool, pool_scale, qk_norm_g, lam_vecs,
           subln_g, w_out):
    b, l, d = x.shape
    depth = w_in.shape[0]
    ctx_len = ctx.shape[1]
    ctx_row = b

    cc = jnp.concatenate([c, c_ctx[None, :], jnp.zeros((8 - b - 1, d), F32)], axis=0)
    mod = _modulation(cc, w_mod, b_mod)
    w_eff = _fold_weights(w_in, w_fourier, w_pool, pool_scale)
    w_out_bf = _cast_bf16(w_out)
    rope_tabs = _rope_tables(l // GRID_W)
    lane_chunk = lax.broadcasted_iota(jnp.int32, (2 * LANES, 2 * LANES), 0) // QK_DIM
    ones_blk = (lane_chunk == lane_chunk.T).astype(BF16)
    n1 = 64
    n2 = l // n1

    pending = None
    for layer in range(depth):
        lam_init = 0.8 - 0.6 * math.exp(-0.3 * layer)
        update_ctx = layer < depth - 1
        ng = norm_g[layer][None, :]
        qk_gain = jnp.tile(qk_norm_g[layer], (1, 2 * LANES // QK_DIM))
        sub_g = subln_g[layer][:, None]
        lv = lam_vecs[layer]

        outs = _in_projection(x, mod, ng, w_eff, qk_gain, ones_blk, rope_tabs, layer=layer, ctx_row=None,
                              tl=512, prev=pending)
        if pending is not None:
            x, outs = outs[0], outs[1:]
        uv, bp, qt, k, vt, sg = outs
        uv_c, bp_c, qt_c, k_c, vt_c, sg_c = _in_projection(ctx, mod, ng, w_eff, qk_gain, ones_blk, None,
                                                           layer=layer, ctx_row=ctx_row, tl=ctx_len)
        gains = jnp.max(jnp.abs(qk_norm_g[layer]), axis=-1)
        score_bound = QK_DIM * gains[0] * gains[1] * (Q_SCALE * BF16_ROUNDING_SLACK)
        attn = _attention_dispatch(score_bound, qt, [(k_c, vt_c), (k, vt)], lv, sub_g, lam_init, tq=ATTN_TQ,
                                  tk=ATTN_TK)
        four = _fourier_latent(uv, n1, n2)
        pool = _pool(bp)
        if update_ctx:
            attn_c = _attention_dispatch(score_bound, qt_c, [(k_c, vt_c)], lv, sub_g, lam_init, tq=ctx_len, tk=512)
            ctx = _out_projection(ctx, _fourier_dense(uv_c), _pool(bp_c), attn_c, sg_c, mod, w_out_bf,
                                  layer=layer, ctx_row=ctx_row, tl=ctx_len)
        if layer + 1 < depth:
            pending = (four, pool, attn, sg, w_out_bf)
        else:
            x = _out_projection(x, four, pool, attn, sg, mod, w_out_bf, layer=layer, ctx_row=None, tl=1024)
    return x
```

```python
import functools
import math

import numpy as np
import jax
import jax.numpy as jnp
from jax import lax
from jax.experimental import pallas as pl
from jax.experimental.pallas import tpu as pltpu

GRID_W = 64
FOURIER_GROUPS = 4
GROUP_W = 64
MIX_Q = 256
POOL_WINDOWS = (2, 4, 8, 16)
QK_DIM = 64
HEAD_W = 2 * QK_DIM
N_HEADS = 4
ATTN_W = N_HEADS * HEAD_W
ROPE_THETA = 10000.0
EPS = 1e-6
Q_SCALE = QK_DIM ** -0.5 * math.log2(math.e)
FOURIER_UNROLL = 8
ATTN_TK = 512
ATTN_TQ = 2048
SHIFTED_TQ = 512
MAX_UNSHIFTED_SCORE = 40.0
BF16_ROUNDING_SLACK = 1.02
U_OFF, V_OFF, P_OFF, Q_OFF, K_OFF, VV_OFF, G_OFF, W_EFF = 0, 256, 512, 768, 1280, 1792, 2304, 3328

LANES = 128
VMEM_LIMIT = 56 * 1024 * 1024
HI = lax.Precision.HIGHEST
F32 = jnp.float32
BF16 = jnp.bfloat16


def _cparams(*sem):
    return pltpu.CompilerParams(dimension_semantics=sem, vmem_limit_bytes=VMEM_LIMIT)


def _channel_dft():
    c = np.arange(GROUP_W)
    ang = 2.0 * np.pi * np.outer(c, c) / GROUP_W
    s = 1.0 / math.sqrt(GROUP_W)
    return np.cos(ang) * s, np.sin(ang) * s


def _stage_tables(n1, n2):
    n = n1 * n2
    k1 = np.arange(n1)
    i1 = np.arange(n1)
    j = np.arange(n2)
    theta = 2.0 * np.pi * (j[:, None, None] * k1[None, :, None] / n + k1[None, :, None] * i1[None, None, :] / n1)
    m1 = np.concatenate([np.cos(theta), -np.sin(theta)], axis=1) / math.sqrt(n)
    ang2 = 2.0 * np.pi * np.outer(j, j) / n2
    return m1.astype(np.float32), np.cos(ang2).astype(np.float32), np.sin(ang2).astype(np.float32)


def _rope_tables(rows):
    row = jnp.repeat(jnp.arange(rows), GRID_W).astype(F32)
    col = jnp.tile(jnp.arange(GRID_W), rows).astype(F32)
    half = QK_DIM // 2
    inv_freq = ROPE_THETA ** (-jnp.arange(0, half, 2, dtype=F32) / half)
    ang_r = row[:, None] * inv_freq[None, :]
    ang_c = col[:, None] * inv_freq[None, :]
    ang = jnp.concatenate([ang_r, ang_r, ang_c, ang_c], axis=-1)
    cos, sin = jnp.cos(ang), jnp.sin(ang)
    quarter = QK_DIM // 4
    first = (jnp.arange(QK_DIM) % half) < quarter
    sin_up = jnp.where(first, -sin, 0.0)
    sin_dn = jnp.where(first, 0.0, sin)
    rep = LANES // QK_DIM
    return tuple(jnp.tile(t, (1, rep)) for t in (cos, sin_up, sin_dn))


def _fold_kernel(w_ref, wf_ref, wp_ref, ps_ref, cd_ref, sd_ref, o_ref):
    cd, sd = cd_ref[...], sd_ref[...]
    for g in range(FOURIER_GROUPS):
        sl = slice(g * GROUP_W, (g + 1) * GROUP_W)
        wa = w_ref[:, sl]
        wf = wf_ref[g]
        cw = jnp.dot(cd, wf, precision=HI, preferred_element_type=F32)
        sw = jnp.dot(sd, wf, precision=HI, preferred_element_type=F32)
        o_ref[:, U_OFF + g * GROUP_W:U_OFF + (g + 1) * GROUP_W] = jnp.dot(
            wa, cw, precision=HI, preferred_element_type=F32).astype(o_ref.dtype)
        o_ref[:, V_OFF + g * GROUP_W:V_OFF + (g + 1) * GROUP_W] = jnp.dot(
            wa, sw, precision=HI, preferred_element_type=F32).astype(o_ref.dtype)
        wb = w_ref[:, MIX_Q + g * GROUP_W:MIX_Q + (g + 1) * GROUP_W]
        pw = jnp.dot(wb, wp_ref[g], precision=HI, preferred_element_type=F32) * ps_ref[:, sl]
        o_ref[:, P_OFF + g * GROUP_W:P_OFF + (g + 1) * GROUP_W] = pw.astype(o_ref.dtype)
    o_ref[:, Q_OFF:] = w_ref[:, 2 * MIX_Q:].astype(o_ref.dtype)


def _fold_weights(w_in, w_fourier, w_pool, pool_scale):
    depth, d, in_w = w_in.shape
    tr = 256
    cd, sd = _channel_dft()
    return pl.pallas_call(
        _fold_kernel,
        grid=(depth, d // tr),
        in_specs=[
            pl.BlockSpec((None, tr, in_w), lambda l, i: (l, i, 0)),
            pl.BlockSpec((None, FOURIER_GROUPS, GROUP_W, GROUP_W), lambda l, i: (l, 0, 0, 0)),
            pl.BlockSpec((None, FOURIER_GROUPS, GROUP_W, GROUP_W), lambda l, i: (l, 0, 0, 0)),
            pl.BlockSpec((None, 1, MIX_Q), lambda l, i: (l, 0, 0)),
            pl.BlockSpec((GROUP_W, GROUP_W), lambda l, i: (0, 0)),
            pl.BlockSpec((GROUP_W, GROUP_W), lambda l, i: (0, 0)),
        ],
        out_specs=pl.BlockSpec((None, tr, W_EFF), lambda l, i: (l, i, 0)),
        out_shape=jax.ShapeDtypeStruct((depth, d, W_EFF), BF16),
        compiler_params=_cparams("parallel", "parallel"),
        name="fold_weights",
    )(w_in, w_fourier, w_pool, pool_scale[:, None, :], jnp.asarray(cd, F32), jnp.asarray(sd, F32))


def _cast_kernel(w_ref, o_ref):
    o_ref[...] = w_ref[...].astype(o_ref.dtype)


def _cast_bf16(w):
    depth, r, c = w.shape
    return pl.pallas_call(
        _cast_kernel,
        grid=(depth,),
        in_specs=[pl.BlockSpec((None, r, c), lambda l: (l, 0, 0))],
        out_specs=pl.BlockSpec((None, r, c), lambda l: (l, 0, 0)),
        out_shape=jax.ShapeDtypeStruct(w.shape, BF16),
        compiler_params=_cparams("parallel"),
        name="cast_w_out",
    )(w)


def _mod_kernel(c_ref, w_ref, b_ref, o_ref):
    cc = c_ref[...]
    sc = cc * jax.nn.sigmoid(cc)
    o_ref[...] = jnp.dot(sc, w_ref[...], precision=HI, preferred_element_type=F32) + b_ref[...]


def _modulation(cc, w_mod, b_mod):
    depth, d, n = w_mod.shape
    tn = 1024
    rows = cc.shape[0]
    return pl.pallas_call(
        _mod_kernel,
        grid=(depth, n // tn),
        in_specs=[
            pl.BlockSpec((rows, d), lambda l, j: (0, 0)),
            pl.BlockSpec((None, d, tn), lambda l, j: (l, 0, j)),
            pl.BlockSpec((None, 1, tn), lambda l, j: (l, 0, j)),
        ],
        out_specs=pl.BlockSpec((None, rows, tn), lambda l, j: (l, 0, j)),
        out_shape=jax.ShapeDtypeStruct((depth, rows, n), F32),
        compiler_params=_cparams("parallel", "parallel"),
        name="modulation",
    )(cc, w_mod, b_mod[:, None, :])


def _qk_normalise(t, ones_blk, gain):
    ss = jnp.dot((t * t).astype(BF16), ones_blk, preferred_element_type=F32)
    return t * lax.rsqrt(ss * (1.0 / QK_DIM) + EPS) * gain


def _rotate(t, cos, sin_up, sin_dn):
    quarter = QK_DIM // 4
    up = pltpu.roll(t, LANES - quarter, 1)
    dn = pltpu.roll(t, quarter, 1)
    return t * cos + up * sin_up + dn * sin_dn


def _residual_update(x_ref, f_ref, p_ref, a_ref, sg_ref, mod_ref, w_ref, row, d, rows=slice(None)):
    gate = mod_ref[pl.ds(row, 1), 2 * d:]
    sg = sg_ref[rows, :].astype(F32)
    yf = (f_ref[rows, :] * sg[:, :MIX_Q]).astype(BF16)
    yp = (p_ref[rows, :] * sg[:, MIX_Q:2 * MIX_Q]).astype(BF16)
    ya = (a_ref[rows, :].astype(F32) * sg[:, 2 * MIX_Q:]).astype(BF16)
    acc = jnp.dot(yf, w_ref[0:MIX_Q, :], preferred_element_type=F32)
    acc += jnp.dot(yp, w_ref[MIX_Q:2 * MIX_Q, :], preferred_element_type=F32)
    acc += jnp.dot(ya, w_ref[2 * MIX_Q:, :], preferred_element_type=F32)
    return x_ref[rows, :] + gate * acc


def _inproj_kernel(x_ref, *rest, d, ctx_row, rope, fused):
    row = ctx_row if ctx_row is not None else pl.program_id(0)
    if fused:
        prev, rest = rest[:6], rest[6:]
    mod_ref, ng_ref, w_ref, qkg_ref, ones_ref = rest[:5]
    rest = rest[5:]
    if rope:
        cos_ref, sup_ref, sdn_ref = rest[:3]
        rest = rest[3:]
    mod = mod_ref[pl.ds(row, 1), :]
    shift, gain = mod[:, :d], ng_ref[...] * (1.0 + mod[:, d:2 * d])

    def modulated_norm(xf):
        ms = jnp.mean(xf * xf, axis=-1, keepdims=True)
        return (xf * lax.rsqrt(ms + EPS) * gain + shift).astype(BF16)

    if fused:
        xnew_ref, rest = rest[0], rest[1:]
    half = x_ref.shape[0] // 2
    h_parts = []
    for part in range(2):
        rows = slice(part * half, (part + 1) * half)
        if fused:
            xf = _residual_update(x_ref, *prev, row, d, rows=rows)
            xnew_ref[rows, :] = xf
        else:
            xf = x_ref[rows, :]
        h_parts.append(modulated_norm(xf))
    h = jnp.concatenate(h_parts, axis=0)
    uv_ref, bp_ref, qt_ref, k_ref, vt_ref, sg_ref = rest

    def proj(lo, hi, by_parts=False):
        if by_parts:
            return jnp.concatenate([jnp.dot(hp, w_ref[:, lo:hi], preferred_element_type=F32) for hp in h_parts],
                                   axis=0)
        return jnp.dot(h, w_ref[:, lo:hi], preferred_element_type=F32)

    pair = 2 * LANES
    n_pairs = ATTN_W // pair
    ones_blk = ones_ref[...]
    q_raw = [proj(Q_OFF + p * pair, Q_OFF + (p + 1) * pair, by_parts=True) for p in range(n_pairs)]
    k_raw = [proj(K_OFF + p * pair, K_OFF + (p + 1) * pair) for p in range(n_pairs)]
    q_nrm = [_qk_normalise(t, ones_blk, qkg_ref[0:1, :]) for t in q_raw]
    k_nrm = [_qk_normalise(t, ones_blk, qkg_ref[1:2, :]) for t in k_raw]
    v_raw = [proj(VV_OFF + p * pair, VV_OFF + (p + 1) * pair) for p in range(n_pairs)]
    g = proj(G_OFF, W_EFF)
    sg_ref[...] = (g * jax.nn.sigmoid(g)).astype(sg_ref.dtype)
    uv_ref[...] = proj(U_OFF, P_OFF)
    bp_ref[...] = proj(P_OFF, Q_OFF)

    if rope:
        cos, sup, sdn = cos_ref[...], sup_ref[...], sdn_ref[...]
    for p in range(n_pairs):
        qp, kp, vp = q_nrm[p], k_nrm[p], v_raw[p]
        for half in range(2):
            sl = slice(p * pair + half * LANES, p * pair + (half + 1) * LANES)
            hs = slice(half * LANES, (half + 1) * LANES)
            qt, kt = qp[:, hs], kp[:, hs]
            if rope:
                qt = _rotate(qt, cos, sup, sdn)
                kt = _rotate(kt, cos, sup, sdn)
            qt_ref[sl, :] = (qt * Q_SCALE).T.astype(qt_ref.dtype)
            k_ref[:, sl] = kt.astype(k_ref.dtype)
            vt_ref[sl, :] = vp[:, hs].T.astype(vt_ref.dtype)


def _in_projection(x, mod, norm_g, w_eff, qk_gain, ones_blk, rope_tabs, *, layer, ctx_row, tl, prev=None):
    b, l, d = x.shape
    rope = rope_tabs is not None
    fused = prev is not None
    row_blk = lambda width: pl.BlockSpec((None, tl, width), lambda bi, i: (bi, i, 0))
    full = lambda arr: pl.BlockSpec(arr.shape, lambda bi, i: (0,) * arr.ndim)

    def of_layer(arr, which=layer):
        return pl.BlockSpec((None,) + arr.shape[1:], lambda bi, i: (which,) + (0,) * (arr.ndim - 1))

    in_specs, args = [row_blk(d)], [x]
    if fused:
        four, pool, attn, sg_prev, w_out = prev
        in_specs += [row_blk(MIX_Q), row_blk(MIX_Q), row_blk(ATTN_W), row_blk(d), of_layer(mod, layer - 1),
                     of_layer(w_out, layer - 1)]
        args += [four, pool, attn, sg_prev, mod, w_out]
    in_specs += [of_layer(mod), full(norm_g), of_layer(w_eff), full(qk_gain), full(ones_blk)]
    args += [mod, norm_g, w_eff, qk_gain, ones_blk]
    if rope:
        in_specs += [pl.BlockSpec((tl, LANES), lambda bi, i: (i, 0))] * 3
        args += list(rope_tabs)
    out_shape = (
        jax.ShapeDtypeStruct((b, l, 2 * MIX_Q), F32),
        jax.ShapeDtypeStruct((b, l, MIX_Q), F32),
        jax.ShapeDtypeStruct((b, ATTN_W, l), BF16),
        jax.ShapeDtypeStruct((b, l, ATTN_W), BF16),
        jax.ShapeDtypeStruct((b, ATTN_W, l), BF16),
        jax.ShapeDtypeStruct((b, l, d), BF16),
    )
    col_blk = pl.BlockSpec((None, ATTN_W, tl), lambda bi, i: (bi, 0, i))
    out_specs = (row_blk(2 * MIX_Q), row_blk(MIX_Q), col_blk, row_blk(ATTN_W), col_blk, row_blk(d))
    if fused:
        out_shape = (jax.ShapeDtypeStruct((b, l, d), F32),) + out_shape
        out_specs = (row_blk(d),) + out_specs
    name = "in_projection_ctx" if ctx_row is not None else "in_projection"
    return pl.pallas_call(
        functools.partial(_inproj_kernel, d=d, ctx_row=ctx_row, rope=rope, fused=fused),
        grid=(b, l // tl),
        in_specs=in_specs,
        out_specs=out_specs,
        out_shape=out_shape,
        compiler_params=_cparams("parallel", "parallel"),
        name="residual_" + name if fused else name,
    )(*args)


def _attn_kernel(*refs, n_seg, seg_len, tk, lam_init):
    qt_ref = refs[0]
    kv_refs = refs[1:1 + 2 * n_seg]
    lam_ref, sub_ref, o_ref, m_ref, l_ref, acc_ref = refs[1 + 2 * n_seg:]
    qt = qt_ref[...]
    first_map = lax.broadcasted_iota(jnp.int32, qt.shape, 0) < QK_DIM
    zero = jnp.zeros_like(qt)
    ws = (jnp.where(first_map, qt, zero), jnp.where(first_map, zero, qt))

    m_ref[...] = jnp.full(m_ref.shape, -jnp.inf, F32)
    l_ref[...] = jnp.zeros(l_ref.shape, F32)
    acc_ref[...] = jnp.zeros(acc_ref.shape, F32)

    def chunk(k_ref, vt_ref, start, size):
        k = k_ref[pl.ds(start, size), :]
        vt = vt_ref[:, pl.ds(start, size)]
        for mp in range(2):
            s = jnp.dot(k, ws[mp], preferred_element_type=F32)
            m_old = m_ref[mp]
            m_new = jnp.maximum(m_old, jnp.max(s, axis=0, keepdims=True))
            alpha = jnp.exp2(m_old - m_new)
            e = jnp.exp2(s - m_new)
            l_ref[mp] = alpha * l_ref[mp] + jnp.sum(e, axis=0, keepdims=True)
            acc_ref[mp] = alpha * acc_ref[mp] + jnp.dot(vt, e.astype(BF16), preferred_element_type=F32)
            m_ref[mp] = m_new

    for si in range(n_seg):
        k_ref, vt_ref = kv_refs[2 * si], kv_refs[2 * si + 1]
        length = seg_len[si]
        size = min(tk, length)
        n_chunks = length // size
        if n_chunks == 1:
            chunk(k_ref, vt_ref, 0, size)
        else:
            def body(j, carry, k_ref=k_ref, vt_ref=vt_ref, size=size):
                chunk(k_ref, vt_ref, pl.multiple_of(j * size, size), size)
                return carry
            lax.fori_loop(0, n_chunks, body, 0)

    lv = lam_ref[...]
    lam = (jnp.exp(jnp.sum(lv[0:1] * lv[1:2], axis=-1, keepdims=True))
           - jnp.exp(jnp.sum(lv[2:3] * lv[3:4], axis=-1, keepdims=True)) + lam_init)
    o = acc_ref[0] / l_ref[0] - lam * (acc_ref[1] / l_ref[1])
    ms = jnp.mean(o * o, axis=0, keepdims=True)
    o = o * lax.rsqrt(ms + EPS) * (sub_ref[...] * (1.0 - lam_init))
    o_ref[...] = o.T.astype(o_ref.dtype)


def _attn_unshifted_kernel(*refs, n_seg, seg_len, tq, tk, lam_init):
    qt_ref = refs[0]
    kv_refs = refs[1:1 + 2 * n_seg]
    lam_ref, sub_ref, o_ref, l_ref, acc_ref, sa_ref, sb_ref = refs[1 + 2 * n_seg:]
    first_map = lax.broadcasted_iota(jnp.int32, (HEAD_W, tq), 0) < QK_DIM
    lv = lam_ref[...]
    lam = (jnp.exp(jnp.sum(lv[0:1] * lv[1:2], axis=-1, keepdims=True))
           - jnp.exp(jnp.sum(lv[2:3] * lv[3:4], axis=-1, keepdims=True)) + lam_init)
    out_gain = sub_ref[...] * (1.0 - lam_init)

    chunks = []
    for i in range(n_seg):
        size = min(tk, seg_len[i])
        assert seg_len[i] % size == 0
        chunks += [(kv_refs[2 * i], kv_refs[2 * i + 1], c * size, size) for c in range(seg_len[i] // size)]
    bufs = (sa_ref, sb_ref)

    def query_tile(qi, carry):
        q0 = pl.multiple_of(qi * tq, tq)
        qt = qt_ref[:, pl.ds(q0, tq)]
        zero = jnp.zeros_like(qt)
        w = jnp.concatenate([jnp.where(first_map, qt, zero), jnp.where(first_map, zero, qt)], axis=1)
        l_ref[...] = jnp.zeros(l_ref.shape, F32)
        acc_ref[...] = jnp.zeros(acc_ref.shape, F32)

        def scores(s_ref, k_ref, start, size):
            s_ref[0:size, :] = jnp.dot(k_ref[pl.ds(start, size), :], w, preferred_element_type=F32)

        def consume(s_ref, vt_ref, start, size):
            e = jnp.exp2(s_ref[0:size, :])
            l_ref[...] += jnp.sum(e.reshape(size // 8, 8, 2 * tq), axis=0)
            acc_ref[...] += jnp.dot(vt_ref[:, pl.ds(start, size)], e.astype(BF16), preferred_element_type=F32)

        scores(bufs[0], chunks[0][0], chunks[0][2], chunks[0][3])
        for i, (_, vt_ref, start, size) in enumerate(chunks):
            if i + 1 < len(chunks):
                nxt = chunks[i + 1]
                scores(bufs[(i + 1) % 2], nxt[0], nxt[2], nxt[3])
            consume(bufs[i % 2], vt_ref, start, size)

        inv = 1.0 / jnp.sum(l_ref[...], axis=0, keepdims=True)
        acc = acc_ref[...]
        o = acc[:, :tq] * inv[:, :tq] - (lam * inv[:, tq:]) * acc[:, tq:]
        ms = jnp.mean(o * o, axis=0, keepdims=True)
        o = o * lax.rsqrt(ms + EPS) * out_gain
        o_ref[pl.ds(q0, tq), :] = o.T.astype(o_ref.dtype)
        return carry

    lax.fori_loop(0, qt_ref.shape[1] // tq, query_tile, 0)


def _attention(qt, segs, lam_vecs, subln_col, lam_init, *, tq, tk, shifted):
    b, _, lq = qt.shape
    tq = min(tq, lq)
    seg_len = tuple(k.shape[1] for k, _ in segs)
    const = lambda arr: pl.BlockSpec(arr.shape, lambda bi, h, *i: (0, 0))
    if shifted:
        tq = min(tq, SHIFTED_TQ)
        grid = (b, N_HEADS, lq // tq)
        q_spec = pl.BlockSpec((None, HEAD_W, tq), lambda bi, h, i: (bi, h, i))
        o_spec = pl.BlockSpec((None, tq, HEAD_W), lambda bi, h, i: (bi, i, h))
        body = functools.partial(_attn_kernel, n_seg=len(segs), seg_len=seg_len, tk=tk, lam_init=lam_init)
        scratch = [pltpu.VMEM((2, 1, tq), F32), pltpu.VMEM((2, 1, tq), F32), pltpu.VMEM((2, HEAD_W, tq), F32)]
        semantics = ("parallel", "parallel", "arbitrary")
    else:
        grid = (b, N_HEADS)
        q_spec = pl.BlockSpec((None, HEAD_W, lq), lambda bi, h: (bi, h, 0))
        o_spec = pl.BlockSpec((None, lq, HEAD_W), lambda bi, h: (bi, 0, h))
        body = functools.partial(_attn_unshifted_kernel, n_seg=len(segs), seg_len=seg_len, tq=tq, tk=tk,
                                 lam_init=lam_init)
        s_rows = min(tk, max(seg_len))
        scratch = [pltpu.VMEM((8, 2 * tq), F32), pltpu.VMEM((HEAD_W, 2 * tq), F32),
                   pltpu.VMEM((s_rows, 2 * tq), F32), pltpu.VMEM((s_rows, 2 * tq), F32)]
        semantics = ("parallel", "parallel")
    in_specs, args = [q_spec], [qt]
    for (k, vt), lk in zip(segs, seg_len):
        in_specs.append(pl.BlockSpec((None, lk, HEAD_W), lambda bi, h, *i: (bi, 0, h)))
        in_specs.append(pl.BlockSpec((None, HEAD_W, lk), lambda bi, h, *i: (bi, h, 0)))
        args += [k, vt]
    in_specs += [const(lam_vecs), const(subln_col)]
    args += [lam_vecs, subln_col]
    return pl.pallas_call(
        body,
        grid=grid,
        in_specs=in_specs,
        out_specs=o_spec,
        out_shape=jax.ShapeDtypeStruct((b, lq, ATTN_W), BF16),
        scratch_shapes=scratch,
        compiler_params=_cparams(*semantics),
        name="diff_attention_shifted" if shifted else "diff_attention",
    )(*args)


def _attention_dispatch(score_bound, *args, **kw):
    return lax.cond(score_bound <= MAX_UNSHIFTED_SCORE,
                    lambda: _attention(*args, shifted=False, **kw),
                    lambda: _attention(*args, shifted=True, **kw))


def _fourier_kernel(u_ref, v_ref, m1_ref, c2_ref, s2_ref, o_ref, yr_ref, yi_ref, *, n1, n2):
    def stage1(j, carry):
        xu = u_ref[pl.ds(j, n1, stride=n2), :]
        xv = v_ref[pl.ds(j, n1, stride=n2), :]
        x = jnp.concatenate([xu, xv], axis=1).astype(BF16)
        p = jnp.dot(m1_ref[j], x, preferred_element_type=F32)
        rows = pl.ds(pl.multiple_of(j * n1, n1), n1)
        yr_ref[rows, :] = p[:n1, :LANES] + p[n1:, LANES:]
        yi_ref[rows, :] = p[n1:, :LANES] - p[:n1, LANES:]
        return carry

    lax.fori_loop(0, n2, stage1, 0, unroll=FOURIER_UNROLL)
    c2, s2 = c2_ref[...], s2_ref[...]

    def stage2(k1, carry):
        rows = pl.ds(k1, n2, stride=n1)
        xr = (jnp.dot(c2, yr_ref[rows, :].astype(BF16), preferred_element_type=F32)
              + jnp.dot(s2, yi_ref[rows, :].astype(BF16), preferred_element_type=F32))
        o_ref[rows, :] = xr
        return carry

    lax.fori_loop(0, n1, stage2, 0, unroll=FOURIER_UNROLL)


def _fourier_latent(uv, n1, n2):
    b, l, _ = uv.shape
    m1, c2, s2 = _stage_tables(n1, n2)
    m1, c2, s2 = (jnp.asarray(t).astype(BF16) for t in (m1, c2, s2))
    halves = MIX_Q // LANES
    return pl.pallas_call(
        functools.partial(_fourier_kernel, n1=n1, n2=n2),
        grid=(b, halves),
        in_specs=[
            pl.BlockSpec((None, l, LANES), lambda bi, c: (bi, 0, c)),
            pl.BlockSpec((None, l, LANES), lambda bi, c: (bi, 0, c + halves)),
            pl.BlockSpec(m1.shape, lambda bi, c: (0, 0, 0)),
            pl.BlockSpec(c2.shape, lambda bi, c: (0, 0)),
            pl.BlockSpec(s2.shape, lambda bi, c: (0, 0)),
        ],
        out_specs=pl.BlockSpec((None, l, LANES), lambda bi, c: (bi, 0, c)),
        out_shape=jax.ShapeDtypeStruct((b, l, MIX_Q), F32),
        scratch_shapes=[pltpu.VMEM((l, LANES), F32), pltpu.VMEM((l, LANES), F32)],
        compiler_params=_cparams("parallel", "parallel"),
        name="fourier_positions",
    )(uv, uv, m1, c2, s2)


def _fourier_dense_kernel(uv_ref, c_ref, s_ref, o_ref):
    uv = uv_ref[...].astype(BF16)
    o_ref[...] = (jnp.dot(c_ref[...], uv[:, :MIX_Q], preferred_element_type=F32)
                  - jnp.dot(s_ref[...], uv[:, MIX_Q:], preferred_element_type=F32))


def _fourier_dense(uv):
    b, l, _ = uv.shape
    n = np.arange(l)
    ang = 2.0 * np.pi * np.outer(n, n) / l
    c = jnp.asarray((np.cos(ang) / math.sqrt(l)).astype(np.float32)).astype(BF16)
    s = jnp.asarray((np.sin(ang) / math.sqrt(l)).astype(np.float32)).astype(BF16)
    return pl.pallas_call(
        _fourier_dense_kernel,
        grid=(b,),
        in_specs=[pl.BlockSpec((None, l, 2 * MIX_Q), lambda bi: (bi, 0, 0)),
                  pl.BlockSpec((l, l), lambda bi: (0, 0)),
                  pl.BlockSpec((l, l), lambda bi: (0, 0))],
        out_specs=pl.BlockSpec((None, l, MIX_Q), lambda bi: (bi, 0, 0)),
        out_shape=jax.ShapeDtypeStruct((b, l, MIX_Q), F32),
        compiler_params=_cparams("parallel"),
        name="fourier_positions_ctx",
    )(uv, c, s)


POOL_HALO = 8


def _pool_kernel(b_ref, o_ref, pad_ref, *, l, rows):
    zeros = jnp.zeros((POOL_HALO, LANES), F32)
    lane = lax.broadcasted_iota(jnp.int32, (rows, LANES), 1)
    low_group = lane < GROUP_W
    n_chunks = l // rows
    for tile in range(MIX_Q // LANES):
        w_lo, w_hi = POOL_WINDOWS[2 * tile], POOL_WINDOWS[2 * tile + 1]
        pad_ref[0:POOL_HALO, :] = zeros
        pad_ref[POOL_HALO + l:, :] = zeros
        pad_ref[POOL_HALO:POOL_HALO + l, :] = b_ref[:, tile * LANES:(tile + 1) * LANES]

        def chunk(t0, edge, tile=tile, w_lo=w_lo, w_hi=w_hi):
            shifted = lambda dlt: pad_ref[pl.ds(t0 + POOL_HALO + dlt, rows), :]
            lo_range = range(-(w_lo // 2), w_lo // 2)
            inner = functools.reduce(lambda a, b: a + b, [shifted(d) for d in lo_range])
            outer = functools.reduce(lambda a, b: a + b,
                                     [shifted(d) for d in range(-(w_hi // 2), w_hi // 2) if d not in lo_range], inner)
            sums = jnp.where(low_group, inner, outer)
            if edge:
                pos = t0 + lax.broadcasted_iota(jnp.int32, (rows, LANES), 0)
                half = jnp.where(low_group, w_lo // 2, w_hi // 2)
                cnt = jnp.minimum(pos + (half - 1), l - 1) - jnp.maximum(pos - half, 0) + 1
                mean = sums / cnt.astype(F32)
            else:
                mean = sums * jnp.where(low_group, 1.0 / w_lo, 1.0 / w_hi)
            o_ref[pl.ds(t0, rows), tile * LANES:(tile + 1) * LANES] = mean - shifted(0)

        assert rows >= POOL_HALO
        chunk(0, True)
        if n_chunks > 1:
            def body(ci, carry, chunk=chunk):
                chunk(pl.multiple_of(ci * rows, rows), False)
                return carry

            lax.fori_loop(1, n_chunks - 1, body, 0)
            chunk((n_chunks - 1) * rows, True)


def _pool(bp):
    b, l, _ = bp.shape
    rows = 256
    return pl.pallas_call(
        functools.partial(_pool_kernel, l=l, rows=rows),
        grid=(b,),
        in_specs=[pl.BlockSpec((None, l, MIX_Q), lambda bi: (bi, 0, 0))],
        out_specs=pl.BlockSpec((None, l, MIX_Q), lambda bi: (bi, 0, 0)),
        out_shape=jax.ShapeDtypeStruct((b, l, MIX_Q), F32),
        scratch_shapes=[pltpu.VMEM((l + 2 * POOL_HALO, LANES), F32)],
        compiler_params=_cparams("parallel"),
        name="pool_windows",
    )(bp)


def _outproj_kernel(x_ref, f_ref, p_ref, a_ref, sg_ref, mod_ref, w_ref, o_ref, *, d, ctx_row):
    row = ctx_row if ctx_row is not None else pl.program_id(0)
    o_ref[...] = _residual_update(x_ref, f_ref, p_ref, a_ref, sg_ref, mod_ref, w_ref, row, d)


def _out_projection(x, four, pool, attn, sg, mod, w_out, *, layer, ctx_row, tl):
    b, l, d = x.shape
    row_blk = lambda width: pl.BlockSpec((None, tl, width), lambda bi, i: (bi, i, 0))
    of_layer = lambda arr: pl.BlockSpec((None,) + arr.shape[1:], lambda bi, i: (layer,) + (0,) * (arr.ndim - 1))
    return pl.pallas_call(
        functools.partial(_outproj_kernel, d=d, ctx_row=ctx_row),
        grid=(b, l // tl),
        in_specs=[row_blk(d), row_blk(MIX_Q), row_blk(MIX_Q), row_blk(ATTN_W), row_blk(d), of_layer(mod),
                  of_layer(w_out)],
        out_specs=row_blk(d),
        out_shape=jax.ShapeDtypeStruct((b, l, d), F32),
        compiler_params=_cparams("parallel", "parallel"),
        name="out_projection_ctx" if ctx_row is not None else "out_projection",
    )(x, four, pool, attn, sg, mod, w_out)


def kernel(x, c, ctx, c_ctx, norm_g, w_mod, b_mod, w_in, w_fourier, w_pool, pool_scale, qk_norm_g, lam_vecs,
           subln_g, w_out):
    b, l, d = x.shape
    depth = w_in.shape[0]
    ctx_len = ctx.shape[1]
    ctx_row = b

    cc = jnp.concatenate([c, c_ctx[None, :], jnp.zeros((8 - b - 1, d), F32)], axis=0)
    mod = _modulation(cc, w_mod, b_mod)
    w_eff = _fold_weights(w_in, w_fourier, w_pool, pool_scale)
    w_out_bf = _cast_bf16(w_out)
    rope_tabs = _rope_tables(l // GRID_W)
    lane_chunk = lax.broadcasted_iota(jnp.int32, (2 * LANES, 2 * LANES), 0) // QK_DIM
    ones_blk = (lane_chunk == lane_chunk.T).astype(BF16)
    n1 = 64
    n2 = l // n1

    pending = None
    for layer in range(depth):
        lam_init = 0.8 - 0.6 * math.exp(-0.3 * layer)
        update_ctx = layer < depth - 1
        ng = norm_g[layer][None, :]
        qk_gain = jnp.tile(qk_norm_g[layer], (1, 2 * LANES // QK_DIM))
        sub_g = subln_g[layer][:, None]
        lv = lam_vecs[layer]

        outs = _in_projection(x, mod, ng, w_eff, qk_gain, ones_blk, rope_tabs, layer=layer, ctx_row=None,
                              tl=512, prev=pending)
        if pending is not None:
            x, outs = outs[0], outs[1:]
        uv, bp, qt, k, vt, sg = outs
        uv_c, bp_c, qt_c, k_c, vt_c, sg_c = _in_projection(ctx, mod, ng, w_eff, qk_gain, ones_blk, None,
                                                           layer=layer, ctx_row=ctx_row, tl=ctx_len)
        gains = jnp.max(jnp.abs(qk_norm_g[layer]), axis=-1)
        score_bound = QK_DIM * gains[0] * gains[1] * (Q_SCALE * BF16_ROUNDING_SLACK)
        attn = _attention_dispatch(score_bound, qt, [(k_c, vt_c), (k, vt)], lv, sub_g, lam_init, tq=ATTN_TQ,
                                  tk=ATTN_TK)
        four = _fourier_latent(uv, n1, n2)
        pool = _pool(bp)
        if update_ctx:
            attn_c = _attention_dispatch(score_bound, qt_c, [(k_c, vt_c)], lv, sub_g, lam_init, tq=ctx_len, tk=512)
            ctx = _out_projection(ctx, _fourier_dense(uv_c), _pool(bp_c), attn_c, sg_c, mod, w_out_bf,
                                  layer=layer, ctx_row=ctx_row, tl=ctx_len)
        if layer + 1 < depth:
            pending = (four, pool, attn, sg, w_out_bf)
        else:
            x = _out_projection(x, four, pool, attn, sg, mod, w_out_bf, layer=layer, ctx_row=None, tl=1024)
    return x
```

```python
import functools
import math

import numpy as np
import jax
import jax.numpy as jnp
from jax import lax
from jax.experimental import pallas as pl
from jax.experimental.pallas import tpu as pltpu

GRID_W = 64
FOURIER_GROUPS = 4
GROUP_W = 64
MIX_Q = 256
POOL_WINDOWS = (2, 4, 8, 16)
QK_DIM = 64
HEAD_W = 2 * QK_DIM
N_HEADS = 4
ATTN_W = N_HEADS * HEAD_W
ROPE_THETA = 10000.0
EPS = 1e-6
Q_SCALE = QK_DIM ** -0.5 * math.log2(math.e)
FOURIER_UNROLL = 8
ATTN_TK = 512
ATTN_TQ = 2048
SHIFTED_TQ = 512
MAX_UNSHIFTED_SCORE = 40.0
BF16_ROUNDING_SLACK = 1.02
U_OFF, V_OFF, P_OFF, Q_OFF, K_OFF, VV_OFF, G_OFF, W_EFF = 0, 256, 512, 768, 1280, 1792, 2304, 3328

LANES = 128
VMEM_LIMIT = 56 * 1024 * 1024
HI = lax.Precision.HIGHEST
F32 = jnp.float32
BF16 = jnp.bfloat16


def _cparams(*sem):
    return pltpu.CompilerParams(dimension_semantics=sem, vmem_limit_bytes=VMEM_LIMIT)


def _channel_dft():
    c = np.arange(GROUP_W)
    ang = 2.0 * np.pi * np.outer(c, c) / GROUP_W
    s = 1.0 / math.sqrt(GROUP_W)
    return np.cos(ang) * s, np.sin(ang) * s


def _stage_tables(n1, n2):
    n = n1 * n2
    k1 = np.arange(n1)
    i1 = np.arange(n1)
    j = np.arange(n2)
    theta = 2.0 * np.pi * (j[:, None, None] * k1[None, :, None] / n + k1[None, :, None] * i1[None, None, :] / n1)
    m1 = np.concatenate([np.cos(theta), -np.sin(theta)], axis=1) / math.sqrt(n)
    ang2 = 2.0 * np.pi * np.outer(j, j) / n2
    return m1.astype(np.float32), np.cos(ang2).astype(np.float32), np.sin(ang2).astype(np.float32)


def _rope_tables(rows):
    f32 = np.float32
    row = np.repeat(np.arange(rows), GRID_W).astype(f32)
    col = np.tile(np.arange(GRID_W), rows).astype(f32)
    half = QK_DIM // 2
    inv_freq = np.power(f32(ROPE_THETA), -np.arange(0, half, 2, dtype=f32) / f32(half))
    ang_r = row[:, None] * inv_freq[None, :]
    ang_c = col[:, None] * inv_freq[None, :]
    ang = np.concatenate([ang_r, ang_r, ang_c, ang_c], axis=-1)
    cos, sin = np.cos(ang), np.sin(ang)
    quarter = QK_DIM // 4
    first = (np.arange(QK_DIM) % half) < quarter
    sin_up = np.where(first, -sin, f32(0.0))
    sin_dn = np.where(first, f32(0.0), sin)
    rep = LANES // QK_DIM
    return tuple(jnp.asarray(np.tile(t, (1, rep)).astype(f32)) for t in (cos, sin_up, sin_dn))


def _fold_kernel(w_ref, wf_ref, wp_ref, ps_ref, cd_ref, sd_ref, o_ref):
    cd, sd = cd_ref[...], sd_ref[...]
    for g in range(FOURIER_GROUPS):
        sl = slice(g * GROUP_W, (g + 1) * GROUP_W)
        wa = w_ref[:, sl]
        wf = wf_ref[g]
        cw = jnp.dot(cd, wf, precision=HI, preferred_element_type=F32)
        sw = jnp.dot(sd, wf, precision=HI, preferred_element_type=F32)
        o_ref[:, U_OFF + g * GROUP_W:U_OFF + (g + 1) * GROUP_W] = jnp.dot(
            wa, cw, precision=HI, preferred_element_type=F32).astype(o_ref.dtype)
        o_ref[:, V_OFF + g * GROUP_W:V_OFF + (g + 1) * GROUP_W] = jnp.dot(
            wa, sw, precision=HI, preferred_element_type=F32).astype(o_ref.dtype)
        wb = w_ref[:, MIX_Q + g * GROUP_W:MIX_Q + (g + 1) * GROUP_W]
        pw = jnp.dot(wb, wp_ref[g], precision=HI, preferred_element_type=F32) * ps_ref[:, sl]
        o_ref[:, P_OFF + g * GROUP_W:P_OFF + (g + 1) * GROUP_W] = pw.astype(o_ref.dtype)
    o_ref[:, Q_OFF:] = w_ref[:, 2 * MIX_Q:].astype(o_ref.dtype)


def _fold_weights(w_in, w_fourier, w_pool, pool_scale):
    depth, d, in_w = w_in.shape
    tr = 256
    cd, sd = _channel_dft()
    return pl.pallas_call(
        _fold_kernel,
        grid=(depth, d // tr),
        in_specs=[
            pl.BlockSpec((None, tr, in_w), lambda l, i: (l, i, 0)),
            pl.BlockSpec((None, FOURIER_GROUPS, GROUP_W, GROUP_W), lambda l, i: (l, 0, 0, 0)),
            pl.BlockSpec((None, FOURIER_GROUPS, GROUP_W, GROUP_W), lambda l, i: (l, 0, 0, 0)),
            pl.BlockSpec((None, 1, MIX_Q), lambda l, i: (l, 0, 0)),
            pl.BlockSpec((GROUP_W, GROUP_W), lambda l, i: (0, 0)),
            pl.BlockSpec((GROUP_W, GROUP_W), lambda l, i: (0, 0)),
        ],
        out_specs=pl.BlockSpec((None, tr, W_EFF), lambda l, i: (l, i, 0)),
        out_shape=jax.ShapeDtypeStruct((depth, d, W_EFF), BF16),
        compiler_params=_cparams("parallel", "parallel"),
        name="fold_weights",
    )(w_in, w_fourier, w_pool, pool_scale[:, None, :], jnp.asarray(cd, F32), jnp.asarray(sd, F32))


def _cast_kernel(w_ref, o_ref):
    o_ref[...] = w_ref[...].astype(o_ref.dtype)


def _cast_bf16(w):
    depth, r, c = w.shape
    return pl.pallas_call(
        _cast_kernel,
        grid=(depth,),
        in_specs=[pl.BlockSpec((None, r, c), lambda l: (l, 0, 0))],
        out_specs=pl.BlockSpec((None, r, c), lambda l: (l, 0, 0)),
        out_shape=jax.ShapeDtypeStruct(w.shape, BF16),
        compiler_params=_cparams("parallel"),
        name="cast_w_out",
    )(w)


def _mod_kernel(c_ref, w_ref, b_ref, o_ref):
    cc = c_ref[...]
    sc = cc * jax.nn.sigmoid(cc)
    o_ref[...] = jnp.dot(sc, w_ref[...], precision=HI, preferred_element_type=F32) + b_ref[...]


def _modulation(cc, w_mod, b_mod):
    depth, d, n = w_mod.shape
    tn = 1024
    rows = cc.shape[0]
    return pl.pallas_call(
        _mod_kernel,
        grid=(depth, n // tn),
        in_specs=[
            pl.BlockSpec((rows, d), lambda l, j: (0, 0)),
            pl.BlockSpec((None, d, tn), lambda l, j: (l, 0, j)),
            pl.BlockSpec((None, 1, tn), lambda l, j: (l, 0, j)),
        ],
        out_specs=pl.BlockSpec((None, rows, tn), lambda l, j: (l, 0, j)),
        out_shape=jax.ShapeDtypeStruct((depth, rows, n), F32),
        compiler_params=_cparams("parallel", "parallel"),
        name="modulation",
    )(cc, w_mod, b_mod[:, None, :])


def _qk_normalise(t, ones_blk, gain):
    ss = jnp.dot((t * t).astype(BF16), ones_blk, preferred_element_type=F32)
    return t * lax.rsqrt(ss * (1.0 / QK_DIM) + EPS) * gain


def _rotate(t, cos, sin_up, sin_dn):
    quarter = QK_DIM // 4
    up = pltpu.roll(t, LANES - quarter, 1)
    dn = pltpu.roll(t, quarter, 1)
    return t * cos + up * sin_up + dn * sin_dn


def _residual_update(x_ref, f_ref, p_ref, a_ref, sg_ref, mod_ref, w_ref, row, d, rows=slice(None)):
    gate = mod_ref[pl.ds(row, 1), 2 * d:]
    sg = sg_ref[rows, :].astype(F32)
    yf = (f_ref[rows, :] * sg[:, :MIX_Q]).astype(BF16)
    yp = (p_ref[rows, :] * sg[:, MIX_Q:2 * MIX_Q]).astype(BF16)
    ya = (a_ref[rows, :].astype(F32) * sg[:, 2 * MIX_Q:]).astype(BF16)
    acc = jnp.dot(yf, w_ref[0:MIX_Q, :], preferred_element_type=F32)
    acc += jnp.dot(yp, w_ref[MIX_Q:2 * MIX_Q, :], preferred_element_type=F32)
    acc += jnp.dot(ya, w_ref[2 * MIX_Q:, :], preferred_element_type=F32)
    return x_ref[rows, :] + gate * acc


def _inproj_kernel(x_ref, *rest, d, ctx_row, rope, fused):
    row = ctx_row if ctx_row is not None else pl.program_id(0)
    if fused:
        prev, rest = rest[:6], rest[6:]
    mod_ref, ng_ref, w_ref, qkg_ref, ones_ref = rest[:5]
    rest = rest[5:]
    if rope:
        cos_ref, sup_ref, sdn_ref = rest[:3]
        rest = rest[3:]
    mod = mod_ref[pl.ds(row, 1), :]
    shift, gain = mod[:, :d], ng_ref[...] * (1.0 + mod[:, d:2 * d])

    def modulated_norm(xf):
        ms = jnp.mean(xf * xf, axis=-1, keepdims=True)
        return (xf * lax.rsqrt(ms + EPS) * gain + shift).astype(BF16)

    if fused:
        xnew_ref, rest = rest[0], rest[1:]
    half = x_ref.shape[0] // 2
    h_parts = []
    for part in range(2):
        rows = slice(part * half, (part + 1) * half)
        if fused:
            xf = _residual_update(x_ref, *prev, row, d, rows=rows)
            xnew_ref[rows, :] = xf
        else:
            xf = x_ref[rows, :]
        h_parts.append(modulated_norm(xf))
    h = jnp.concatenate(h_parts, axis=0)
    uv_ref, bp_ref, qt_ref, k_ref, vt_ref, sg_ref = rest

    def proj(lo, hi, by_parts=False):
        if by_parts:
            return jnp.concatenate([jnp.dot(hp, w_ref[:, lo:hi], preferred_element_type=F32) for hp in h_parts],
                                   axis=0)
        return jnp.dot(h, w_ref[:, lo:hi], preferred_element_type=F32)

    pair = 2 * LANES
    n_pairs = ATTN_W // pair
    ones_blk = ones_ref[...]
    q_raw = [proj(Q_OFF + p * pair, Q_OFF + (p + 1) * pair, by_parts=True) for p in range(n_pairs)]
    k_raw = [proj(K_OFF + p * pair, K_OFF + (p + 1) * pair) for p in range(n_pairs)]
    q_nrm = [_qk_normalise(t, ones_blk, qkg_ref[0:1, :]) for t in q_raw]
    k_nrm = [_qk_normalise(t, ones_blk, qkg_ref[1:2, :]) for t in k_raw]
    v_raw = [proj(VV_OFF + p * pair, VV_OFF + (p + 1) * pair) for p in range(n_pairs)]
    g = proj(G_OFF, W_EFF)
    sg_ref[...] = (g * jax.nn.sigmoid(g)).astype(sg_ref.dtype)
    uv_ref[...] = proj(U_OFF, P_OFF)
    bp_ref[...] = proj(P_OFF, Q_OFF)

    if rope:
        cos, sup, sdn = cos_ref[...], sup_ref[...], sdn_ref[...]
    for p in range(n_pairs):
        qp, kp, vp = q_nrm[p], k_nrm[p], v_raw[p]
        for half in range(2):
            sl = slice(p * pair + half * LANES, p * pair + (half + 1) * LANES)
            hs = slice(half * LANES, (half + 1) * LANES)
            qt, kt = qp[:, hs], kp[:, hs]
            if rope:
                qt = _rotate(qt, cos, sup, sdn)
                kt = _rotate(kt, cos, sup, sdn)
            qt_ref[sl, :] = (qt * Q_SCALE).T.astype(qt_ref.dtype)
            k_ref[:, sl] = kt.astype(k_ref.dtype)
            vt_ref[sl, :] = vp[:, hs].T.astype(vt_ref.dtype)


def _in_projection(x, mod, norm_g, w_eff, qk_gain, ones_blk, rope_tabs, *, layer, ctx_row, tl, prev=None):
    b, l, d = x.shape
    rope = rope_tabs is not None
    fused = prev is not None
    row_blk = lambda width: pl.BlockSpec((None, tl, width), lambda bi, i: (bi, i, 0))
    full = lambda arr: pl.BlockSpec(arr.shape, lambda bi, i: (0,) * arr.ndim)

    def of_layer(arr, which=layer):
        return pl.BlockSpec((None,) + arr.shape[1:], lambda bi, i: (which,) + (0,) * (arr.ndim - 1))

    in_specs, args = [row_blk(d)], [x]
    if fused:
        four, pool, attn, sg_prev, w_out = prev
        in_specs += [row_blk(MIX_Q), row_blk(MIX_Q), row_blk(ATTN_W), row_blk(d), of_layer(mod, layer - 1),
                     of_layer(w_out, layer - 1)]
        args += [four, pool, attn, sg_prev, mod, w_out]
    in_specs += [of_layer(mod), full(norm_g), of_layer(w_eff), full(qk_gain), full(ones_blk)]
    args += [mod, norm_g, w_eff, qk_gain, ones_blk]
    if rope:
        in_specs += [pl.BlockSpec((tl, LANES), lambda bi, i: (i, 0))] * 3
        args += list(rope_tabs)
    out_shape = (
        jax.ShapeDtypeStruct((b, l, 2 * MIX_Q), F32),
        jax.ShapeDtypeStruct((b, l, MIX_Q), F32),
        jax.ShapeDtypeStruct((b, ATTN_W, l), BF16),
        jax.ShapeDtypeStruct((b, l, ATTN_W), BF16),
        jax.ShapeDtypeStruct((b, ATTN_W, l), BF16),
        jax.ShapeDtypeStruct((b, l, d), BF16),
    )
    col_blk = pl.BlockSpec((None, ATTN_W, tl), lambda bi, i: (bi, 0, i))
    out_specs = (row_blk(2 * MIX_Q), row_blk(MIX_Q), col_blk, row_blk(ATTN_W), col_blk, row_blk(d))
    if fused:
        out_shape = (jax.ShapeDtypeStruct((b, l, d), F32),) + out_shape
        out_specs = (row_blk(d),) + out_specs
    name = "in_projection_ctx" if ctx_row is not None else "in_projection"
    return pl.pallas_call(
        functools.partial(_inproj_kernel, d=d, ctx_row=ctx_row, rope=rope, fused=fused),
        grid=(b, l // tl),
        in_specs=in_specs,
        out_specs=out_specs,
        out_shape=out_shape,
        compiler_params=_cparams("parallel", "parallel"),
        name="residual_" + name if fused else name,
    )(*args)


def _attn_kernel(*refs, n_seg, seg_len, tk, lam_init):
    qt_ref = refs[0]
    kv_refs = refs[1:1 + 2 * n_seg]
    lam_ref, sub_ref, o_ref, m_ref, l_ref, acc_ref = refs[1 + 2 * n_seg:]
    qt = qt_ref[...]
    first_map = lax.broadcasted_iota(jnp.int32, qt.shape, 0) < QK_DIM
    zero = jnp.zeros_like(qt)
    ws = (jnp.where(first_map, qt, zero), jnp.where(first_map, zero, qt))

    m_ref[...] = jnp.full(m_ref.shape, -jnp.inf, F32)
    l_ref[...] = jnp.zeros(l_ref.shape, F32)
    acc_ref[...] = jnp.zeros(acc_ref.shape, F32)

    def chunk(k_ref, vt_ref, start, size):
        k = k_ref[pl.ds(start, size), :]
        vt = vt_ref[:, pl.ds(start, size)]
        for mp in range(2):
            s = jnp.dot(k, ws[mp], preferred_element_type=F32)
            m_old = m_ref[mp]
            m_new = jnp.maximum(m_old, jnp.max(s, axis=0, keepdims=True))
            alpha = jnp.exp2(m_old - m_new)
            e = jnp.exp2(s - m_new)
            l_ref[mp] = alpha * l_ref[mp] + jnp.sum(e, axis=0, keepdims=True)
            acc_ref[mp] = alpha * acc_ref[mp] + jnp.dot(vt, e.astype(BF16), preferred_element_type=F32)
            m_ref[mp] = m_new

    for si in range(n_seg):
        k_ref, vt_ref = kv_refs[2 * si], kv_refs[2 * si + 1]
        length = seg_len[si]
        size = min(tk, length)
        n_chunks = length // size
        if n_chunks == 1:
            chunk(k_ref, vt_ref, 0, size)
        else:
            def body(j, carry, k_ref=k_ref, vt_ref=vt_ref, size=size):
                chunk(k_ref, vt_ref, pl.multiple_of(j * size, size), size)
                return carry
            lax.fori_loop(0, n_chunks, body, 0)

    lv = lam_ref[...]
    lam = (jnp.exp(jnp.sum(lv[0:1] * lv[1:2], axis=-1, keepdims=True))
           - jnp.exp(jnp.sum(lv[2:3] * lv[3:4], axis=-1, keepdims=True)) + lam_init)
    o = acc_ref[0] / l_ref[0] - lam * (acc_ref[1] / l_ref[1])
    ms = jnp.mean(o * o, axis=0, keepdims=True)
    o = o * lax.rsqrt(ms + EPS) * (sub_ref[...] * (1.0 - lam_init))
    o_ref[...] = o.T.astype(o_ref.dtype)


def _attn_unshifted_kernel(*refs, n_seg, seg_len, tq, tk, lam_init):
    qt_ref = refs[0]
    kv_refs = refs[1:1 + 2 * n_seg]
    lam_ref, sub_ref, o_ref, l_ref, acc_ref, sa_ref, sb_ref = refs[1 + 2 * n_seg:]
    first_map = lax.broadcasted_iota(jnp.int32, (HEAD_W, tq), 0) < QK_DIM
    lv = lam_ref[...]
    lam = (jnp.exp(jnp.sum(lv[0:1] * lv[1:2], axis=-1, keepdims=True))
           - jnp.exp(jnp.sum(lv[2:3] * lv[3:4], axis=-1, keepdims=True)) + lam_init)
    out_gain = sub_ref[...] * (1.0 - lam_init)

    chunks = []
    for i in range(n_seg):
        size = min(tk, seg_len[i])
        assert seg_len[i] % size == 0
        chunks += [(kv_refs[2 * i], kv_refs[2 * i + 1], c * size, size) for c in range(seg_len[i] // size)]
    bufs = (sa_ref, sb_ref)

    def query_tile(qi, carry):
        q0 = pl.multiple_of(qi * tq, tq)
        qt = qt_ref[:, pl.ds(q0, tq)]
        zero = jnp.zeros_like(qt)
        w = jnp.concatenate([jnp.where(first_map, qt, zero), jnp.where(first_map, zero, qt)], axis=1)
        l_ref[...] = jnp.zeros(l_ref.shape, F32)
        acc_ref[...] = jnp.zeros(acc_ref.shape, F32)

        def scores(s_ref, k_ref, start, size):
            s_ref[0:size, :] = jnp.dot(k_ref[pl.ds(start, size), :], w, preferred_element_type=F32)

        def consume(s_ref, vt_ref, start, size):
            e = jnp.exp2(s_ref[0:size, :])
            l_ref[...] += jnp.sum(e.reshape(size // 8, 8, 2 * tq), axis=0)
            acc_ref[...] += jnp.dot(vt_ref[:, pl.ds(start, size)], e.astype(BF16), preferred_element_type=F32)

        scores(bufs[0], chunks[0][0], chunks[0][2], chunks[0][3])
        for i, (_, vt_ref, start, size) in enumerate(chunks):
            if i + 1 < len(chunks):
                nxt = chunks[i + 1]
                scores(bufs[(i + 1) % 2], nxt[0], nxt[2], nxt[3])
            consume(bufs[i % 2], vt_ref, start, size)

        inv = 1.0 / jnp.sum(l_ref[...], axis=0, keepdims=True)
        acc = acc_ref[...]
        o = acc[:, :tq] * inv[:, :tq] - (lam * inv[:, tq:]) * acc[:, tq:]
        ms = jnp.mean(o * o, axis=0, keepdims=True)
        o = o * lax.rsqrt(ms + EPS) * out_gain
        o_ref[pl.ds(q0, tq), :] = o.T.astype(o_ref.dtype)
        return carry

    lax.fori_loop(0, qt_ref.shape[1] // tq, query_tile, 0)


def _attention(qt, segs, lam_vecs, subln_col, lam_init, *, tq, tk, shifted):
    b, _, lq = qt.shape
    tq = min(tq, lq)
    seg_len = tuple(k.shape[1] for k, _ in segs)
    const = lambda arr: pl.BlockSpec(arr.shape, lambda bi, h, *i: (0, 0))
    if shifted:
        tq = min(tq, SHIFTED_TQ)
        grid = (b, N_HEADS, lq // tq)
        q_spec = pl.BlockSpec((None, HEAD_W, tq), lambda bi, h, i: (bi, h, i))
        o_spec = pl.BlockSpec((None, tq, HEAD_W), lambda bi, h, i: (bi, i, h))
        body = functools.partial(_attn_kernel, n_seg=len(segs), seg_len=seg_len, tk=tk, lam_init=lam_init)
        scratch = [pltpu.VMEM((2, 1, tq), F32), pltpu.VMEM((2, 1, tq), F32), pltpu.VMEM((2, HEAD_W, tq), F32)]
        semantics = ("parallel", "parallel", "arbitrary")
    else:
        grid = (b, N_HEADS)
        q_spec = pl.BlockSpec((None, HEAD_W, lq), lambda bi, h: (bi, h, 0))
        o_spec = pl.BlockSpec((None, lq, HEAD_W), lambda bi, h: (bi, 0, h))
        body = functools.partial(_attn_unshifted_kernel, n_seg=len(segs), seg_len=seg_len, tq=tq, tk=tk,
                                 lam_init=lam_init)
        s_rows = min(tk, max(seg_len))
        scratch = [pltpu.VMEM((8, 2 * tq), F32), pltpu.VMEM((HEAD_W, 2 * tq), F32),
                   pltpu.VMEM((s_rows, 2 * tq), F32), pltpu.VMEM((s_rows, 2 * tq), F32)]
        semantics = ("parallel", "parallel")
    in_specs, args = [q_spec], [qt]
    for (k, vt), lk in zip(segs, seg_len):
        in_specs.append(pl.BlockSpec((None, lk, HEAD_W), lambda bi, h, *i: (bi, 0, h)))
        in_specs.append(pl.BlockSpec((None, HEAD_W, lk), lambda bi, h, *i: (bi, h, 0)))
        args += [k, vt]
    in_specs += [const(lam_vecs), const(subln_col)]
    args += [lam_vecs, subln_col]
    return pl.pallas_call(
        body,
        grid=grid,
        in_specs=in_specs,
        out_specs=o_spec,
        out_shape=jax.ShapeDtypeStruct((b, lq, ATTN_W), BF16),
        scratch_shapes=scratch,
        compiler_params=_cparams(*semantics),
        name="diff_attention_shifted" if shifted else "diff_attention",
    )(*args)


def _attention_dispatch(score_bound, *args, **kw):
    return lax.cond(score_bound <= MAX_UNSHIFTED_SCORE,
                    lambda: _attention(*args, shifted=False, **kw),
                    lambda: _attention(*args, shifted=True, **kw))


def _fourier_kernel(u_ref, v_ref, m1_ref, c2_ref, s2_ref, o_ref, yr_ref, yi_ref, *, n1, n2):
    def stage1(j, carry):
        xu = u_ref[pl.ds(j, n1, stride=n2), :]
        xv = v_ref[pl.ds(j, n1, stride=n2), :]
        x = jnp.concatenate([xu, xv], axis=1).astype(BF16)
        p = jnp.dot(m1_ref[j], x, preferred_element_type=F32)
        rows = pl.ds(pl.multiple_of(j * n1, n1), n1)
        yr_ref[rows, :] = p[:n1, :LANES] + p[n1:, LANES:]
        yi_ref[rows, :] = p[n1:, :LANES] - p[:n1, LANES:]
        return carry

    lax.fori_loop(0, n2, stage1, 0, unroll=FOURIER_UNROLL)
    c2, s2 = c2_ref[...], s2_ref[...]

    def stage2(k1, carry):
        rows = pl.ds(k1, n2, stride=n1)
        xr = (jnp.dot(c2, yr_ref[rows, :].astype(BF16), preferred_element_type=F32)
              + jnp.dot(s2, yi_ref[rows, :].astype(BF16), preferred_element_type=F32))
        o_ref[rows, :] = xr
        return carry

    lax.fori_loop(0, n1, stage2, 0, unroll=FOURIER_UNROLL)


def _fourier_latent(uv, n1, n2):
    b, l, _ = uv.shape
    m1, c2, s2 = _stage_tables(n1, n2)
    m1, c2, s2 = (jnp.asarray(t).astype(BF16) for t in (m1, c2, s2))
    halves = MIX_Q // LANES
    return pl.pallas_call(
        functools.partial(_fourier_kernel, n1=n1, n2=n2),
        grid=(b, halves),
        in_specs=[
            pl.BlockSpec((None, l, LANES), lambda bi, c: (bi, 0, c)),
            pl.BlockSpec((None, l, LANES), lambda bi, c: (bi, 0, c + halves)),
            pl.BlockSpec(m1.shape, lambda bi, c: (0, 0, 0)),
            pl.BlockSpec(c2.shape, lambda bi, c: (0, 0)),
            pl.BlockSpec(s2.shape, lambda bi, c: (0, 0)),
        ],
        out_specs=pl.BlockSpec((None, l, LANES), lambda bi, c: (bi, 0, c)),
        out_shape=jax.ShapeDtypeStruct((b, l, MIX_Q), F32),
        scratch_shapes=[pltpu.VMEM((l, LANES), F32), pltpu.VMEM((l, LANES), F32)],
        compiler_params=_cparams("parallel", "parallel"),
        name="fourier_positions",
    )(uv, uv, m1, c2, s2)


def _fourier_dense_kernel(uv_ref, c_ref, s_ref, o_ref):
    uv = uv_ref[...].astype(BF16)
    o_ref[...] = (jnp.dot(c_ref[...], uv[:, :MIX_Q], preferred_element_type=F32)
                  - jnp.dot(s_ref[...], uv[:, MIX_Q:], preferred_element_type=F32))


def _fourier_dense(uv):
    b, l, _ = uv.shape
    n = np.arange(l)
    ang = 2.0 * np.pi * np.outer(n, n) / l
    c = jnp.asarray((np.cos(ang) / math.sqrt(l)).astype(np.float32)).astype(BF16)
    s = jnp.asarray((np.sin(ang) / math.sqrt(l)).astype(np.float32)).astype(BF16)
    return pl.pallas_call(
        _fourier_dense_kernel,
        grid=(b,),
        in_specs=[pl.BlockSpec((None, l, 2 * MIX_Q), lambda bi: (bi, 0, 0)),
                  pl.BlockSpec((l, l), lambda bi: (0, 0)),
                  pl.BlockSpec((l, l), lambda bi: (0, 0))],
        out_specs=pl.BlockSpec((None, l, MIX_Q), lambda bi: (bi, 0, 0)),
        out_shape=jax.ShapeDtypeStruct((b, l, MIX_Q), F32),
        compiler_params=_cparams("parallel"),
        name="fourier_positions_ctx",
    )(uv, c, s)


POOL_HALO = 8


def _pool_kernel(b_ref, o_ref, pad_ref, *, l, rows):
    zeros = jnp.zeros((POOL_HALO, LANES), F32)
    lane = lax.broadcasted_iota(jnp.int32, (rows, LANES), 1)
    low_group = lane < GROUP_W
    n_chunks = l // rows
    for tile in range(MIX_Q // LANES):
        w_lo, w_hi = POOL_WINDOWS[2 * tile], POOL_WINDOWS[2 * tile + 1]
        pad_ref[0:POOL_HALO, :] = zeros
        pad_ref[POOL_HALO + l:, :] = zeros
        pad_ref[POOL_HALO:POOL_HALO + l, :] = b_ref[:, tile * LANES:(tile + 1) * LANES]

        def chunk(t0, edge, tile=tile, w_lo=w_lo, w_hi=w_hi):
            shifted = lambda dlt: pad_ref[pl.ds(t0 + POOL_HALO + dlt, rows), :]
            lo_range = range(-(w_lo // 2), w_lo // 2)
            inner = functools.reduce(lambda a, b: a + b, [shifted(d) for d in lo_range])
            outer = functools.reduce(lambda a, b: a + b,
                                     [shifted(d) for d in range(-(w_hi // 2), w_hi // 2) if d not in lo_range], inner)
            sums = jnp.where(low_group, inner, outer)
            if edge:
                pos = t0 + lax.broadcasted_iota(jnp.int32, (rows, LANES), 0)
                half = jnp.where(low_group, w_lo // 2, w_hi // 2)
                cnt = jnp.minimum(pos + (half - 1), l - 1) - jnp.maximum(pos - half, 0) + 1
                mean = sums / cnt.astype(F32)
            else:
                mean = sums * jnp.where(low_group, 1.0 / w_lo, 1.0 / w_hi)
            o_ref[pl.ds(t0, rows), tile * LANES:(tile + 1) * LANES] = mean - shifted(0)

        assert rows >= POOL_HALO
        chunk(0, True)
        if n_chunks > 1:
            def body(ci, carry, chunk=chunk):
                chunk(pl.multiple_of(ci * rows, rows), False)
                return carry

            lax.fori_loop(1, n_chunks - 1, body, 0)
            chunk((n_chunks - 1) * rows, True)


def _pool(bp):
    b, l, _ = bp.shape
    rows = 256
    return pl.pallas_call(
        functools.partial(_pool_kernel, l=l, rows=rows),
        grid=(b,),
        in_specs=[pl.BlockSpec((None, l, MIX_Q), lambda bi: (bi, 0, 0))],
        out_specs=pl.BlockSpec((None, l, MIX_Q), lambda bi: (bi, 0, 0)),
        out_shape=jax.ShapeDtypeStruct((b, l, MIX_Q), F32),
        scratch_shapes=[pltpu.VMEM((l + 2 * POOL_HALO, LANES), F32)],
        compiler_params=_cparams("parallel"),
        name="pool_windows",
    )(bp)


def _outproj_kernel(x_ref, f_ref, p_ref, a_ref, sg_ref, mod_ref, w_ref, o_ref, *, d, ctx_row):
    row = ctx_row if ctx_row is not None else pl.program_id(0)
    o_ref[...] = _residual_update(x_ref, f_ref, p_ref, a_ref, sg_ref, mod_ref, w_ref, row, d)


def _out_projection(x, four, pool, attn, sg, mod, w_out, *, layer, ctx_row, tl):
    b, l, d = x.shape
    row_blk = lambda width: pl.BlockSpec((None, tl, width), lambda bi, i: (bi, i, 0))
    of_layer = lambda arr: pl.BlockSpec((None,) + arr.shape[1:], lambda bi, i: (layer,) + (0,) * (arr.ndim - 1))
    return pl.pallas_call(
        functools.partial(_outproj_kernel, d=d, ctx_row=ctx_row),
        grid=(b, l // tl),
        in_specs=[row_blk(d), row_blk(MIX_Q), row_blk(MIX_Q), row_blk(ATTN_W), row_blk(d), of_layer(mod),
                  of_layer(w_out)],
        out_specs=row_blk(d),
        out_shape=jax.ShapeDtypeStruct((b, l, d), F32),
        compiler_params=_cparams("parallel", "parallel"),
        name="out_projection_ctx" if ctx_row is not None else "out_projection",
    )(x, four, pool, attn, sg, mod, w_out)


def kernel(x, c, ctx, c_ctx, norm_g, w_mod, b_mod, w_in, w_fourier, w_pool, pool_scale, qk_norm_g, lam_vecs,
           subln_g, w_out):
    b, l, d = x.shape
    depth = w_in.shape[0]
    ctx_len = ctx.shape[1]
    ctx_row = b

    cc = jnp.concatenate([c, c_ctx[None, :], jnp.zeros((8 - b - 1, d), F32)], axis=0)
    mod = _modulation(cc, w_mod, b_mod)
    w_eff = _fold_weights(w_in, w_fourier, w_pool, pool_scale)
    w_out_bf = _cast_bf16(w_out)
    rope_tabs = _rope_tables(l // GRID_W)
    lane_chunk = lax.broadcasted_iota(jnp.int32, (2 * LANES, 2 * LANES), 0) // QK_DIM
    ones_blk = (lane_chunk == lane_chunk.T).astype(BF16)
    n1 = 64
    n2 = l // n1

    pending = None
    for layer in range(depth):
        lam_init = 0.8 - 0.6 * math.exp(-0.3 * layer)
        update_ctx = layer < depth - 1
        ng = norm_g[layer][None, :]
        qk_gain = jnp.tile(qk_norm_g[layer], (1, 2 * LANES // QK_DIM))
        sub_g = subln_g[layer][:, None]
        lv = lam_vecs[layer]

        outs = _in_projection(x, mod, ng, w_eff, qk_gain, ones_blk, rope_tabs, layer=layer, ctx_row=None,
                              tl=512, prev=pending)
        if pending is not None:
            x, outs = outs[0], outs[1:]
        uv, bp, qt, k, vt, sg = outs
        uv_c, bp_c, qt_c, k_c, vt_c, sg_c = _in_projection(ctx, mod, ng, w_eff, qk_gain, ones_blk, None,
                                                           layer=layer, ctx_row=ctx_row, tl=ctx_len)
        gains = jnp.max(jnp.abs(qk_norm_g[layer]), axis=-1)
        score_bound = QK_DIM * gains[0] * gains[1] * (Q_SCALE * BF16_ROUNDING_SLACK)
        attn = _attention_dispatch(score_bound, qt, [(k_c, vt_c), (k, vt)], lv, sub_g, lam_init, tq=ATTN_TQ,
                                  tk=ATTN_TK)
        four = _fourier_latent(uv, n1, n2)
        pool = _pool(bp)
        if update_ctx:
            attn_c = _attention_dispatch(score_bound, qt_c, [(k_c, vt_c)], lv, sub_g, lam_init, tq=ctx_len, tk=512)
            ctx = _out_projection(ctx, _fourier_dense(uv_c), _pool(bp_c), attn_c, sg_c, mod, w_out_bf,
                                  layer=layer, ctx_row=ctx_row, tl=ctx_len)
        if layer + 1 < depth:
            pending = (four, pool, attn, sg, w_out_bf)
        else:
            x = _out_projection(x, four, pool, attn, sg, mod, w_out_bf, layer=layer, ctx_row=None, tl=1024)
    return x
```

```python
import functools
import math

import numpy as np
import jax
import jax.numpy as jnp
from jax import lax
from jax.experimental import pallas as pl
from jax.experimental.pallas import tpu as pltpu

GRID_W = 64
FOURIER_GROUPS = 4
GROUP_W = 64
MIX_Q = 256
POOL_WINDOWS = (2, 4, 8, 16)
QK_DIM = 64
HEAD_W = 2 * QK_DIM
N_HEADS = 4
ATTN_W = N_HEADS * HEAD_W
ROPE_THETA = 10000.0
EPS = 1e-6
Q_SCALE = QK_DIM ** -0.5 * math.log2(math.e)
FOURIER_UNROLL = 8
ATTN_TK = 512
ATTN_TQ = 2048
SHIFTED_TQ = 512
MAX_UNSHIFTED_SCORE = 40.0
BF16_ROUNDING_SLACK = 1.02
U_OFF, V_OFF, P_OFF, Q_OFF, K_OFF, VV_OFF, G_OFF, W_EFF = 0, 256, 512, 768, 1280, 1792, 2304, 3328

LANES = 128
VMEM_LIMIT = 56 * 1024 * 1024
HI = lax.Precision.HIGHEST
F32 = jnp.float32
BF16 = jnp.bfloat16


def _cparams(*sem):
    return pltpu.CompilerParams(dimension_semantics=sem, vmem_limit_bytes=VMEM_LIMIT)


def _channel_dft():
    c = np.arange(GROUP_W)
    ang = 2.0 * np.pi * np.outer(c, c) / GROUP_W
    s = 1.0 / math.sqrt(GROUP_W)
    return np.cos(ang) * s, np.sin(ang) * s


def _stage_tables(n1, n2):
    n = n1 * n2
    k1 = np.arange(n1)
    i1 = np.arange(n1)
    j = np.arange(n2)
    theta = 2.0 * np.pi * (j[:, None, None] * k1[None, :, None] / n + k1[None, :, None] * i1[None, None, :] / n1)
    m1 = np.concatenate([np.cos(theta), -np.sin(theta)], axis=1) / math.sqrt(n)
    ang2 = 2.0 * np.pi * np.outer(j, j) / n2
    return m1.astype(np.float32), np.cos(ang2).astype(np.float32), np.sin(ang2).astype(np.float32)


def _rope_tables(rows):
    f32 = np.float32
    row = np.repeat(np.arange(rows), GRID_W).astype(f32)
    col = np.tile(np.arange(GRID_W), rows).astype(f32)
    half = QK_DIM // 2
    inv_freq = np.power(f32(ROPE_THETA), -np.arange(0, half, 2, dtype=f32) / f32(half))
    ang_r = row[:, None] * inv_freq[None, :]
    ang_c = col[:, None] * inv_freq[None, :]
    ang = np.concatenate([ang_r, ang_r, ang_c, ang_c], axis=-1)
    cos, sin = np.cos(ang), np.sin(ang)
    quarter = QK_DIM // 4
    first = (np.arange(QK_DIM) % half) < quarter
    sin_up = np.where(first, -sin, f32(0.0))
    sin_dn = np.where(first, f32(0.0), sin)
    rep = LANES // QK_DIM
    return tuple(jnp.asarray(np.tile(t, (1, rep)).astype(f32)) for t in (cos, sin_up, sin_dn))


def _fold_kernel(w_ref, wf_ref, wp_ref, ps_ref, cd_ref, sd_ref, o_ref, bd_ref):
    @pl.when(pl.program_id(1) == 0)
    def _():
        bd_ref[...] = jnp.zeros(bd_ref.shape, F32)
        cd, sd = cd_ref[...], sd_ref[...]
        for g in range(FOURIER_GROUPS):
            sl = slice(g * GROUP_W, (g + 1) * GROUP_W)
            bd_ref[0, sl, sl] = jnp.dot(cd, wf_ref[g], precision=HI, preferred_element_type=F32)
            bd_ref[1, sl, sl] = jnp.dot(sd, wf_ref[g], precision=HI, preferred_element_type=F32)
            bd_ref[2, sl, sl] = wp_ref[g] * ps_ref[:, sl]

    wa, wb = w_ref[:, :MIX_Q], w_ref[:, MIX_Q:2 * MIX_Q]
    for idx, (src, off) in enumerate(((wa, U_OFF), (wa, V_OFF), (wb, P_OFF))):
        o_ref[:, off:off + MIX_Q] = jnp.dot(src, bd_ref[idx], precision=HI,
                                            preferred_element_type=F32).astype(o_ref.dtype)
    o_ref[:, Q_OFF:] = w_ref[:, 2 * MIX_Q:].astype(o_ref.dtype)


def _fold_weights(w_in, w_fourier, w_pool, pool_scale):
    depth, d, in_w = w_in.shape
    tr = 256
    cd, sd = _channel_dft()
    return pl.pallas_call(
        _fold_kernel,
        grid=(depth, d // tr),
        in_specs=[
            pl.BlockSpec((None, tr, in_w), lambda l, i: (l, i, 0)),
            pl.BlockSpec((None, FOURIER_GROUPS, GROUP_W, GROUP_W), lambda l, i: (l, 0, 0, 0)),
            pl.BlockSpec((None, FOURIER_GROUPS, GROUP_W, GROUP_W), lambda l, i: (l, 0, 0, 0)),
            pl.BlockSpec((None, 1, MIX_Q), lambda l, i: (l, 0, 0)),
            pl.BlockSpec((GROUP_W, GROUP_W), lambda l, i: (0, 0)),
            pl.BlockSpec((GROUP_W, GROUP_W), lambda l, i: (0, 0)),
        ],
        out_specs=pl.BlockSpec((None, tr, W_EFF), lambda l, i: (l, i, 0)),
        out_shape=jax.ShapeDtypeStruct((depth, d, W_EFF), BF16),
        scratch_shapes=[pltpu.VMEM((3, MIX_Q, MIX_Q), F32)],
        compiler_params=_cparams("parallel", "arbitrary"),
        name="fold_weights",
    )(w_in, w_fourier, w_pool, pool_scale[:, None, :], jnp.asarray(cd, F32), jnp.asarray(sd, F32))


def _cast_kernel(w_ref, o_ref):
    o_ref[...] = w_ref[...].astype(o_ref.dtype)


def _cast_bf16(w):
    depth, r, c = w.shape
    return pl.pallas_call(
        _cast_kernel,
        grid=(depth,),
        in_specs=[pl.BlockSpec((None, r, c), lambda l: (l, 0, 0))],
        out_specs=pl.BlockSpec((None, r, c), lambda l: (l, 0, 0)),
        out_shape=jax.ShapeDtypeStruct(w.shape, BF16),
        compiler_params=_cparams("parallel"),
        name="cast_w_out",
    )(w)


def _mod_kernel(c_ref, w_ref, b_ref, o_ref):
    cc = c_ref[...]
    sc = cc * jax.nn.sigmoid(cc)
    o_ref[...] = jnp.dot(sc, w_ref[...], precision=HI, preferred_element_type=F32) + b_ref[...]


def _modulation(cc, w_mod, b_mod):
    depth, d, n = w_mod.shape
    tn = 1024
    rows = cc.shape[0]
    return pl.pallas_call(
        _mod_kernel,
        grid=(depth, n // tn),
        in_specs=[
            pl.BlockSpec((rows, d), lambda l, j: (0, 0)),
            pl.BlockSpec((None, d, tn), lambda l, j: (l, 0, j)),
            pl.BlockSpec((None, 1, tn), lambda l, j: (l, 0, j)),
        ],
        out_specs=pl.BlockSpec((None, rows, tn), lambda l, j: (l, 0, j)),
        out_shape=jax.ShapeDtypeStruct((depth, rows, n), F32),
        compiler_params=_cparams("parallel", "parallel"),
        name="modulation",
    )(cc, w_mod, b_mod[:, None, :])


def _qk_normalise(t, ones_blk, gain):
    ss = jnp.dot((t * t).astype(BF16), ones_blk, preferred_element_type=F32)
    return t * lax.rsqrt(ss * (1.0 / QK_DIM) + EPS) * gain


def _rotate(t, cos, sin_up, sin_dn):
    quarter = QK_DIM // 4
    up = pltpu.roll(t, LANES - quarter, 1)
    dn = pltpu.roll(t, quarter, 1)
    return t * cos + up * sin_up + dn * sin_dn


def _residual_update(x_ref, f_ref, p_ref, a_ref, sg_ref, mod_ref, w_ref, row, d, rows=slice(None)):
    gate = mod_ref[pl.ds(row, 1), 2 * d:]
    sg = sg_ref[rows, :].astype(F32)
    yf = (f_ref[rows, :] * sg[:, :MIX_Q]).astype(BF16)
    yp = (p_ref[rows, :] * sg[:, MIX_Q:2 * MIX_Q]).astype(BF16)
    ya = (a_ref[rows, :].astype(F32) * sg[:, 2 * MIX_Q:]).astype(BF16)
    acc = jnp.dot(yf, w_ref[0:MIX_Q, :], preferred_element_type=F32)
    acc += jnp.dot(yp, w_ref[MIX_Q:2 * MIX_Q, :], preferred_element_type=F32)
    acc += jnp.dot(ya, w_ref[2 * MIX_Q:, :], preferred_element_type=F32)
    return x_ref[rows, :] + gate * acc


def _inproj_kernel(x_ref, *rest, d, ctx_row, rope, fused):
    row = ctx_row if ctx_row is not None else pl.program_id(0)
    if fused:
        prev, rest = rest[:6], rest[6:]
    mod_ref, ng_ref, w_ref, qkg_ref, ones_ref = rest[:5]
    rest = rest[5:]
    if rope:
        cos_ref, sup_ref, sdn_ref = rest[:3]
        rest = rest[3:]
    mod = mod_ref[pl.ds(row, 1), :]
    shift, gain = mod[:, :d], ng_ref[...] * (1.0 + mod[:, d:2 * d])

    def modulated_norm(xf):
        ms = jnp.mean(xf * xf, axis=-1, keepdims=True)
        return (xf * lax.rsqrt(ms + EPS) * gain + shift).astype(BF16)

    if fused:
        xnew_ref, rest = rest[0], rest[1:]
    half = x_ref.shape[0] // 2
    h_parts = []
    for part in range(2):
        rows = slice(part * half, (part + 1) * half)
        if fused:
            xf = _residual_update(x_ref, *prev, row, d, rows=rows)
            xnew_ref[rows, :] = xf
        else:
            xf = x_ref[rows, :]
        h_parts.append(modulated_norm(xf))
    h = jnp.concatenate(h_parts, axis=0)
    uv_ref, bp_ref, qt_ref, k_ref, vt_ref, sg_ref = rest

    def proj(lo, hi, by_parts=False):
        if by_parts:
            return jnp.concatenate([jnp.dot(hp, w_ref[:, lo:hi], preferred_element_type=F32) for hp in h_parts],
                                   axis=0)
        return jnp.dot(h, w_ref[:, lo:hi], preferred_element_type=F32)

    pair = 2 * LANES
    n_pairs = ATTN_W // pair
    ones_blk = ones_ref[...]
    q_raw = [proj(Q_OFF + p * pair, Q_OFF + (p + 1) * pair, by_parts=True) for p in range(n_pairs)]
    k_raw = [proj(K_OFF + p * pair, K_OFF + (p + 1) * pair) for p in range(n_pairs)]
    q_nrm = [_qk_normalise(t, ones_blk, qkg_ref[0:1, :]) for t in q_raw]
    k_nrm = [_qk_normalise(t, ones_blk, qkg_ref[1:2, :]) for t in k_raw]
    v_raw = [proj(VV_OFF + p * pair, VV_OFF + (p + 1) * pair) for p in range(n_pairs)]
    g = proj(G_OFF, W_EFF)
    sg_ref[...] = (g * jax.nn.sigmoid(g)).astype(sg_ref.dtype)
    uv_ref[...] = proj(U_OFF, P_OFF)
    bp_ref[...] = proj(P_OFF, Q_OFF)

    if rope:
        cos, sup, sdn = cos_ref[...], sup_ref[...], sdn_ref[...]
    for p in range(n_pairs):
        qp, kp, vp = q_nrm[p], k_nrm[p], v_raw[p]
        for half in range(2):
            sl = slice(p * pair + half * LANES, p * pair + (half + 1) * LANES)
            hs = slice(half * LANES, (half + 1) * LANES)
            qt, kt = qp[:, hs], kp[:, hs]
            if rope:
                qt = _rotate(qt, cos, sup, sdn)
                kt = _rotate(kt, cos, sup, sdn)
            qt_ref[sl, :] = (qt * Q_SCALE).T.astype(qt_ref.dtype)
            k_ref[:, sl] = kt.astype(k_ref.dtype)
            vt_ref[sl, :] = vp[:, hs].T.astype(vt_ref.dtype)


def _in_projection(x, mod, norm_g, w_eff, qk_gain, ones_blk, rope_tabs, *, layer, ctx_row, tl, prev=None):
    b, l, d = x.shape
    rope = rope_tabs is not None
    fused = prev is not None
    row_blk = lambda width: pl.BlockSpec((None, tl, width), lambda bi, i: (bi, i, 0))
    full = lambda arr: pl.BlockSpec(arr.shape, lambda bi, i: (0,) * arr.ndim)

    def of_layer(arr, which=layer):
        return pl.BlockSpec((None,) + arr.shape[1:], lambda bi, i: (which,) + (0,) * (arr.ndim - 1))

    in_specs, args = [row_blk(d)], [x]
    if fused:
        four, pool, attn, sg_prev, w_out = prev
        in_specs += [row_blk(MIX_Q), row_blk(MIX_Q), row_blk(ATTN_W), row_blk(d), of_layer(mod, layer - 1),
                     of_layer(w_out, layer - 1)]
        args += [four, pool, attn, sg_prev, mod, w_out]
    in_specs += [of_layer(mod), full(norm_g), of_layer(w_eff), full(qk_gain), full(ones_blk)]
    args += [mod, norm_g, w_eff, qk_gain, ones_blk]
    if rope:
        in_specs += [pl.BlockSpec((tl, LANES), lambda bi, i: (i, 0))] * 3
        args += list(rope_tabs)
    out_shape = (
        jax.ShapeDtypeStruct((b, l, 2 * MIX_Q), F32),
        jax.ShapeDtypeStruct((b, l, MIX_Q), F32),
        jax.ShapeDtypeStruct((b, ATTN_W, l), BF16),
        jax.ShapeDtypeStruct((b, l, ATTN_W), BF16),
        jax.ShapeDtypeStruct((b, ATTN_W, l), BF16),
        jax.ShapeDtypeStruct((b, l, d), BF16),
    )
    col_blk = pl.BlockSpec((None, ATTN_W, tl), lambda bi, i: (bi, 0, i))
    out_specs = (row_blk(2 * MIX_Q), row_blk(MIX_Q), col_blk, row_blk(ATTN_W), col_blk, row_blk(d))
    if fused:
        out_shape = (jax.ShapeDtypeStruct((b, l, d), F32),) + out_shape
        out_specs = (row_blk(d),) + out_specs
    name = "in_projection_ctx" if ctx_row is not None else "in_projection"
    return pl.pallas_call(
        functools.partial(_inproj_kernel, d=d, ctx_row=ctx_row, rope=rope, fused=fused),
        grid=(b, l // tl),
        in_specs=in_specs,
        out_specs=out_specs,
        out_shape=out_shape,
        compiler_params=_cparams("parallel", "parallel"),
        name="residual_" + name if fused else name,
    )(*args)


def _attn_kernel(*refs, n_seg, seg_len, tk, lam_init):
    qt_ref = refs[0]
    kv_refs = refs[1:1 + 2 * n_seg]
    lam_ref, sub_ref, o_ref, m_ref, l_ref, acc_ref = refs[1 + 2 * n_seg:]
    qt = qt_ref[...]
    first_map = lax.broadcasted_iota(jnp.int32, qt.shape, 0) < QK_DIM
    zero = jnp.zeros_like(qt)
    ws = (jnp.where(first_map, qt, zero), jnp.where(first_map, zero, qt))

    m_ref[...] = jnp.full(m_ref.shape, -jnp.inf, F32)
    l_ref[...] = jnp.zeros(l_ref.shape, F32)
    acc_ref[...] = jnp.zeros(acc_ref.shape, F32)

    def chunk(k_ref, vt_ref, start, size):
        k = k_ref[pl.ds(start, size), :]
        vt = vt_ref[:, pl.ds(start, size)]
        for mp in range(2):
            s = jnp.dot(k, ws[mp], preferred_element_type=F32)
            m_old = m_ref[mp]
            m_new = jnp.maximum(m_old, jnp.max(s, axis=0, keepdims=True))
            alpha = jnp.exp2(m_old - m_new)
            e = jnp.exp2(s - m_new)
            l_ref[mp] = alpha * l_ref[mp] + jnp.sum(e, axis=0, keepdims=True)
            acc_ref[mp] = alpha * acc_ref[mp] + jnp.dot(vt, e.astype(BF16), preferred_element_type=F32)
            m_ref[mp] = m_new

    for si in range(n_seg):
        k_ref, vt_ref = kv_refs[2 * si], kv_refs[2 * si + 1]
        length = seg_len[si]
        size = min(tk, length)
        n_chunks = length // size
        if n_chunks == 1:
            chunk(k_ref, vt_ref, 0, size)
        else:
            def body(j, carry, k_ref=k_ref, vt_ref=vt_ref, size=size):
                chunk(k_ref, vt_ref, pl.multiple_of(j * size, size), size)
                return carry
            lax.fori_loop(0, n_chunks, body, 0)

    lv = lam_ref[...]
    lam = (jnp.exp(jnp.sum(lv[0:1] * lv[1:2], axis=-1, keepdims=True))
           - jnp.exp(jnp.sum(lv[2:3] * lv[3:4], axis=-1, keepdims=True)) + lam_init)
    o = acc_ref[0] / l_ref[0] - lam * (acc_ref[1] / l_ref[1])
    ms = jnp.mean(o * o, axis=0, keepdims=True)
    o = o * lax.rsqrt(ms + EPS) * (sub_ref[...] * (1.0 - lam_init))
    o_ref[...] = o.T.astype(o_ref.dtype)


def _attn_unshifted_kernel(*refs, n_seg, seg_len, tq, tk, lam_init):
    qt_ref = refs[0]
    kv_refs = refs[1:1 + 2 * n_seg]
    lam_ref, sub_ref, o_ref, l_ref, acc_ref, sa_ref, sb_ref = refs[1 + 2 * n_seg:]
    first_map = lax.broadcasted_iota(jnp.int32, (HEAD_W, tq), 0) < QK_DIM
    lv = lam_ref[...]
    lam = (jnp.exp(jnp.sum(lv[0:1] * lv[1:2], axis=-1, keepdims=True))
           - jnp.exp(jnp.sum(lv[2:3] * lv[3:4], axis=-1, keepdims=True)) + lam_init)
    out_gain = sub_ref[...] * (1.0 - lam_init)

    chunks = []
    for i in range(n_seg):
        size = min(tk, seg_len[i])
        assert seg_len[i] % size == 0
        chunks += [(kv_refs[2 * i], kv_refs[2 * i + 1], c * size, size) for c in range(seg_len[i] // size)]
    bufs = (sa_ref, sb_ref)

    def query_tile(qi, carry):
        q0 = pl.multiple_of(qi * tq, tq)
        qt = qt_ref[:, pl.ds(q0, tq)]
        zero = jnp.zeros_like(qt)
        w = jnp.concatenate([jnp.where(first_map, qt, zero), jnp.where(first_map, zero, qt)], axis=1)
        l_ref[...] = jnp.zeros(l_ref.shape, F32)
        acc_ref[...] = jnp.zeros(acc_ref.shape, F32)

        def scores(s_ref, k_ref, start, size):
            s_ref[0:size, :] = jnp.dot(k_ref[pl.ds(start, size), :], w, preferred_element_type=F32)

        def consume(s_ref, vt_ref, start, size):
            e = jnp.exp2(s_ref[0:size, :])
            l_ref[...] += jnp.sum(e.reshape(size // 8, 8, 2 * tq), axis=0)
            acc_ref[...] += jnp.dot(vt_ref[:, pl.ds(start, size)], e.astype(BF16), preferred_element_type=F32)

        scores(bufs[0], chunks[0][0], chunks[0][2], chunks[0][3])
        for i, (_, vt_ref, start, size) in enumerate(chunks):
            if i + 1 < len(chunks):
                nxt = chunks[i + 1]
                scores(bufs[(i + 1) % 2], nxt[0], nxt[2], nxt[3])
            consume(bufs[i % 2], vt_ref, start, size)

        inv = 1.0 / jnp.sum(l_ref[...], axis=0, keepdims=True)
        acc = acc_ref[...]
        o = acc[:, :tq] * inv[:, :tq] - (lam * inv[:, tq:]) * acc[:, tq:]
        ms = jnp.mean(o * o, axis=0, keepdims=True)
        o = o * lax.rsqrt(ms + EPS) * out_gain
        o_ref[pl.ds(q0, tq), :] = o.T.astype(o_ref.dtype)
        return carry

    lax.fori_loop(0, qt_ref.shape[1] // tq, query_tile, 0)


def _attention(qt, segs, lam_vecs, subln_col, lam_init, *, tq, tk, shifted):
    b, _, lq = qt.shape
    tq = min(tq, lq)
    seg_len = tuple(k.shape[1] for k, _ in segs)
    const = lambda arr: pl.BlockSpec(arr.shape, lambda bi, h, *i: (0, 0))
    if shifted:
        tq = min(tq, SHIFTED_TQ)
        grid = (b, N_HEADS, lq // tq)
        q_spec = pl.BlockSpec((None, HEAD_W, tq), lambda bi, h, i: (bi, h, i))
        o_spec = pl.BlockSpec((None, tq, HEAD_W), lambda bi, h, i: (bi, i, h))
        body = functools.partial(_attn_kernel, n_seg=len(segs), seg_len=seg_len, tk=tk, lam_init=lam_init)
        scratch = [pltpu.VMEM((2, 1, tq), F32), pltpu.VMEM((2, 1, tq), F32), pltpu.VMEM((2, HEAD_W, tq), F32)]
        semantics = ("parallel", "parallel", "arbitrary")
    else:
        grid = (b, N_HEADS)
        q_spec = pl.BlockSpec((None, HEAD_W, lq), lambda bi, h: (bi, h, 0))
        o_spec = pl.BlockSpec((None, lq, HEAD_W), lambda bi, h: (bi, 0, h))
        body = functools.partial(_attn_unshifted_kernel, n_seg=len(segs), seg_len=seg_len, tq=tq, tk=tk,
                                 lam_init=lam_init)
        s_rows = min(tk, max(seg_len))
        scratch = [pltpu.VMEM((8, 2 * tq), F32), pltpu.VMEM((HEAD_W, 2 * tq), F32),
                   pltpu.VMEM((s_rows, 2 * tq), F32), pltpu.VMEM((s_rows, 2 * tq), F32)]
        semantics = ("parallel", "parallel")
    in_specs, args = [q_spec], [qt]
    for (k, vt), lk in zip(segs, seg_len):
        in_specs.append(pl.BlockSpec((None, lk, HEAD_W), lambda bi, h, *i: (bi, 0, h)))
        in_specs.append(pl.BlockSpec((None, HEAD_W, lk), lambda bi, h, *i: (bi, h, 0)))
        args += [k, vt]
    in_specs += [const(lam_vecs), const(subln_col)]
    args += [lam_vecs, subln_col]
    return pl.pallas_call(
        body,
        grid=grid,
        in_specs=in_specs,
        out_specs=o_spec,
        out_shape=jax.ShapeDtypeStruct((b, lq, ATTN_W), BF16),
        scratch_shapes=scratch,
        compiler_params=_cparams(*semantics),
        name="diff_attention_shifted" if shifted else "diff_attention",
    )(*args)


def _attention_dispatch(score_bound, *args, **kw):
    return lax.cond(score_bound <= MAX_UNSHIFTED_SCORE,
                    lambda: _attention(*args, shifted=False, **kw),
                    lambda: _attention(*args, shifted=True, **kw))


def _fourier_kernel(u_ref, v_ref, m1_ref, c2_ref, s2_ref, o_ref, yr_ref, yi_ref, *, n1, n2):
    def stage1(j, carry):
        xu = u_ref[pl.ds(j, n1, stride=n2), :]
        xv = v_ref[pl.ds(j, n1, stride=n2), :]
        x = jnp.concatenate([xu, xv], axis=1).astype(BF16)
        p = jnp.dot(m1_ref[j], x, preferred_element_type=F32)
        rows = pl.ds(pl.multiple_of(j * n1, n1), n1)
        yr_ref[rows, :] = p[:n1, :LANES] + p[n1:, LANES:]
        yi_ref[rows, :] = p[n1:, :LANES] - p[:n1, LANES:]
        return carry

    lax.fori_loop(0, n2, stage1, 0, unroll=FOURIER_UNROLL)
    c2, s2 = c2_ref[...], s2_ref[...]

    def stage2(k1, carry):
        rows = pl.ds(k1, n2, stride=n1)
        xr = (jnp.dot(c2, yr_ref[rows, :].astype(BF16), preferred_element_type=F32)
              + jnp.dot(s2, yi_ref[rows, :].astype(BF16), preferred_element_type=F32))
        o_ref[rows, :] = xr
        return carry

    lax.fori_loop(0, n1, stage2, 0, unroll=FOURIER_UNROLL)


def _fourier_latent(uv, n1, n2):
    b, l, _ = uv.shape
    m1, c2, s2 = _stage_tables(n1, n2)
    m1, c2, s2 = (jnp.asarray(t).astype(BF16) for t in (m1, c2, s2))
    halves = MIX_Q // LANES
    return pl.pallas_call(
        functools.partial(_fourier_kernel, n1=n1, n2=n2),
        grid=(b, halves),
        in_specs=[
            pl.BlockSpec((None, l, LANES), lambda bi, c: (bi, 0, c)),
            pl.BlockSpec((None, l, LANES), lambda bi, c: (bi, 0, c + halves)),
            pl.BlockSpec(m1.shape, lambda bi, c: (0, 0, 0)),
            pl.BlockSpec(c2.shape, lambda bi, c: (0, 0)),
            pl.BlockSpec(s2.shape, lambda bi, c: (0, 0)),
        ],
        out_specs=pl.BlockSpec((None, l, LANES), lambda bi, c: (bi, 0, c)),
        out_shape=jax.ShapeDtypeStruct((b, l, MIX_Q), F32),
        scratch_shapes=[pltpu.VMEM((l, LANES), F32), pltpu.VMEM((l, LANES), F32)],
        compiler_params=_cparams("parallel", "parallel"),
        name="fourier_positions",
    )(uv, uv, m1, c2, s2)


def _fourier_dense_kernel(uv_ref, c_ref, s_ref, o_ref):
    uv = uv_ref[...].astype(BF16)
    o_ref[...] = (jnp.dot(c_ref[...], uv[:, :MIX_Q], preferred_element_type=F32)
                  - jnp.dot(s_ref[...], uv[:, MIX_Q:], preferred_element_type=F32))


def _fourier_dense(uv):
    b, l, _ = uv.shape
    n = np.arange(l)
    ang = 2.0 * np.pi * np.outer(n, n) / l
    c = jnp.asarray((np.cos(ang) / math.sqrt(l)).astype(np.float32)).astype(BF16)
    s = jnp.asarray((np.sin(ang) / math.sqrt(l)).astype(np.float32)).astype(BF16)
    return pl.pallas_call(
        _fourier_dense_kernel,
        grid=(b,),
        in_specs=[pl.BlockSpec((None, l, 2 * MIX_Q), lambda bi: (bi, 0, 0)),
                  pl.BlockSpec((l, l), lambda bi: (0, 0)),
                  pl.BlockSpec((l, l), lambda bi: (0, 0))],
        out_specs=pl.BlockSpec((None, l, MIX_Q), lambda bi: (bi, 0, 0)),
        out_shape=jax.ShapeDtypeStruct((b, l, MIX_Q), F32),
        compiler_params=_cparams("parallel"),
        name="fourier_positions_ctx",
    )(uv, c, s)


POOL_HALO = 8


def _pool_kernel(b_ref, o_ref, pad_ref, *, l, rows):
    zeros = jnp.zeros((POOL_HALO, LANES), F32)
    lane = lax.broadcasted_iota(jnp.int32, (rows, LANES), 1)
    low_group = lane < GROUP_W
    n_chunks = l // rows
    for tile in range(MIX_Q // LANES):
        w_lo, w_hi = POOL_WINDOWS[2 * tile], POOL_WINDOWS[2 * tile + 1]
        pad_ref[0:POOL_HALO, :] = zeros
        pad_ref[POOL_HALO + l:, :] = zeros
        pad_ref[POOL_HALO:POOL_HALO + l, :] = b_ref[:, tile * LANES:(tile + 1) * LANES]

        def chunk(t0, edge, tile=tile, w_lo=w_lo, w_hi=w_hi):
            shifted = lambda dlt: pad_ref[pl.ds(t0 + POOL_HALO + dlt, rows), :]
            lo_range = range(-(w_lo // 2), w_lo // 2)
            inner = functools.reduce(lambda a, b: a + b, [shifted(d) for d in lo_range])
            outer = functools.reduce(lambda a, b: a + b,
                                     [shifted(d) for d in range(-(w_hi // 2), w_hi // 2) if d not in lo_range], inner)
            sums = jnp.where(low_group, inner, outer)
            if edge:
                pos = t0 + lax.broadcasted_iota(jnp.int32, (rows, LANES), 0)
                half = jnp.where(low_group, w_lo // 2, w_hi // 2)
                cnt = jnp.minimum(pos + (half - 1), l - 1) - jnp.maximum(pos - half, 0) + 1
                mean = sums / cnt.astype(F32)
            else:
                mean = sums * jnp.where(low_group, 1.0 / w_lo, 1.0 / w_hi)
            o_ref[pl.ds(t0, rows), tile * LANES:(tile + 1) * LANES] = mean - shifted(0)

        assert rows >= POOL_HALO
        chunk(0, True)
        if n_chunks > 1:
            def body(ci, carry, chunk=chunk):
                chunk(pl.multiple_of(ci * rows, rows), False)
                return carry

            lax.fori_loop(1, n_chunks - 1, body, 0)
            chunk((n_chunks - 1) * rows, True)


def _pool(bp):
    b, l, _ = bp.shape
    rows = 256
    return pl.pallas_call(
        functools.partial(_pool_kernel, l=l, rows=rows),
        grid=(b,),
        in_specs=[pl.BlockSpec((None, l, MIX_Q), lambda bi: (bi, 0, 0))],
        out_specs=pl.BlockSpec((None, l, MIX_Q), lambda bi: (bi, 0, 0)),
        out_shape=jax.ShapeDtypeStruct((b, l, MIX_Q), F32),
        scratch_shapes=[pltpu.VMEM((l + 2 * POOL_HALO, LANES), F32)],
        compiler_params=_cparams("parallel"),
        name="pool_windows",
    )(bp)


def _outproj_kernel(x_ref, f_ref, p_ref, a_ref, sg_ref, mod_ref, w_ref, o_ref, *, d, ctx_row):
    row = ctx_row if ctx_row is not None else pl.program_id(0)
    o_ref[...] = _residual_update(x_ref, f_ref, p_ref, a_ref, sg_ref, mod_ref, w_ref, row, d)


def _out_projection(x, four, pool, attn, sg, mod, w_out, *, layer, ctx_row, tl):
    b, l, d = x.shape
    row_blk = lambda width: pl.BlockSpec((None, tl, width), lambda bi, i: (bi, i, 0))
    of_layer = lambda arr: pl.BlockSpec((None,) + arr.shape[1:], lambda bi, i: (layer,) + (0,) * (arr.ndim - 1))
    return pl.pallas_call(
        functools.partial(_outproj_kernel, d=d, ctx_row=ctx_row),
        grid=(b, l // tl),
        in_specs=[row_blk(d), row_blk(MIX_Q), row_blk(MIX_Q), row_blk(ATTN_W), row_blk(d), of_layer(mod),
                  of_layer(w_out)],
        out_specs=row_blk(d),
        out_shape=jax.ShapeDtypeStruct((b, l, d), F32),
        compiler_params=_cparams("parallel", "parallel"),
        name="out_projection_ctx" if ctx_row is not None else "out_projection",
    )(x, four, pool, attn, sg, mod, w_out)


def kernel(x, c, ctx, c_ctx, norm_g, w_mod, b_mod, w_in, w_fourier, w_pool, pool_scale, qk_norm_g, lam_vecs,
           subln_g, w_out):
    b, l, d = x.shape
    depth = w_in.shape[0]
    ctx_len = ctx.shape[1]
    ctx_row = b

    cc = jnp.concatenate([c, c_ctx[None, :], jnp.zeros((8 - b - 1, d), F32)], axis=0)
    mod = _modulation(cc, w_mod, b_mod)
    w_eff = _fold_weights(w_in, w_fourier, w_pool, pool_scale)
    w_out_bf = _cast_bf16(w_out)
    rope_tabs = _rope_tables(l // GRID_W)
    lane_chunk = lax.broadcasted_iota(jnp.int32, (2 * LANES, 2 * LANES), 0) // QK_DIM
    ones_blk = (lane_chunk == lane_chunk.T).astype(BF16)
    n1 = 64
    n2 = l // n1

    pending = None
    for layer in range(depth):
        lam_init = 0.8 - 0.6 * math.exp(-0.3 * layer)
        update_ctx = layer < depth - 1
        ng = norm_g[layer][None, :]
        qk_gain = jnp.tile(qk_norm_g[layer], (1, 2 * LANES // QK_DIM))
        sub_g = subln_g[layer][:, None]
        lv = lam_vecs[layer]

        outs = _in_projection(x, mod, ng, w_eff, qk_gain, ones_blk, rope_tabs, layer=layer, ctx_row=None,
                              tl=512, prev=pending)
        if pending is not None:
            x, outs = outs[0], outs[1:]
        uv, bp, qt, k, vt, sg = outs
        uv_c, bp_c, qt_c, k_c, vt_c, sg_c = _in_projection(ctx, mod, ng, w_eff, qk_gain, ones_blk, None,
                                                           layer=layer, ctx_row=ctx_row, tl=ctx_len)
        gains = jnp.max(jnp.abs(qk_norm_g[layer]), axis=-1)
        score_bound = QK_DIM * gains[0] * gains[1] * (Q_SCALE * BF16_ROUNDING_SLACK)
        attn = _attention_dispatch(score_bound, qt, [(k_c, vt_c), (k, vt)], lv, sub_g, lam_init, tq=ATTN_TQ,
                                  tk=ATTN_TK)
        four = _fourier_latent(uv, n1, n2)
        pool = _pool(bp)
        if update_ctx:
            attn_c = _attention_dispatch(score_bound, qt_c, [(k_c, vt_c)], lv, sub_g, lam_init, tq=ctx_len, tk=512)
            ctx = _out_projection(ctx, _fourier_dense(uv_c), _pool(bp_c), attn_c, sg_c, mod, w_out_bf,
                                  layer=layer, ctx_row=ctx_row, tl=ctx_len)
        if layer + 1 < depth:
            pending = (four, pool, attn, sg, w_out_bf)
        else:
            x = _out_projection(x, four, pool, attn, sg, mod, w_out_bf, layer=layer, ctx_row=None, tl=1024)
    return x
```

```python
import functools
import math

import numpy as np
import jax
import jax.numpy as jnp
from jax import lax
from jax.experimental import pallas as pl
from jax.experimental.pallas import tpu as pltpu

GRID_W = 64
FOURIER_GROUPS = 4
GROUP_W = 64
MIX_Q = 256
POOL_WINDOWS = (2, 4, 8, 16)
QK_DIM = 64
HEAD_W = 2 * QK_DIM
N_HEADS = 4
ATTN_W = N_HEADS * HEAD_W
ROPE_THETA = 10000.0
EPS = 1e-6
Q_SCALE = QK_DIM ** -0.5 * math.log2(math.e)
FOURIER_UNROLL = 8
ATTN_TK = 512
ATTN_TQ = 2048
SHIFTED_TQ = 512
MAX_UNSHIFTED_SCORE = 40.0
BF16_ROUNDING_SLACK = 1.02
U_OFF, V_OFF, P_OFF, Q_OFF, K_OFF, VV_OFF, G_OFF, W_EFF = 0, 256, 512, 768, 1280, 1792, 2304, 3328

LANES = 128
VMEM_LIMIT = 56 * 1024 * 1024
HI = lax.Precision.HIGHEST
F32 = jnp.float32
BF16 = jnp.bfloat16


def _cparams(*sem):
    return pltpu.CompilerParams(dimension_semantics=sem, vmem_limit_bytes=VMEM_LIMIT)


def _channel_dft():
    c = np.arange(GROUP_W)
    ang = 2.0 * np.pi * np.outer(c, c) / GROUP_W
    s = 1.0 / math.sqrt(GROUP_W)
    return np.cos(ang) * s, np.sin(ang) * s


def _stage_tables(n1, n2):
    n = n1 * n2
    k1 = np.arange(n1)
    i1 = np.arange(n1)
    j = np.arange(n2)
    theta = 2.0 * np.pi * (j[:, None, None] * k1[None, :, None] / n + k1[None, :, None] * i1[None, None, :] / n1)
    m1 = np.concatenate([np.cos(theta), -np.sin(theta)], axis=1) / math.sqrt(n)
    ang2 = 2.0 * np.pi * np.outer(j, j) / n2
    return m1.astype(np.float32), np.cos(ang2).astype(np.float32), np.sin(ang2).astype(np.float32)


def _rope_tables(rows):
    f32 = np.float32
    row = np.repeat(np.arange(rows), GRID_W).astype(f32)
    col = np.tile(np.arange(GRID_W), rows).astype(f32)
    half = QK_DIM // 2
    inv_freq = np.power(f32(ROPE_THETA), -np.arange(0, half, 2, dtype=f32) / f32(half))
    ang_r = row[:, None] * inv_freq[None, :]
    ang_c = col[:, None] * inv_freq[None, :]
    ang = np.concatenate([ang_r, ang_r, ang_c, ang_c], axis=-1)
    cos, sin = np.cos(ang), np.sin(ang)
    quarter = QK_DIM // 4
    first = (np.arange(QK_DIM) % half) < quarter
    sin_up = np.where(first, -sin, f32(0.0))
    sin_dn = np.where(first, f32(0.0), sin)
    rep = LANES // QK_DIM
    return tuple(jnp.asarray(np.tile(t, (1, rep)).astype(f32)) for t in (cos, sin_up, sin_dn))


def _fold_kernel(w_ref, wf_ref, wp_ref, ps_ref, cd_ref, sd_ref, o_ref, bd_ref):
    @pl.when(pl.program_id(1) == 0)
    def _():
        bd_ref[...] = jnp.zeros(bd_ref.shape, F32)
        cd, sd = cd_ref[...], sd_ref[...]
        for g in range(FOURIER_GROUPS):
            sl = slice(g * GROUP_W, (g + 1) * GROUP_W)
            bd_ref[0, sl, sl] = jnp.dot(cd, wf_ref[g], precision=HI, preferred_element_type=F32)
            bd_ref[1, sl, sl] = jnp.dot(sd, wf_ref[g], precision=HI, preferred_element_type=F32)
            bd_ref[2, sl, sl] = wp_ref[g] * ps_ref[:, sl]

    wa, wb = w_ref[:, :MIX_Q], w_ref[:, MIX_Q:2 * MIX_Q]
    for idx, (src, off) in enumerate(((wa, U_OFF), (wa, V_OFF), (wb, P_OFF))):
        o_ref[:, off:off + MIX_Q] = jnp.dot(src, bd_ref[idx], precision=HI,
                                            preferred_element_type=F32).astype(o_ref.dtype)
    o_ref[:, Q_OFF:] = w_ref[:, 2 * MIX_Q:].astype(o_ref.dtype)


def _fold_weights(w_in, w_fourier, w_pool, pool_scale):
    depth, d, in_w = w_in.shape
    tr = 256
    cd, sd = _channel_dft()
    return pl.pallas_call(
        _fold_kernel,
        grid=(depth, d // tr),
        in_specs=[
            pl.BlockSpec((None, tr, in_w), lambda l, i: (l, i, 0)),
            pl.BlockSpec((None, FOURIER_GROUPS, GROUP_W, GROUP_W), lambda l, i: (l, 0, 0, 0)),
            pl.BlockSpec((None, FOURIER_GROUPS, GROUP_W, GROUP_W), lambda l, i: (l, 0, 0, 0)),
            pl.BlockSpec((None, 1, MIX_Q), lambda l, i: (l, 0, 0)),
            pl.BlockSpec((GROUP_W, GROUP_W), lambda l, i: (0, 0)),
            pl.BlockSpec((GROUP_W, GROUP_W), lambda l, i: (0, 0)),
        ],
        out_specs=pl.BlockSpec((None, tr, W_EFF), lambda l, i: (l, i, 0)),
        out_shape=jax.ShapeDtypeStruct((depth, d, W_EFF), BF16),
        scratch_shapes=[pltpu.VMEM((3, MIX_Q, MIX_Q), F32)],
        compiler_params=_cparams("parallel", "arbitrary"),
        name="fold_weights",
    )(w_in, w_fourier, w_pool, pool_scale[:, None, :], jnp.asarray(cd, F32), jnp.asarray(sd, F32))


def _cast_kernel(w_ref, o_ref):
    o_ref[...] = w_ref[...].astype(o_ref.dtype)


def _cast_bf16(w):
    depth, r, c = w.shape
    return pl.pallas_call(
        _cast_kernel,
        grid=(depth,),
        in_specs=[pl.BlockSpec((None, r, c), lambda l: (l, 0, 0))],
        out_specs=pl.BlockSpec((None, r, c), lambda l: (l, 0, 0)),
        out_shape=jax.ShapeDtypeStruct(w.shape, BF16),
        compiler_params=_cparams("parallel"),
        name="cast_w_out",
    )(w)


def _mod_kernel(c_ref, w_ref, b_ref, o_ref):
    cc = c_ref[...]
    sc = cc * jax.nn.sigmoid(cc)
    o_ref[...] = jnp.dot(sc, w_ref[...], precision=HI, preferred_element_type=F32) + b_ref[...]


def _modulation(cc, w_mod, b_mod):
    depth, d, n = w_mod.shape
    tn = 1024
    rows = cc.shape[0]
    return pl.pallas_call(
        _mod_kernel,
        grid=(depth, n // tn),
        in_specs=[
            pl.BlockSpec((rows, d), lambda l, j: (0, 0)),
            pl.BlockSpec((None, d, tn), lambda l, j: (l, 0, j)),
            pl.BlockSpec((None, 1, tn), lambda l, j: (l, 0, j)),
        ],
        out_specs=pl.BlockSpec((None, rows, tn), lambda l, j: (l, 0, j)),
        out_shape=jax.ShapeDtypeStruct((depth, rows, n), F32),
        compiler_params=_cparams("parallel", "parallel"),
        name="modulation",
    )(cc, w_mod, b_mod[:, None, :])


def _qk_normalise(t, ones_blk, gain):
    ss = jnp.dot((t * t).astype(BF16), ones_blk, preferred_element_type=F32)
    return t * lax.rsqrt(ss * (1.0 / QK_DIM) + EPS) * gain


def _rotate(t, cos, sin_up, sin_dn):
    quarter = QK_DIM // 4
    up = pltpu.roll(t, LANES - quarter, 1)
    dn = pltpu.roll(t, quarter, 1)
    return t * cos + up * sin_up + dn * sin_dn


def _residual_update(x_ref, f_ref, p_ref, a_ref, sg_ref, mod_ref, w_ref, row, d, rows=slice(None)):
    gate = mod_ref[pl.ds(row, 1), 2 * d:]
    sg = sg_ref[rows, :].astype(F32)
    yf = (f_ref[rows, :] * sg[:, :MIX_Q]).astype(BF16)
    yp = (p_ref[rows, :] * sg[:, MIX_Q:2 * MIX_Q]).astype(BF16)
    ya = (a_ref[rows, :].astype(F32) * sg[:, 2 * MIX_Q:]).astype(BF16)
    acc = jnp.dot(yf, w_ref[0:MIX_Q, :], preferred_element_type=F32)
    acc += jnp.dot(yp, w_ref[MIX_Q:2 * MIX_Q, :], preferred_element_type=F32)
    acc += jnp.dot(ya, w_ref[2 * MIX_Q:, :], preferred_element_type=F32)
    return x_ref[rows, :] + gate * acc


def _inproj_kernel(x_ref, *rest, d, ctx_row, rope, fused):
    row = ctx_row if ctx_row is not None else pl.program_id(0)
    if fused:
        prev, rest = rest[:6], rest[6:]
    mod_ref, ng_ref, w_ref, qkg_ref, ones_ref = rest[:5]
    rest = rest[5:]
    if rope:
        cos_ref, sup_ref, sdn_ref = rest[:3]
        rest = rest[3:]
    mod = mod_ref[pl.ds(row, 1), :]
    shift, gain = mod[:, :d], ng_ref[...] * (1.0 + mod[:, d:2 * d])

    def modulated_norm(xf):
        ms = jnp.mean(xf * xf, axis=-1, keepdims=True)
        return (xf * lax.rsqrt(ms + EPS) * gain + shift).astype(BF16)

    if fused:
        xnew_ref, rest = rest[0], rest[1:]
    half = x_ref.shape[0] // 2
    h_parts = []
    for part in range(2):
        rows = slice(part * half, (part + 1) * half)
        if fused:
            xf = _residual_update(x_ref, *prev, row, d, rows=rows)
            xnew_ref[rows, :] = xf
        else:
            xf = x_ref[rows, :]
        h_parts.append(modulated_norm(xf))
    h = jnp.concatenate(h_parts, axis=0)
    uv_ref, bp_ref, qt_ref, k_ref, vt_ref, sg_ref = rest

    def proj(lo, hi, by_parts=False):
        if by_parts:
            return jnp.concatenate([jnp.dot(hp, w_ref[:, lo:hi], preferred_element_type=F32) for hp in h_parts],
                                   axis=0)
        return jnp.dot(h, w_ref[:, lo:hi], preferred_element_type=F32)

    pair = 2 * LANES
    n_pairs = ATTN_W // pair
    ones_blk = ones_ref[...]
    q_raw = [proj(Q_OFF + p * pair, Q_OFF + (p + 1) * pair, by_parts=True) for p in range(n_pairs)]
    k_raw = [proj(K_OFF + p * pair, K_OFF + (p + 1) * pair) for p in range(n_pairs)]
    q_nrm = [_qk_normalise(t, ones_blk, qkg_ref[0:1, :]) for t in q_raw]
    k_nrm = [_qk_normalise(t, ones_blk, qkg_ref[1:2, :]) for t in k_raw]
    v_raw = [proj(VV_OFF + p * pair, VV_OFF + (p + 1) * pair) for p in range(n_pairs)]
    g = proj(G_OFF, W_EFF)
    sg_ref[...] = (g * jax.nn.sigmoid(g)).astype(sg_ref.dtype)
    uv_ref[...] = proj(U_OFF, P_OFF)
    bp_ref[...] = proj(P_OFF, Q_OFF)

    if rope:
        cos, sup, sdn = cos_ref[...], sup_ref[...], sdn_ref[...]
    for p in range(n_pairs):
        qp, kp, vp = q_nrm[p], k_nrm[p], v_raw[p]
        for half in range(2):
            sl = slice(p * pair + half * LANES, p * pair + (half + 1) * LANES)
            hs = slice(half * LANES, (half + 1) * LANES)
            qt, kt = qp[:, hs], kp[:, hs]
            if rope:
                qt = _rotate(qt, cos, sup, sdn)
                kt = _rotate(kt, cos, sup, sdn)
            qt_ref[sl, :] = (qt * Q_SCALE).T.astype(qt_ref.dtype)
            k_ref[:, sl] = kt.astype(k_ref.dtype)
            vt_ref[sl, :] = vp[:, hs].T.astype(vt_ref.dtype)


def _in_projection(x, mod, norm_g, w_eff, qk_gain, ones_blk, rope_tabs, *, layer, ctx_row, tl, prev=None):
    b, l, d = x.shape
    rope = rope_tabs is not None
    fused = prev is not None
    row_blk = lambda width: pl.BlockSpec((None, tl, width), lambda bi, i: (bi, i, 0))
    full = lambda arr: pl.BlockSpec(arr.shape, lambda bi, i: (0,) * arr.ndim)

    def of_layer(arr, which=layer):
        return pl.BlockSpec((None,) + arr.shape[1:], lambda bi, i: (which,) + (0,) * (arr.ndim - 1))

    in_specs, args = [row_blk(d)], [x]
    if fused:
        four, pool, attn, sg_prev, w_out = prev
        in_specs += [row_blk(MIX_Q), row_blk(MIX_Q), row_blk(ATTN_W), row_blk(d), of_layer(mod, layer - 1),
                     of_layer(w_out, layer - 1)]
        args += [four, pool, attn, sg_prev, mod, w_out]
    in_specs += [of_layer(mod), full(norm_g), of_layer(w_eff), full(qk_gain), full(ones_blk)]
    args += [mod, norm_g, w_eff, qk_gain, ones_blk]
    if rope:
        in_specs += [pl.BlockSpec((tl, LANES), lambda bi, i: (i, 0))] * 3
        args += list(rope_tabs)
    out_shape = (
        jax.ShapeDtypeStruct((b, l, 2 * MIX_Q), F32),
        jax.ShapeDtypeStruct((b, l, MIX_Q), F32),
        jax.ShapeDtypeStruct((b, ATTN_W, l), BF16),
        jax.ShapeDtypeStruct((b, l, ATTN_W), BF16),
        jax.ShapeDtypeStruct((b, ATTN_W, l), BF16),
        jax.ShapeDtypeStruct((b, l, d), BF16),
    )
    col_blk = pl.BlockSpec((None, ATTN_W, tl), lambda bi, i: (bi, 0, i))
    out_specs = (row_blk(2 * MIX_Q), row_blk(MIX_Q), col_blk, row_blk(ATTN_W), col_blk, row_blk(d))
    if fused:
        out_shape = (jax.ShapeDtypeStruct((b, l, d), F32),) + out_shape
        out_specs = (row_blk(d),) + out_specs
    name = "in_projection_ctx" if ctx_row is not None else "in_projection"
    return pl.pallas_call(
        functools.partial(_inproj_kernel, d=d, ctx_row=ctx_row, rope=rope, fused=fused),
        grid=(b, l // tl),
        in_specs=in_specs,
        out_specs=out_specs,
        out_shape=out_shape,
        compiler_params=_cparams("parallel", "parallel"),
        name="residual_" + name if fused else name,
    )(*args)


def _attn_kernel(*refs, n_seg, seg_len, tk, lam_init):
    qt_ref = refs[0]
    kv_refs = refs[1:1 + 2 * n_seg]
    lam_ref, sub_ref, o_ref, m_ref, l_ref, acc_ref = refs[1 + 2 * n_seg:]
    qt = qt_ref[...]
    first_map = lax.broadcasted_iota(jnp.int32, qt.shape, 0) < QK_DIM
    zero = jnp.zeros_like(qt)
    ws = (jnp.where(first_map, qt, zero), jnp.where(first_map, zero, qt))

    m_ref[...] = jnp.full(m_ref.shape, -jnp.inf, F32)
    l_ref[...] = jnp.zeros(l_ref.shape, F32)
    acc_ref[...] = jnp.zeros(acc_ref.shape, F32)

    def chunk(k_ref, vt_ref, start, size):
        k = k_ref[pl.ds(start, size), :]
        vt = vt_ref[:, pl.ds(start, size)]
        for mp in range(2):
            s = jnp.dot(k, ws[mp], preferred_element_type=F32)
            m_old = m_ref[mp]
            m_new = jnp.maximum(m_old, jnp.max(s, axis=0, keepdims=True))
            alpha = jnp.exp2(m_old - m_new)
            e = jnp.exp2(s - m_new)
            l_ref[mp] = alpha * l_ref[mp] + jnp.sum(e, axis=0, keepdims=True)
            acc_ref[mp] = alpha * acc_ref[mp] + jnp.dot(vt, e.astype(BF16), preferred_element_type=F32)
            m_ref[mp] = m_new

    for si in range(n_seg):
        k_ref, vt_ref = kv_refs[2 * si], kv_refs[2 * si + 1]
        length = seg_len[si]
        size = min(tk, length)
        n_chunks = length // size
        if n_chunks == 1:
            chunk(k_ref, vt_ref, 0, size)
        else:
            def body(j, carry, k_ref=k_ref, vt_ref=vt_ref, size=size):
                chunk(k_ref, vt_ref, pl.multiple_of(j * size, size), size)
                return carry
            lax.fori_loop(0, n_chunks, body, 0)

    lv = lam_ref[...]
    lam = (jnp.exp(jnp.sum(lv[0:1] * lv[1:2], axis=-1, keepdims=True))
           - jnp.exp(jnp.sum(lv[2:3] * lv[3:4], axis=-1, keepdims=True)) + lam_init)
    o = acc_ref[0] / l_ref[0] - lam * (acc_ref[1] / l_ref[1])
    ms = jnp.mean(o * o, axis=0, keepdims=True)
    o = o * lax.rsqrt(ms + EPS) * (sub_ref[...] * (1.0 - lam_init))
    o_ref[...] = o.T.astype(o_ref.dtype)


def _attn_unshifted_kernel(*refs, n_seg, seg_len, tq, tk, lam_init):
    qt_ref = refs[0]
    kv_refs = refs[1:1 + 2 * n_seg]
    lam_ref, sub_ref, o_ref, l_ref, acc_ref, sa_ref, sb_ref = refs[1 + 2 * n_seg:]
    first_map = lax.broadcasted_iota(jnp.int32, (HEAD_W, tq), 0) < QK_DIM
    lv = lam_ref[...]
    lam = (jnp.exp(jnp.sum(lv[0:1] * lv[1:2], axis=-1, keepdims=True))
           - jnp.exp(jnp.sum(lv[2:3] * lv[3:4], axis=-1, keepdims=True)) + lam_init)
    out_gain = sub_ref[...] * (1.0 - lam_init)

    chunks = []
    for i in range(n_seg):
        size = min(tk, seg_len[i])
        assert seg_len[i] % size == 0
        chunks += [(kv_refs[2 * i], kv_refs[2 * i + 1], c * size, size) for c in range(seg_len[i] // size)]
    bufs = (sa_ref, sb_ref)

    def query_tile(qi, carry):
        q0 = pl.multiple_of(qi * tq, tq)
        qt = qt_ref[:, pl.ds(q0, tq)]
        zero = jnp.zeros_like(qt)
        w = jnp.concatenate([jnp.where(first_map, qt, zero), jnp.where(first_map, zero, qt)], axis=1)
        l_ref[...] = jnp.zeros(l_ref.shape, F32)
        acc_ref[...] = jnp.zeros(acc_ref.shape, F32)

        def scores(s_ref, k_ref, start, size):
            s_ref[0:size, :] = jnp.dot(k_ref[pl.ds(start, size), :], w, preferred_element_type=F32)

        def consume(s_ref, vt_ref, start, size):
            e = jnp.exp2(s_ref[0:size, :])
            l_ref[...] += jnp.sum(e.reshape(size // 8, 8, 2 * tq), axis=0)
            acc_ref[...] += jnp.dot(vt_ref[:, pl.ds(start, size)], e.astype(BF16), preferred_element_type=F32)

        scores(bufs[0], chunks[0][0], chunks[0][2], chunks[0][3])
        for i, (_, vt_ref, start, size) in enumerate(chunks):
            if i + 1 < len(chunks):
                nxt = chunks[i + 1]
                scores(bufs[(i + 1) % 2], nxt[0], nxt[2], nxt[3])
            consume(bufs[i % 2], vt_ref, start, size)

        inv = 1.0 / jnp.sum(l_ref[...], axis=0, keepdims=True)
        acc = acc_ref[...]
        o = acc[:, :tq] * inv[:, :tq] - (lam * inv[:, tq:]) * acc[:, tq:]
        ms = jnp.mean(o * o, axis=0, keepdims=True)
        o = o * lax.rsqrt(ms + EPS) * out_gain
        o_ref[pl.ds(q0, tq), :] = o.T.astype(o_ref.dtype)
        return carry

    lax.fori_loop(0, qt_ref.shape[1] // tq, query_tile, 0)


def _attention(qt, segs, lam_vecs, subln_col, lam_init, *, tq, tk, shifted):
    b, _, lq = qt.shape
    tq = min(tq, lq)
    seg_len = tuple(k.shape[1] for k, _ in segs)
    const = lambda arr: pl.BlockSpec(arr.shape, lambda bi, h, *i: (0, 0))
    if shifted:
        tq = min(tq, SHIFTED_TQ)
        grid = (b, N_HEADS, lq // tq)
        q_spec = pl.BlockSpec((None, HEAD_W, tq), lambda bi, h, i: (bi, h, i))
        o_spec = pl.BlockSpec((None, tq, HEAD_W), lambda bi, h, i: (bi, i, h))
        body = functools.partial(_attn_kernel, n_seg=len(segs), seg_len=seg_len, tk=tk, lam_init=lam_init)
        scratch = [pltpu.VMEM((2, 1, tq), F32), pltpu.VMEM((2, 1, tq), F32), pltpu.VMEM((2, HEAD_W, tq), F32)]
        semantics = ("parallel", "parallel", "arbitrary")
    else:
        grid = (b, N_HEADS)
        q_spec = pl.BlockSpec((None, HEAD_W, lq), lambda bi, h: (bi, h, 0))
        o_spec = pl.BlockSpec((None, lq, HEAD_W), lambda bi, h: (bi, 0, h))
        body = functools.partial(_attn_unshifted_kernel, n_seg=len(segs), seg_len=seg_len, tq=tq, tk=tk,
                                 lam_init=lam_init)
        s_rows = min(tk, max(seg_len))
        scratch = [pltpu.VMEM((8, 2 * tq), F32), pltpu.VMEM((HEAD_W, 2 * tq), F32),
                   pltpu.VMEM((s_rows, 2 * tq), F32), pltpu.VMEM((s_rows, 2 * tq), F32)]
        semantics = ("parallel", "parallel")
    in_specs, args = [q_spec], [qt]
    for (k, vt), lk in zip(segs, seg_len):
        in_specs.append(pl.BlockSpec((None, lk, HEAD_W), lambda bi, h, *i: (bi, 0, h)))
        in_specs.append(pl.BlockSpec((None, HEAD_W, lk), lambda bi, h, *i: (bi, h, 0)))
        args += [k, vt]
    in_specs += [const(lam_vecs), const(subln_col)]
    args += [lam_vecs, subln_col]
    return pl.pallas_call(
        body,
        grid=grid,
        in_specs=in_specs,
        out_specs=o_spec,
        out_shape=jax.ShapeDtypeStruct((b, lq, ATTN_W), BF16),
        scratch_shapes=scratch,
        compiler_params=_cparams(*semantics),
        name="diff_attention_shifted" if shifted else "diff_attention",
    )(*args)


def _attention_dispatch(score_bound, *args, **kw):
    return lax.cond(score_bound <= MAX_UNSHIFTED_SCORE,
                    lambda: _attention(*args, shifted=False, **kw),
                    lambda: _attention(*args, shifted=True, **kw))


def _fourier_kernel(u_ref, v_ref, m1_ref, c2_ref, s2_ref, o_ref, yr_ref, yi_ref, *, n1, n2):
    def stage1(j, carry):
        xu = u_ref[pl.ds(j, n1, stride=n2), :]
        xv = v_ref[pl.ds(j, n1, stride=n2), :]
        x = jnp.concatenate([xu, xv], axis=1).astype(BF16)
        p = jnp.dot(m1_ref[j], x, preferred_element_type=F32)
        rows = pl.ds(pl.multiple_of(j * n1, n1), n1)
        yr_ref[rows, :] = p[:n1, :LANES] + p[n1:, LANES:]
        yi_ref[rows, :] = p[n1:, :LANES] - p[:n1, LANES:]
        return carry

    lax.fori_loop(0, n2, stage1, 0, unroll=FOURIER_UNROLL)
    c2, s2 = c2_ref[...], s2_ref[...]

    def stage2(k1, carry):
        rows = pl.ds(k1, n2, stride=n1)
        xr = (jnp.dot(c2, yr_ref[rows, :].astype(BF16), preferred_element_type=F32)
              + jnp.dot(s2, yi_ref[rows, :].astype(BF16), preferred_element_type=F32))
        o_ref[rows, :] = xr
        return carry

    lax.fori_loop(0, n1, stage2, 0, unroll=FOURIER_UNROLL)


def _fourier_latent(uv, n1, n2):
    b, l, _ = uv.shape
    m1, c2, s2 = _stage_tables(n1, n2)
    m1, c2, s2 = (jnp.asarray(t).astype(BF16) for t in (m1, c2, s2))
    halves = MIX_Q // LANES
    return pl.pallas_call(
        functools.partial(_fourier_kernel, n1=n1, n2=n2),
        grid=(b, halves),
        in_specs=[
            pl.BlockSpec((None, l, LANES), lambda bi, c: (bi, 0, c)),
            pl.BlockSpec((None, l, LANES), lambda bi, c: (bi, 0, c + halves)),
            pl.BlockSpec(m1.shape, lambda bi, c: (0, 0, 0)),
            pl.BlockSpec(c2.shape, lambda bi, c: (0, 0)),
            pl.BlockSpec(s2.shape, lambda bi, c: (0, 0)),
        ],
        out_specs=pl.BlockSpec((None, l, LANES), lambda bi, c: (bi, 0, c)),
        out_shape=jax.ShapeDtypeStruct((b, l, MIX_Q), F32),
        scratch_shapes=[pltpu.VMEM((l, LANES), F32), pltpu.VMEM((l, LANES), F32)],
        compiler_params=_cparams("parallel", "parallel"),
        name="fourier_positions",
    )(uv, uv, m1, c2, s2)


def _fourier_dense_kernel(uv_ref, c_ref, s_ref, o_ref):
    uv = uv_ref[...].astype(BF16)
    o_ref[...] = (jnp.dot(c_ref[...], uv[:, :MIX_Q], preferred_element_type=F32)
                  - jnp.dot(s_ref[...], uv[:, MIX_Q:], preferred_element_type=F32))


def _fourier_dense(uv):
    b, l, _ = uv.shape
    n = np.arange(l)
    ang = 2.0 * np.pi * np.outer(n, n) / l
    c = jnp.asarray((np.cos(ang) / math.sqrt(l)).astype(np.float32)).astype(BF16)
    s = jnp.asarray((np.sin(ang) / math.sqrt(l)).astype(np.float32)).astype(BF16)
    return pl.pallas_call(
        _fourier_dense_kernel,
        grid=(b,),
        in_specs=[pl.BlockSpec((None, l, 2 * MIX_Q), lambda bi: (bi, 0, 0)),
                  pl.BlockSpec((l, l), lambda bi: (0, 0)),
                  pl.BlockSpec((l, l), lambda bi: (0, 0))],
        out_specs=pl.BlockSpec((None, l, MIX_Q), lambda bi: (bi, 0, 0)),
        out_shape=jax.ShapeDtypeStruct((b, l, MIX_Q), F32),
        compiler_params=_cparams("parallel"),
        name="fourier_positions_ctx",
    )(uv, c, s)


POOL_HALO = 8


def _pool_kernel(b_ref, o_ref, pad_ref, *, l, rows):
    zeros = jnp.zeros((POOL_HALO, LANES), F32)
    lane = lax.broadcasted_iota(jnp.int32, (rows, LANES), 1)
    low_group = lane < GROUP_W
    n_chunks = l // rows
    pad_ref[0:POOL_HALO, :] = zeros
    pad_ref[POOL_HALO + l:, :] = zeros
    pad_ref[POOL_HALO:POOL_HALO + l, :] = b_ref[...]

    def lane_tile(tile):
        w_lo, w_hi = POOL_WINDOWS[2 * tile], POOL_WINDOWS[2 * tile + 1]

        def chunk(t0, edge):
            shifted = lambda dlt: pad_ref[pl.ds(t0 + POOL_HALO + dlt, rows), :]
            lo_range = range(-(w_lo // 2), w_lo // 2)
            inner = functools.reduce(lambda a, b: a + b, [shifted(d) for d in lo_range])
            outer = functools.reduce(lambda a, b: a + b,
                                     [shifted(d) for d in range(-(w_hi // 2), w_hi // 2) if d not in lo_range], inner)
            sums = jnp.where(low_group, inner, outer)
            if edge:
                pos = t0 + lax.broadcasted_iota(jnp.int32, (rows, LANES), 0)
                half = jnp.where(low_group, w_lo // 2, w_hi // 2)
                cnt = jnp.minimum(pos + (half - 1), l - 1) - jnp.maximum(pos - half, 0) + 1
                mean = sums / cnt.astype(F32)
            else:
                mean = sums * jnp.where(low_group, 1.0 / w_lo, 1.0 / w_hi)
            o_ref[pl.ds(t0, rows), :] = mean - shifted(0)

        assert rows >= POOL_HALO
        chunk(0, True)
        if n_chunks > 1:
            def body(ci, carry):
                chunk(pl.multiple_of(ci * rows, rows), False)
                return carry

            lax.fori_loop(1, n_chunks - 1, body, 0)
            chunk((n_chunks - 1) * rows, True)

    for tile in range(MIX_Q // LANES):
        pl.when(pl.program_id(1) == tile)(functools.partial(lane_tile, tile))


def _pool(bp):
    b, l, _ = bp.shape
    rows = 256
    tile_blk = pl.BlockSpec((None, l, LANES), lambda bi, t: (bi, 0, t))
    return pl.pallas_call(
        functools.partial(_pool_kernel, l=l, rows=rows),
        grid=(b, MIX_Q // LANES),
        in_specs=[tile_blk],
        out_specs=tile_blk,
        out_shape=jax.ShapeDtypeStruct((b, l, MIX_Q), F32),
        scratch_shapes=[pltpu.VMEM((l + 2 * POOL_HALO, LANES), F32)],
        compiler_params=_cparams("parallel", "parallel"),
        name="pool_windows",
    )(bp)


def _outproj_kernel(x_ref, f_ref, p_ref, a_ref, sg_ref, mod_ref, w_ref, o_ref, *, d, ctx_row):
    row = ctx_row if ctx_row is not None else pl.program_id(0)
    o_ref[...] = _residual_update(x_ref, f_ref, p_ref, a_ref, sg_ref, mod_ref, w_ref, row, d)


def _out_projection(x, four, pool, attn, sg, mod, w_out, *, layer, ctx_row, tl):
    b, l, d = x.shape
    row_blk = lambda width: pl.BlockSpec((None, tl, width), lambda bi, i: (bi, i, 0))
    of_layer = lambda arr: pl.BlockSpec((None,) + arr.shape[1:], lambda bi, i: (layer,) + (0,) * (arr.ndim - 1))
    return pl.pallas_call(
        functools.partial(_outproj_kernel, d=d, ctx_row=ctx_row),
        grid=(b, l // tl),
        in_specs=[row_blk(d), row_blk(MIX_Q), row_blk(MIX_Q), row_blk(ATTN_W), row_blk(d), of_layer(mod),
                  of_layer(w_out)],
        out_specs=row_blk(d),
        out_shape=jax.ShapeDtypeStruct((b, l, d), F32),
        compiler_params=_cparams("parallel", "parallel"),
        name="out_projection_ctx" if ctx_row is not None else "out_projection",
    )(x, four, pool, attn, sg, mod, w_out)


def kernel(x, c, ctx, c_ctx, norm_g, w_mod, b_mod, w_in, w_fourier, w_pool, pool_scale, qk_norm_g, lam_vecs,
           subln_g, w_out):
    b, l, d = x.shape
    depth = w_in.shape[0]
    ctx_len = ctx.shape[1]
    ctx_row = b

    cc = jnp.concatenate([c, c_ctx[None, :], jnp.zeros((8 - b - 1, d), F32)], axis=0)
    mod = _modulation(cc, w_mod, b_mod)
    w_eff = _fold_weights(w_in, w_fourier, w_pool, pool_scale)
    w_out_bf = _cast_bf16(w_out)
    rope_tabs = _rope_tables(l // GRID_W)
    lane_chunk = lax.broadcasted_iota(jnp.int32, (2 * LANES, 2 * LANES), 0) // QK_DIM
    ones_blk = (lane_chunk == lane_chunk.T).astype(BF16)
    n1 = 64
    n2 = l // n1

    pending = None
    for layer in range(depth):
        lam_init = 0.8 - 0.6 * math.exp(-0.3 * layer)
        update_ctx = layer < depth - 1
        ng = norm_g[layer][None, :]
        qk_gain = jnp.tile(qk_norm_g[layer], (1, 2 * LANES // QK_DIM))
        sub_g = subln_g[layer][:, None]
        lv = lam_vecs[layer]

        outs = _in_projection(x, mod, ng, w_eff, qk_gain, ones_blk, rope_tabs, layer=layer, ctx_row=None,
                              tl=512, prev=pending)
        if pending is not None:
            x, outs = outs[0], outs[1:]
        uv, bp, qt, k, vt, sg = outs
        uv_c, bp_c, qt_c, k_c, vt_c, sg_c = _in_projection(ctx, mod, ng, w_eff, qk_gain, ones_blk, None,
                                                           layer=layer, ctx_row=ctx_row, tl=ctx_len)
        gains = jnp.max(jnp.abs(qk_norm_g[layer]), axis=-1)
        score_bound = QK_DIM * gains[0] * gains[1] * (Q_SCALE * BF16_ROUNDING_SLACK)
        attn = _attention_dispatch(score_bound, qt, [(k_c, vt_c), (k, vt)], lv, sub_g, lam_init, tq=ATTN_TQ,
                                  tk=ATTN_TK)
        four = _fourier_latent(uv, n1, n2)
        pool = _pool(bp)
        if update_ctx:
            attn_c = _attention_dispatch(score_bound, qt_c, [(k_c, vt_c)], lv, sub_g, lam_init, tq=ctx_len, tk=512)
            ctx = _out_projection(ctx, _fourier_dense(uv_c), _pool(bp_c), attn_c, sg_c, mod, w_out_bf,
                                  layer=layer, ctx_row=ctx_row, tl=ctx_len)
        if layer + 1 < depth:
            pending = (four, pool, attn, sg, w_out_bf)
        else:
            x = _out_projection(x, four, pool, attn, sg, mod, w_out_bf, layer=layer, ctx_row=None, tl=1024)
    return x
```

```python
import functools
import math

import numpy as np
import jax
import jax.numpy as jnp
from jax import lax
from jax.experimental import pallas as pl
from jax.experimental.pallas import tpu as pltpu

GRID_W = 64
FOURIER_GROUPS = 4
GROUP_W = 64
MIX_Q = 256
POOL_WINDOWS = (2, 4, 8, 16)
QK_DIM = 64
HEAD_W = 2 * QK_DIM
N_HEADS = 4
ATTN_W = N_HEADS * HEAD_W
ROPE_THETA = 10000.0
EPS = 1e-6
Q_SCALE = QK_DIM ** -0.5 * math.log2(math.e)
FOURIER_UNROLL = 16
ATTN_TK = 512
ATTN_TQ = 2048
SHIFTED_TQ = 512
MAX_UNSHIFTED_SCORE = 40.0
BF16_ROUNDING_SLACK = 1.02
U_OFF, V_OFF, P_OFF, Q_OFF, K_OFF, VV_OFF, G_OFF, W_EFF = 0, 256, 512, 768, 1280, 1792, 2304, 3328

LANES = 128
VMEM_LIMIT = 56 * 1024 * 1024
HI = lax.Precision.HIGHEST
F32 = jnp.float32
BF16 = jnp.bfloat16


def _cparams(*sem):
    return pltpu.CompilerParams(dimension_semantics=sem, vmem_limit_bytes=VMEM_LIMIT)


def _channel_dft():
    c = np.arange(GROUP_W)
    ang = 2.0 * np.pi * np.outer(c, c) / GROUP_W
    s = 1.0 / math.sqrt(GROUP_W)
    return np.cos(ang) * s, np.sin(ang) * s


def _stage_tables(n1, n2):
    n = n1 * n2
    k1 = np.arange(n1)
    i1 = np.arange(n1)
    j = np.arange(n2)
    theta = 2.0 * np.pi * (j[:, None, None] * k1[None, :, None] / n + k1[None, :, None] * i1[None, None, :] / n1)
    m1 = np.concatenate([np.cos(theta), -np.sin(theta)], axis=1) / math.sqrt(n)
    ang2 = 2.0 * np.pi * np.outer(j, j) / n2
    return m1.astype(np.float32), np.cos(ang2).astype(np.float32), np.sin(ang2).astype(np.float32)


def _rope_tables(rows):
    f32 = np.float32
    row = np.repeat(np.arange(rows), GRID_W).astype(f32)
    col = np.tile(np.arange(GRID_W), rows).astype(f32)
    half = QK_DIM // 2
    inv_freq = np.power(f32(ROPE_THETA), -np.arange(0, half, 2, dtype=f32) / f32(half))
    ang_r = row[:, None] * inv_freq[None, :]
    ang_c = col[:, None] * inv_freq[None, :]
    ang = np.concatenate([ang_r, ang_r, ang_c, ang_c], axis=-1)
    cos, sin = np.cos(ang), np.sin(ang)
    quarter = QK_DIM // 4
    first = (np.arange(QK_DIM) % half) < quarter
    sin_up = np.where(first, -sin, f32(0.0))
    sin_dn = np.where(first, f32(0.0), sin)
    rep = LANES // QK_DIM
    return tuple(jnp.asarray(np.tile(t, (1, rep)).astype(f32)) for t in (cos, sin_up, sin_dn))


def _fold_kernel(w_ref, wf_ref, wp_ref, ps_ref, cd_ref, sd_ref, o_ref, bd_ref):
    @pl.when(pl.program_id(1) == 0)
    def _():
        bd_ref[...] = jnp.zeros(bd_ref.shape, F32)
        cd, sd = cd_ref[...], sd_ref[...]
        for g in range(FOURIER_GROUPS):
            sl = slice(g * GROUP_W, (g + 1) * GROUP_W)
            bd_ref[0, sl, sl] = jnp.dot(cd, wf_ref[g], precision=HI, preferred_element_type=F32)
            bd_ref[1, sl, sl] = jnp.dot(sd, wf_ref[g], precision=HI, preferred_element_type=F32)
            bd_ref[2, sl, sl] = wp_ref[g] * ps_ref[:, sl]

    wa, wb = w_ref[:, :MIX_Q], w_ref[:, MIX_Q:2 * MIX_Q]
    for idx, (src, off) in enumerate(((wa, U_OFF), (wa, V_OFF), (wb, P_OFF))):
        o_ref[:, off:off + MIX_Q] = jnp.dot(src, bd_ref[idx], precision=HI,
                                            preferred_element_type=F32).astype(o_ref.dtype)
    o_ref[:, Q_OFF:] = w_ref[:, 2 * MIX_Q:].astype(o_ref.dtype)


def _fold_weights(w_in, w_fourier, w_pool, pool_scale):
    depth, d, in_w = w_in.shape
    tr = 256
    cd, sd = _channel_dft()
    return pl.pallas_call(
        _fold_kernel,
        grid=(depth, d // tr),
        in_specs=[
            pl.BlockSpec((None, tr, in_w), lambda l, i: (l, i, 0)),
            pl.BlockSpec((None, FOURIER_GROUPS, GROUP_W, GROUP_W), lambda l, i: (l, 0, 0, 0)),
            pl.BlockSpec((None, FOURIER_GROUPS, GROUP_W, GROUP_W), lambda l, i: (l, 0, 0, 0)),
            pl.BlockSpec((None, 1, MIX_Q), lambda l, i: (l, 0, 0)),
            pl.BlockSpec((GROUP_W, GROUP_W), lambda l, i: (0, 0)),
            pl.BlockSpec((GROUP_W, GROUP_W), lambda l, i: (0, 0)),
        ],
        out_specs=pl.BlockSpec((None, tr, W_EFF), lambda l, i: (l, i, 0)),
        out_shape=jax.ShapeDtypeStruct((depth, d, W_EFF), BF16),
        scratch_shapes=[pltpu.VMEM((3, MIX_Q, MIX_Q), F32)],
        compiler_params=_cparams("parallel", "arbitrary"),
        name="fold_weights",
    )(w_in, w_fourier, w_pool, pool_scale[:, None, :], jnp.asarray(cd, F32), jnp.asarray(sd, F32))


def _cast_kernel(w_ref, o_ref):
    o_ref[...] = w_ref[...].astype(o_ref.dtype)


def _cast_bf16(w):
    depth, r, c = w.shape
    return pl.pallas_call(
        _cast_kernel,
        grid=(depth,),
        in_specs=[pl.BlockSpec((None, r, c), lambda l: (l, 0, 0))],
        out_specs=pl.BlockSpec((None, r, c), lambda l: (l, 0, 0)),
        out_shape=jax.ShapeDtypeStruct(w.shape, BF16),
        compiler_params=_cparams("parallel"),
        name="cast_w_out",
    )(w)


def _mod_kernel(c_ref, w_ref, b_ref, o_ref):
    cc = c_ref[...]
    sc = cc * jax.nn.sigmoid(cc)
    o_ref[...] = jnp.dot(sc, w_ref[...], precision=HI, preferred_element_type=F32) + b_ref[...]


def _modulation(cc, w_mod, b_mod):
    depth, d, n = w_mod.shape
    tn = 1024
    rows = cc.shape[0]
    return pl.pallas_call(
        _mod_kernel,
        grid=(depth, n // tn),
        in_specs=[
            pl.BlockSpec((rows, d), lambda l, j: (0, 0)),
            pl.BlockSpec((None, d, tn), lambda l, j: (l, 0, j)),
            pl.BlockSpec((None, 1, tn), lambda l, j: (l, 0, j)),
        ],
        out_specs=pl.BlockSpec((None, rows, tn), lambda l, j: (l, 0, j)),
        out_shape=jax.ShapeDtypeStruct((depth, rows, n), F32),
        compiler_params=_cparams("parallel", "parallel"),
        name="modulation",
    )(cc, w_mod, b_mod[:, None, :])


def _qk_normalise(t, ones_blk, gain):
    ss = jnp.dot((t * t).astype(BF16), ones_blk, preferred_element_type=F32)
    return t * lax.rsqrt(ss * (1.0 / QK_DIM) + EPS) * gain


def _rotate(t, cos, sin_up, sin_dn):
    quarter = QK_DIM // 4
    up = pltpu.roll(t, LANES - quarter, 1)
    dn = pltpu.roll(t, quarter, 1)
    return t * cos + up * sin_up + dn * sin_dn


def _residual_update(x_ref, f_ref, p_ref, a_ref, sg_ref, mod_ref, w_ref, row, d, rows=slice(None)):
    gate = mod_ref[pl.ds(row, 1), 2 * d:]
    sg = sg_ref[rows, :].astype(F32)
    yf = (f_ref[rows, :] * sg[:, :MIX_Q]).astype(BF16)
    yp = (p_ref[rows, :] * sg[:, MIX_Q:2 * MIX_Q]).astype(BF16)
    ya = (a_ref[rows, :].astype(F32) * sg[:, 2 * MIX_Q:]).astype(BF16)
    acc = jnp.dot(yf, w_ref[0:MIX_Q, :], preferred_element_type=F32)
    acc += jnp.dot(yp, w_ref[MIX_Q:2 * MIX_Q, :], preferred_element_type=F32)
    acc += jnp.dot(ya, w_ref[2 * MIX_Q:, :], preferred_element_type=F32)
    return x_ref[rows, :] + gate * acc


def _inproj_kernel(x_ref, *rest, d, ctx_row, rope, fused):
    row = ctx_row if ctx_row is not None else pl.program_id(0)
    if fused:
        prev, rest = rest[:6], rest[6:]
    mod_ref, ng_ref, w_ref, qkg_ref, ones_ref = rest[:5]
    rest = rest[5:]
    if rope:
        cos_ref, sup_ref, sdn_ref = rest[:3]
        rest = rest[3:]
    mod = mod_ref[pl.ds(row, 1), :]
    shift, gain = mod[:, :d], ng_ref[...] * (1.0 + mod[:, d:2 * d])

    def modulated_norm(xf):
        ms = jnp.mean(xf * xf, axis=-1, keepdims=True)
        return (xf * lax.rsqrt(ms + EPS) * gain + shift).astype(BF16)

    if fused:
        xnew_ref, rest = rest[0], rest[1:]
    half = x_ref.shape[0] // 2
    h_parts = []
    for part in range(2):
        rows = slice(part * half, (part + 1) * half)
        if fused:
            xf = _residual_update(x_ref, *prev, row, d, rows=rows)
            xnew_ref[rows, :] = xf
        else:
            xf = x_ref[rows, :]
        h_parts.append(modulated_norm(xf))
    h = jnp.concatenate(h_parts, axis=0)
    uv_ref, bp_ref, qt_ref, k_ref, vt_ref, sg_ref = rest

    def proj(lo, hi, by_parts=False):
        if by_parts:
            return jnp.concatenate([jnp.dot(hp, w_ref[:, lo:hi], preferred_element_type=F32) for hp in h_parts],
                                   axis=0)
        return jnp.dot(h, w_ref[:, lo:hi], preferred_element_type=F32)

    pair = 2 * LANES
    n_pairs = ATTN_W // pair
    ones_blk = ones_ref[...]
    q_raw = [proj(Q_OFF + p * pair, Q_OFF + (p + 1) * pair, by_parts=True) for p in range(n_pairs)]
    k_raw = [proj(K_OFF + p * pair, K_OFF + (p + 1) * pair) for p in range(n_pairs)]
    q_nrm = [_qk_normalise(t, ones_blk, qkg_ref[0:1, :]) for t in q_raw]
    k_nrm = [_qk_normalise(t, ones_blk, qkg_ref[1:2, :]) for t in k_raw]
    v_raw = [proj(VV_OFF + p * pair, VV_OFF + (p + 1) * pair) for p in range(n_pairs)]
    g = proj(G_OFF, W_EFF)
    sg_ref[...] = (g * jax.nn.sigmoid(g)).astype(sg_ref.dtype)
    uv_ref[...] = proj(U_OFF, P_OFF)
    bp_ref[...] = proj(P_OFF, Q_OFF)

    if rope:
        cos, sup, sdn = cos_ref[...], sup_ref[...], sdn_ref[...]
    for p in range(n_pairs):
        qp, kp, vp = q_nrm[p], k_nrm[p], v_raw[p]
        for half in range(2):
            sl = slice(p * pair + half * LANES, p * pair + (half + 1) * LANES)
            hs = slice(half * LANES, (half + 1) * LANES)
            qt, kt = qp[:, hs], kp[:, hs]
            if rope:
                qt = _rotate(qt, cos, sup, sdn)
                kt = _rotate(kt, cos, sup, sdn)
            qt_ref[sl, :] = (qt * Q_SCALE).T.astype(qt_ref.dtype)
            k_ref[:, sl] = kt.astype(k_ref.dtype)
            vt_ref[sl, :] = vp[:, hs].T.astype(vt_ref.dtype)


def _in_projection(x, mod, norm_g, w_eff, qk_gain, ones_blk, rope_tabs, *, layer, ctx_row, tl, prev=None):
    b, l, d = x.shape
    rope = rope_tabs is not None
    fused = prev is not None
    row_blk = lambda width: pl.BlockSpec((None, tl, width), lambda bi, i: (bi, i, 0))
    full = lambda arr: pl.BlockSpec(arr.shape, lambda bi, i: (0,) * arr.ndim)

    def of_layer(arr, which=layer):
        return pl.BlockSpec((None,) + arr.shape[1:], lambda bi, i: (which,) + (0,) * (arr.ndim - 1))

    in_specs, args = [row_blk(d)], [x]
    if fused:
        four, pool, attn, sg_prev, w_out = prev
        in_specs += [row_blk(MIX_Q), row_blk(MIX_Q), row_blk(ATTN_W), row_blk(d), of_layer(mod, layer - 1),
                     of_layer(w_out, layer - 1)]
        args += [four, pool, attn, sg_prev, mod, w_out]
    in_specs += [of_layer(mod), full(norm_g), of_layer(w_eff), full(qk_gain), full(ones_blk)]
    args += [mod, norm_g, w_eff, qk_gain, ones_blk]
    if rope:
        in_specs += [pl.BlockSpec((tl, LANES), lambda bi, i: (i, 0))] * 3
        args += list(rope_tabs)
    out_shape = (
        jax.ShapeDtypeStruct((b, l, 2 * MIX_Q), F32),
        jax.ShapeDtypeStruct((b, l, MIX_Q), F32),
        jax.ShapeDtypeStruct((b, ATTN_W, l), BF16),
        jax.ShapeDtypeStruct((b, l, ATTN_W), BF16),
        jax.ShapeDtypeStruct((b, ATTN_W, l), BF16),
        jax.ShapeDtypeStruct((b, l, d), BF16),
    )
    col_blk = pl.BlockSpec((None, ATTN_W, tl), lambda bi, i: (bi, 0, i))
    out_specs = (row_blk(2 * MIX_Q), row_blk(MIX_Q), col_blk, row_blk(ATTN_W), col_blk, row_blk(d))
    if fused:
        out_shape = (jax.ShapeDtypeStruct((b, l, d), F32),) + out_shape
        out_specs = (row_blk(d),) + out_specs
    name = "in_projection_ctx" if ctx_row is not None else "in_projection"
    return pl.pallas_call(
        functools.partial(_inproj_kernel, d=d, ctx_row=ctx_row, rope=rope, fused=fused),
        grid=(b, l // tl),
        in_specs=in_specs,
        out_specs=out_specs,
        out_shape=out_shape,
        compiler_params=_cparams("parallel", "parallel"),
        name="residual_" + name if fused else name,
    )(*args)


def _attn_kernel(*refs, n_seg, seg_len, tk, lam_init):
    qt_ref = refs[0]
    kv_refs = refs[1:1 + 2 * n_seg]
    lam_ref, sub_ref, o_ref, m_ref, l_ref, acc_ref = refs[1 + 2 * n_seg:]
    qt = qt_ref[...]
    first_map = lax.broadcasted_iota(jnp.int32, qt.shape, 0) < QK_DIM
    zero = jnp.zeros_like(qt)
    ws = (jnp.where(first_map, qt, zero), jnp.where(first_map, zero, qt))

    m_ref[...] = jnp.full(m_ref.shape, -jnp.inf, F32)
    l_ref[...] = jnp.zeros(l_ref.shape, F32)
    acc_ref[...] = jnp.zeros(acc_ref.shape, F32)

    def chunk(k_ref, vt_ref, start, size):
        k = k_ref[pl.ds(start, size), :]
        vt = vt_ref[:, pl.ds(start, size)]
        for mp in range(2):
            s = jnp.dot(k, ws[mp], preferred_element_type=F32)
            m_old = m_ref[mp]
            m_new = jnp.maximum(m_old, jnp.max(s, axis=0, keepdims=True))
            alpha = jnp.exp2(m_old - m_new)
            e = jnp.exp2(s - m_new)
            l_ref[mp] = alpha * l_ref[mp] + jnp.sum(e, axis=0, keepdims=True)
            acc_ref[mp] = alpha * acc_ref[mp] + jnp.dot(vt, e.astype(BF16), preferred_element_type=F32)
            m_ref[mp] = m_new

    for si in range(n_seg):
        k_ref, vt_ref = kv_refs[2 * si], kv_refs[2 * si + 1]
        length = seg_len[si]
        size = min(tk, length)
        n_chunks = length // size
        if n_chunks == 1:
            chunk(k_ref, vt_ref, 0, size)
        else:
            def body(j, carry, k_ref=k_ref, vt_ref=vt_ref, size=size):
                chunk(k_ref, vt_ref, pl.multiple_of(j * size, size), size)
                return carry
            lax.fori_loop(0, n_chunks, body, 0)

    lv = lam_ref[...]
    lam = (jnp.exp(jnp.sum(lv[0:1] * lv[1:2], axis=-1, keepdims=True))
           - jnp.exp(jnp.sum(lv[2:3] * lv[3:4], axis=-1, keepdims=True)) + lam_init)
    o = acc_ref[0] / l_ref[0] - lam * (acc_ref[1] / l_ref[1])
    ms = jnp.mean(o * o, axis=0, keepdims=True)
    o = o * lax.rsqrt(ms + EPS) * (sub_ref[...] * (1.0 - lam_init))
    o_ref[...] = o.T.astype(o_ref.dtype)


def _attn_unshifted_kernel(*refs, n_seg, seg_len, tq, tk, lam_init):
    qt_ref = refs[0]
    kv_refs = refs[1:1 + 2 * n_seg]
    lam_ref, sub_ref, o_ref, l_ref, acc_ref, sa_ref, sb_ref = refs[1 + 2 * n_seg:]
    first_map = lax.broadcasted_iota(jnp.int32, (HEAD_W, tq), 0) < QK_DIM
    lv = lam_ref[...]
    lam = (jnp.exp(jnp.sum(lv[0:1] * lv[1:2], axis=-1, keepdims=True))
           - jnp.exp(jnp.sum(lv[2:3] * lv[3:4], axis=-1, keepdims=True)) + lam_init)
    out_gain = sub_ref[...] * (1.0 - lam_init)

    chunks = []
    for i in range(n_seg):
        size = min(tk, seg_len[i])
        assert seg_len[i] % size == 0
        chunks += [(kv_refs[2 * i], kv_refs[2 * i + 1], c * size, size) for c in range(seg_len[i] // size)]
    bufs = (sa_ref, sb_ref)

    def query_tile(qi, carry):
        q0 = pl.multiple_of(qi * tq, tq)
        qt = qt_ref[:, pl.ds(q0, tq)]
        zero = jnp.zeros_like(qt)
        w = jnp.concatenate([jnp.where(first_map, qt, zero), jnp.where(first_map, zero, qt)], axis=1)
        l_ref[...] = jnp.zeros(l_ref.shape, F32)
        acc_ref[...] = jnp.zeros(acc_ref.shape, F32)

        def scores(s_ref, k_ref, start, size):
            s_ref[0:size, :] = jnp.dot(k_ref[pl.ds(start, size), :], w, preferred_element_type=F32)

        def consume(s_ref, vt_ref, start, size):
            e = jnp.exp2(s_ref[0:size, :])
            l_ref[...] += jnp.sum(e.reshape(size // 8, 8, 2 * tq), axis=0)
            acc_ref[...] += jnp.dot(vt_ref[:, pl.ds(start, size)], e.astype(BF16), preferred_element_type=F32)

        scores(bufs[0], chunks[0][0], chunks[0][2], chunks[0][3])
        for i, (_, vt_ref, start, size) in enumerate(chunks):
            if i + 1 < len(chunks):
                nxt = chunks[i + 1]
                scores(bufs[(i + 1) % 2], nxt[0], nxt[2], nxt[3])
            consume(bufs[i % 2], vt_ref, start, size)

        inv = 1.0 / jnp.sum(l_ref[...], axis=0, keepdims=True)
        acc = acc_ref[...]
        o = acc[:, :tq] * inv[:, :tq] - (lam * inv[:, tq:]) * acc[:, tq:]
        ms = jnp.mean(o * o, axis=0, keepdims=True)
        o = o * lax.rsqrt(ms + EPS) * out_gain
        o_ref[pl.ds(q0, tq), :] = o.T.astype(o_ref.dtype)
        return carry

    lax.fori_loop(0, qt_ref.shape[1] // tq, query_tile, 0)


def _attention(qt, segs, lam_vecs, subln_col, lam_init, *, tq, tk, shifted):
    b, _, lq = qt.shape
    tq = min(tq, lq)
    seg_len = tuple(k.shape[1] for k, _ in segs)
    const = lambda arr: pl.BlockSpec(arr.shape, lambda bi, h, *i: (0, 0))
    if shifted:
        tq = min(tq, SHIFTED_TQ)
        grid = (b, N_HEADS, lq // tq)
        q_spec = pl.BlockSpec((None, HEAD_W, tq), lambda bi, h, i: (bi, h, i))
        o_spec = pl.BlockSpec((None, tq, HEAD_W), lambda bi, h, i: (bi, i, h))
        body = functools.partial(_attn_kernel, n_seg=len(segs), seg_len=seg_len, tk=tk, lam_init=lam_init)
        scratch = [pltpu.VMEM((2, 1, tq), F32), pltpu.VMEM((2, 1, tq), F32), pltpu.VMEM((2, HEAD_W, tq), F32)]
        semantics = ("parallel", "parallel", "arbitrary")
    else:
        grid = (b, N_HEADS)
        q_spec = pl.BlockSpec((None, HEAD_W, lq), lambda bi, h: (bi, h, 0))
        o_spec = pl.BlockSpec((None, lq, HEAD_W), lambda bi, h: (bi, 0, h))
        body = functools.partial(_attn_unshifted_kernel, n_seg=len(segs), seg_len=seg_len, tq=tq, tk=tk,
                                 lam_init=lam_init)
        s_rows = min(tk, max(seg_len))
        scratch = [pltpu.VMEM((8, 2 * tq), F32), pltpu.VMEM((HEAD_W, 2 * tq), F32),
                   pltpu.VMEM((s_rows, 2 * tq), F32), pltpu.VMEM((s_rows, 2 * tq), F32)]
        semantics = ("parallel", "parallel")
    in_specs, args = [q_spec], [qt]
    for (k, vt), lk in zip(segs, seg_len):
        in_specs.append(pl.BlockSpec((None, lk, HEAD_W), lambda bi, h, *i: (bi, 0, h)))
        in_specs.append(pl.BlockSpec((None, HEAD_W, lk), lambda bi, h, *i: (bi, h, 0)))
        args += [k, vt]
    in_specs += [const(lam_vecs), const(subln_col)]
    args += [lam_vecs, subln_col]
    return pl.pallas_call(
        body,
        grid=grid,
        in_specs=in_specs,
        out_specs=o_spec,
        out_shape=jax.ShapeDtypeStruct((b, lq, ATTN_W), BF16),
        scratch_shapes=scratch,
        compiler_params=_cparams(*semantics),
        name="diff_attention_shifted" if shifted else "diff_attention",
    )(*args)


def _attention_dispatch(score_bound, *args, **kw):
    return lax.cond(score_bound <= MAX_UNSHIFTED_SCORE,
                    lambda: _attention(*args, shifted=False, **kw),
                    lambda: _attention(*args, shifted=True, **kw))


def _fourier_kernel(u_ref, v_ref, m1_ref, c2_ref, s2_ref, o_ref, yr_ref, yi_ref, *, n1, n2):
    def stage1(j, carry):
        xu = u_ref[pl.ds(j, n1, stride=n2), :]
        xv = v_ref[pl.ds(j, n1, stride=n2), :]
        x = jnp.concatenate([xu, xv], axis=1).astype(BF16)
        p = jnp.dot(m1_ref[j], x, preferred_element_type=F32)
        rows = pl.ds(pl.multiple_of(j * n1, n1), n1)
        yr_ref[rows, :] = p[:n1, :LANES] + p[n1:, LANES:]
        yi_ref[rows, :] = p[n1:, :LANES] - p[:n1, LANES:]
        return carry

    lax.fori_loop(0, n2, stage1, 0, unroll=FOURIER_UNROLL)
    c2, s2 = c2_ref[...], s2_ref[...]

    def stage2(k1, carry):
        rows = pl.ds(k1, n2, stride=n1)
        xr = (jnp.dot(c2, yr_ref[rows, :].astype(BF16), preferred_element_type=F32)
              + jnp.dot(s2, yi_ref[rows, :].astype(BF16), preferred_element_type=F32))
        o_ref[rows, :] = xr
        return carry

    lax.fori_loop(0, n1, stage2, 0, unroll=FOURIER_UNROLL)


def _fourier_latent(uv, n1, n2):
    b, l, _ = uv.shape
    m1, c2, s2 = _stage_tables(n1, n2)
    m1, c2, s2 = (jnp.asarray(t).astype(BF16) for t in (m1, c2, s2))
    halves = MIX_Q // LANES
    return pl.pallas_call(
        functools.partial(_fourier_kernel, n1=n1, n2=n2),
        grid=(b, halves),
        in_specs=[
            pl.BlockSpec((None, l, LANES), lambda bi, c: (bi, 0, c)),
            pl.BlockSpec((None, l, LANES), lambda bi, c: (bi, 0, c + halves)),
            pl.BlockSpec(m1.shape, lambda bi, c: (0, 0, 0)),
            pl.BlockSpec(c2.shape, lambda bi, c: (0, 0)),
            pl.BlockSpec(s2.shape, lambda bi, c: (0, 0)),
        ],
        out_specs=pl.BlockSpec((None, l, LANES), lambda bi, c: (bi, 0, c)),
        out_shape=jax.ShapeDtypeStruct((b, l, MIX_Q), F32),
        scratch_shapes=[pltpu.VMEM((l, LANES), F32), pltpu.VMEM((l, LANES), F32)],
        compiler_params=_cparams("parallel", "parallel"),
        name="fourier_positions",
    )(uv, uv, m1, c2, s2)


def _fourier_dense_kernel(uv_ref, c_ref, s_ref, o_ref):
    uv = uv_ref[...].astype(BF16)
    o_ref[...] = (jnp.dot(c_ref[...], uv[:, :MIX_Q], preferred_element_type=F32)
                  - jnp.dot(s_ref[...], uv[:, MIX_Q:], preferred_element_type=F32))


def _fourier_dense(uv):
    b, l, _ = uv.shape
    n = np.arange(l)
    ang = 2.0 * np.pi * np.outer(n, n) / l
    c = jnp.asarray((np.cos(ang) / math.sqrt(l)).astype(np.float32)).astype(BF16)
    s = jnp.asarray((np.sin(ang) / math.sqrt(l)).astype(np.float32)).astype(BF16)
    return pl.pallas_call(
        _fourier_dense_kernel,
        grid=(b,),
        in_specs=[pl.BlockSpec((None, l, 2 * MIX_Q), lambda bi: (bi, 0, 0)),
                  pl.BlockSpec((l, l), lambda bi: (0, 0)),
                  pl.BlockSpec((l, l), lambda bi: (0, 0))],
        out_specs=pl.BlockSpec((None, l, MIX_Q), lambda bi: (bi, 0, 0)),
        out_shape=jax.ShapeDtypeStruct((b, l, MIX_Q), F32),
        compiler_params=_cparams("parallel"),
        name="fourier_positions_ctx",
    )(uv, c, s)


POOL_HALO = 8


def _pool_kernel(b_ref, o_ref, pad_ref, *, l, rows):
    zeros = jnp.zeros((POOL_HALO, LANES), F32)
    lane = lax.broadcasted_iota(jnp.int32, (rows, LANES), 1)
    low_group = lane < GROUP_W
    n_chunks = l // rows
    for tile in range(MIX_Q // LANES):
        w_lo, w_hi = POOL_WINDOWS[2 * tile], POOL_WINDOWS[2 * tile + 1]
        pad_ref[0:POOL_HALO, :] = zeros
        pad_ref[POOL_HALO + l:, :] = zeros
        pad_ref[POOL_HALO:POOL_HALO + l, :] = b_ref[:, tile * LANES:(tile + 1) * LANES]

        def chunk(t0, edge, tile=tile, w_lo=w_lo, w_hi=w_hi):
            shifted = lambda dlt: pad_ref[pl.ds(t0 + POOL_HALO + dlt, rows), :]
            lo_range = range(-(w_lo // 2), w_lo // 2)
            inner = functools.reduce(lambda a, b: a + b, [shifted(d) for d in lo_range])
            outer = functools.reduce(lambda a, b: a + b,
                                     [shifted(d) for d in range(-(w_hi // 2), w_hi // 2) if d not in lo_range], inner)
            sums = jnp.where(low_group, inner, outer)
            if edge:
                pos = t0 + lax.broadcasted_iota(jnp.int32, (rows, LANES), 0)
                half = jnp.where(low_group, w_lo // 2, w_hi // 2)
                cnt = jnp.minimum(pos + (half - 1), l - 1) - jnp.maximum(pos - half, 0) + 1
                mean = sums / cnt.astype(F32)
            else:
                mean = sums * jnp.where(low_group, 1.0 / w_lo, 1.0 / w_hi)
            o_ref[pl.ds(t0, rows), tile * LANES:(tile + 1) * LANES] = mean - shifted(0)

        assert rows >= POOL_HALO
        chunk(0, True)
        if n_chunks > 1:
            def body(ci, carry, chunk=chunk):
                chunk(pl.multiple_of(ci * rows, rows), False)
                return carry

            lax.fori_loop(1, n_chunks - 1, body, 0)
            chunk((n_chunks - 1) * rows, True)


def _pool(bp):
    b, l, _ = bp.shape
    rows = 256
    return pl.pallas_call(
        functools.partial(_pool_kernel, l=l, rows=rows),
        grid=(b,),
        in_specs=[pl.BlockSpec((None, l, MIX_Q), lambda bi: (bi, 0, 0))],
        out_specs=pl.BlockSpec((None, l, MIX_Q), lambda bi: (bi, 0, 0)),
        out_shape=jax.ShapeDtypeStruct((b, l, MIX_Q), F32),
        scratch_shapes=[pltpu.VMEM((l + 2 * POOL_HALO, LANES), F32)],
        compiler_params=_cparams("parallel"),
        name="pool_windows",
    )(bp)


def _outproj_kernel(x_ref, f_ref, p_ref, a_ref, sg_ref, mod_ref, w_ref, o_ref, *, d, ctx_row):
    row = ctx_row if ctx_row is not None else pl.program_id(0)
    o_ref[...] = _residual_update(x_ref, f_ref, p_ref, a_ref, sg_ref, mod_ref, w_ref, row, d)


def _out_projection(x, four, pool, attn, sg, mod, w_out, *, layer, ctx_row, tl):
    b, l, d = x.shape
    row_blk = lambda width: pl.BlockSpec((None, tl, width), lambda bi, i: (bi, i, 0))
    of_layer = lambda arr: pl.BlockSpec((None,) + arr.shape[1:], lambda bi, i: (layer,) + (0,) * (arr.ndim - 1))
    return pl.pallas_call(
        functools.partial(_outproj_kernel, d=d, ctx_row=ctx_row),
        grid=(b, l // tl),
        in_specs=[row_blk(d), row_blk(MIX_Q), row_blk(MIX_Q), row_blk(ATTN_W), row_blk(d), of_layer(mod),
                  of_layer(w_out)],
        out_specs=row_blk(d),
        out_shape=jax.ShapeDtypeStruct((b, l, d), F32),
        compiler_params=_cparams("parallel", "parallel"),
        name="out_projection_ctx" if ctx_row is not None else "out_projection",
    )(x, four, pool, attn, sg, mod, w_out)


def kernel(x, c, ctx, c_ctx, norm_g, w_mod, b_mod, w_in, w_fourier, w_pool, pool_scale, qk_norm_g, lam_vecs,
           subln_g, w_out):
    b, l, d = x.shape
    depth = w_in.shape[0]
    ctx_len = ctx.shape[1]
    ctx_row = b

    cc = jnp.concatenate([c, c_ctx[None, :], jnp.zeros((8 - b - 1, d), F32)], axis=0)
    mod = _modulation(cc, w_mod, b_mod)
    w_eff = _fold_weights(w_in, w_fourier, w_pool, pool_scale)
    w_out_bf = _cast_bf16(w_out)
    rope_tabs = _rope_tables(l // GRID_W)
    lane_chunk = lax.broadcasted_iota(jnp.int32, (2 * LANES, 2 * LANES), 0) // QK_DIM
    ones_blk = (lane_chunk == lane_chunk.T).astype(BF16)
    n1 = 64
    n2 = l // n1

    pending = None
    for layer in range(depth):
        lam_init = 0.8 - 0.6 * math.exp(-0.3 * layer)
        update_ctx = layer < depth - 1
        ng = norm_g[layer][None, :]
        qk_gain = jnp.tile(qk_norm_g[layer], (1, 2 * LANES // QK_DIM))
        sub_g = subln_g[layer][:, None]
        lv = lam_vecs[layer]

        outs = _in_projection(x, mod, ng, w_eff, qk_gain, ones_blk, rope_tabs, layer=layer, ctx_row=None,
                              tl=512, prev=pending)
        if pending is not None:
            x, outs = outs[0], outs[1:]
        uv, bp, qt, k, vt, sg = outs
        uv_c, bp_c, qt_c, k_c, vt_c, sg_c = _in_projection(ctx, mod, ng, w_eff, qk_gain, ones_blk, None,
                                                           layer=layer, ctx_row=ctx_row, tl=ctx_len)
        gains = jnp.max(jnp.abs(qk_norm_g[layer]), axis=-1)
        score_bound = QK_DIM * gains[0] * gains[1] * (Q_SCALE * BF16_ROUNDING_SLACK)
        attn = _attention_dispatch(score_bound, qt, [(k_c, vt_c), (k, vt)], lv, sub_g, lam_init, tq=ATTN_TQ,
                                  tk=ATTN_TK)
        four = _fourier_latent(uv, n1, n2)
        pool = _pool(bp)
        if update_ctx:
            attn_c = _attention_dispatch(score_bound, qt_c, [(k_c, vt_c)], lv, sub_g, lam_init, tq=ctx_len, tk=512)
            ctx = _out_projection(ctx, _fourier_dense(uv_c), _pool(bp_c), attn_c, sg_c, mod, w_out_bf,
                                  layer=layer, ctx_row=ctx_row, tl=ctx_len)
        if layer + 1 < depth:
            pending = (four, pool, attn, sg, w_out_bf)
        else:
            x = _out_projection(x, four, pool, attn, sg, mod, w_out_bf, layer=layer, ctx_row=None, tl=1024)
    return x
```
